```python
import math
import jax, jax.numpy as jnp
from jax import lax
import numpy as np

D_MODEL = 1024
BATCH = 8
SEQ = 2048
DEPTH = 2
DEC_BATCH = 128
DEC_SEQ = 8
PAST_LEN = 16384
PAGE_SIZE = 128

N_EVEN = (DEPTH + 1) // 2
N_ODD = DEPTH // 2
RWKV_HEADS = 8
RWKV_HEAD = 64
RWKV_W = RWKV_HEADS * RWKV_HEAD
DECAY_LORA = 64
AAA_LORA = 64
GATE_LORA = 128
RWKV_PROJ = 3 * RWKV_W + DECAY_LORA + AAA_LORA + GATE_LORA
RWKV_SPLITS = [RWKV_W, 2 * RWKV_W, 3 * RWKV_W, 3 * RWKV_W + DECAY_LORA, 3 * RWKV_W + DECAY_LORA + AAA_LORA]
GN_EPS = 64e-5
GLA_HEADS = 4
GLA_DK = 64
GLA_DV = 128
GLA_K = GLA_HEADS * GLA_DK
GLA_V = GLA_HEADS * GLA_DV
GLA_GATE_LORA = 16
GLA_TAU = 16.0
GLA_CHUNK = 32
GLA_PROJ = 2 * GLA_K + 2 * GLA_V + GLA_GATE_LORA
GLA_SPLITS = [GLA_K, 2 * GLA_K, 2 * GLA_K + GLA_V, 2 * GLA_K + GLA_V + GLA_GATE_LORA]
MIX_IN = RWKV_PROJ + GLA_PROJ
MIX_OUT = RWKV_W + GLA_V
S5_WIDTH = D_MODEL
S5_GROUP = 16
S5_GROUPS = S5_WIDTH // S5_GROUP
S5_STATE = 64
DT_MIN = 1e-3
DT_MAX = 1e-1
D_FF = 4 * D_MODEL
NORM_EPS = 1e-6

kernel_name = 'hybrid_rwkv7_gla_s5_decode_step'


def rmsnorm(x, g):
    xf = x.astype(jnp.float32)
    y = xf * lax.rsqrt(jnp.mean(xf * xf, axis=-1, keepdims=True) + NORM_EPS)
    return (y * g.astype(jnp.float32)).astype(x.dtype)


def rwkv7_mix(p, shift0, S0, mix, w0, w2, a0, a2, g2, k_k, k_a, r_k, gn_g, gn_b):
    B, T, _ = p.shape
    prev = jnp.concatenate([shift0[:, None, :].astype(jnp.float32), p[:, :-1]], axis=1)
    z = p + (prev - p) * mix
    r, k, v, xw, xa, xg = jnp.split(z, RWKV_SPLITS, axis=-1)
    w = -jax.nn.softplus(-(w0 + jnp.tanh(xw) @ w2)) - 0.5
    decay = jnp.exp(-jnp.exp(w))
    a = jax.nn.sigmoid(a0 + xa @ a2)
    g = jax.nn.sigmoid(xg) @ g2
    heads = lambda t: t.reshape(B, T, RWKV_HEADS, RWKV_HEAD)
    kk = heads(k * k_k)
    kk = kk * lax.rsqrt(jnp.maximum(jnp.sum(kk * kk, axis=-1, keepdims=True), 1e-24))
    k = k * (1.0 + (a - 1.0) * k_a)
    rh, kh, vh, wh, ah = heads(r), heads(k), heads(v), heads(decay), heads(a)

    def step(S, inp):
        r_t, w_t, k_t, v_t, kk_t, a_t = inp
        s_kk = jnp.einsum('bhij,bhj->bhi', S, kk_t)
        S = (S * w_t[:, :, None, :]
             - s_kk[..., None] * (kk_t * a_t)[:, :, None, :]
             + v_t[..., None] * k_t[:, :, None, :])
        return S, jnp.einsum('bhij,bhj->bhi', S, r_t)

    xs = tuple(jnp.moveaxis(t, 1, 0) for t in (rh, wh, kh, vh, kk, ah))
    S_T, y = lax.scan(step, S0.astype(jnp.float32), xs)
    y = jnp.moveaxis(y, 0, 1)
    mean_y = jnp.mean(y, axis=-1, keepdims=True)
    var_y = jnp.mean(jnp.square(y - mean_y), axis=-1, keepdims=True)
    yn = ((y - mean_y) * lax.rsqrt(var_y + GN_EPS)).reshape(B, T, RWKV_W) * gn_g + gn_b
    bonus = jnp.sum(rh * kh * r_k.reshape(RWKV_HEADS, RWKV_HEAD), axis=-1, keepdims=True) * vh
    out = (yn + bonus.reshape(B, T, RWKV_W)) * g
    return out, p[:, -1], S_T


def gla_chunked(q, k, v, gk, S0):
    B, T, H, DK = q.shape
    DV = v.shape[-1]
    C = min(GLA_CHUNK, T)
    pad = (-T) % C
    if pad:
        padt = lambda t: jnp.pad(t, ((0, 0), (0, pad), (0, 0), (0, 0)))
        q, k, v, gk = padt(q), padt(k), padt(v), padt(gk)
    Tp = T + pad
    n = Tp // C
    blk = lambda t: t.reshape(B, n, C, H, t.shape[-1]).transpose(1, 0, 2, 3, 4)
    qc, kc, vc, gc = blk(q), blk(k), blk(v), blk(gk)
    bc = jnp.cumsum(gc, axis=2)
    b_last = bc[:, :, -1]
    q_t = qc * jnp.exp(bc)
    k_t = kc * jnp.exp(-bc)
    k_s = kc * jnp.exp(b_last[:, :, None] - bc)
    mask = jnp.tril(jnp.ones((C, C), dtype=bool))
    att = jnp.where(mask, jnp.einsum('nbihd,nbjhd->nbhij', q_t, k_t), 0.0)
    o_intra = jnp.einsum('nbhij,nbjhv->nbihv', att, vc)

    def step(S, inp):
        q_n, k_n, v_n, bl = inp
        o = jnp.einsum('bihd,bhdv->bihv', q_n, S)
        S = jnp.exp(bl)[..., None] * S + jnp.einsum('bjhd,bjhv->bhdv', k_n, v_n)
        return S, o

    S_T, o_inter = lax.scan(step, S0, (q_t, k_s, vc, b_last))
    o = (o_intra + o_inter).transpose(1, 0, 2, 3, 4).reshape(B, Tp, H, DV)[:, :T]
    return o, S_T


def gla_mix(p, S0, wa2, ba, g_norm):
    B, T, _ = p.shape
    q, k, v, xa, gz = jnp.split(p, GLA_SPLITS, axis=-1)
    gk = jax.nn.log_sigmoid(xa @ wa2 + ba) / GLA_TAU
    q = q * GLA_DK ** -0.5
    hk = lambda t: t.reshape(B, T, GLA_HEADS, GLA_DK)
    o, S_T = gla_chunked(hk(q), hk(k), v.reshape(B, T, GLA_HEADS, GLA_DV), hk(gk), S0.astype(jnp.float32))
    o = o * lax.rsqrt(jnp.mean(o * o, axis=-1, keepdims=True) + NORM_EPS) * g_norm
    o = o.reshape(B, T, GLA_V) * jax.nn.silu(gz)
    return o, S_T


def _cplx_affine_combine(left, right):
    ar1, ai1, br1, bi1 = left
    ar2, ai2, br2, bi2 = right
    return (ar2 * ar1 - ai2 * ai1, ar2 * ai1 + ai2 * ar1,
            ar2 * br1 - ai2 * bi1 + br2, ar2 * bi1 + ai2 * br1 + bi2)


def s5_scan(u, x0_re, x0_im, lam_re, lam_im, log_dt, b_re, b_im, c_re, c_im, d_skip):
    B, T, E = u.shape
    uf = u.astype(jnp.float32)
    ug = uf.reshape(B, T, S5_GROUPS, S5_GROUP)
    dt = jnp.exp(log_dt.astype(jnp.float32))[:, None]
    lr = lam_re.astype(jnp.float32)
    li = lam_im.astype(jnp.float32)
    mag = jnp.exp(lr * dt)
    ang = li * dt
    ab_re, ab_im = mag * jnp.cos(ang), mag * jnp.sin(ang)
    den = lr * lr + li * li
    f_re = ((ab_re - 1.0) * lr + ab_im * li) / den
    f_im = (ab_im * lr - (ab_re - 1.0) * li) / den
    bb_re = f_re[..., None] * b_re - f_im[..., None] * b_im
    bb_im = f_re[..., None] * b_im + f_im[..., None] * b_re
    e_re = jnp.einsum('btgc,gpc->tbgp', ug, bb_re)
    e_im = jnp.einsum('btgc,gpc->tbgp', ug, bb_im)
    x0r = x0_re.astype(jnp.float32)
    x0i = x0_im.astype(jnp.float32)
    e_re = e_re.at[0].add(ab_re * x0r - ab_im * x0i)
    e_im = e_im.at[0].add(ab_re * x0i + ab_im * x0r)
    a_re = jnp.broadcast_to(ab_re, (T, 1) + ab_re.shape)
    a_im = jnp.broadcast_to(ab_im, (T, 1) + ab_im.shape)
    _, _, s_re, s_im = lax.associative_scan(_cplx_affine_combine, (a_re, a_im, e_re, e_im), axis=0)
    y = jnp.einsum('tbgp,gcp->btgc', s_re, c_re) - jnp.einsum('tbgp,gcp->btgc', s_im, c_im)
    y = y.reshape(B, T, E) + d_skip * uf
    return y, s_re[-1], s_im[-1]


def even_mixer(h, shift0, wkv0, gla0, prm, i):
    p = jnp.matmul(h, prm['w_mix_in'][i]).astype(jnp.float32)
    o_r, sh, wkv = rwkv7_mix(p[..., :RWKV_PROJ], shift0, wkv0, prm['rwkv_mu'][i], prm['rwkv_w0'][i],
                             prm['rwkv_w2'][i], prm['rwkv_a0'][i], prm['rwkv_a2'][i], prm['rwkv_g2'][i],
                             prm['rwkv_k_k'][i], prm['rwkv_k_a'][i], prm['rwkv_r_k'][i],
                             prm['rwkv_gn_g'][i], prm['rwkv_gn_b'][i])
    o_g, gl = gla_mix(p[..., RWKV_PROJ:], gla0, prm['gla_wa2'][i], prm['gla_ba'][i], prm['gla_norm'][i])
    o = jnp.concatenate([o_r, o_g], axis=-1).astype(h.dtype)
    return jnp.matmul(o, prm['w_mix_out'][i]), sh, wkv, gl


def odd_mixer(h, s_re0, s_im0, prm, i):
    u = jnp.matmul(h, prm['s5_w_in'][i])
    y, s_re, s_im = s5_scan(u, s_re0, s_im0, prm['s5_lam_re'][i], prm['s5_lam_im'][i], prm['s5_log_dt'][i],
                            prm['s5_b_re'][i], prm['s5_b_im'][i], prm['s5_c_re'][i], prm['s5_c_im'][i],
                            prm['s5_d'][i])
    zo = jnp.matmul(jax.nn.gelu(y).astype(h.dtype), prm['s5_w_out'][i])
    out = zo[..., :D_MODEL] * jax.nn.sigmoid(zo[..., D_MODEL:])
    return out, s_re, s_im


def trunk(x, st_shift, st_wkv, st_gla, st_re, st_im, prm):
    new_shift, new_wkv, new_gla, new_re, new_im = [], [], [], [], []
    for layer in range(DEPTH):
        i = layer // 2
        h = rmsnorm(x, prm['norm_mix_pre'][layer])
        if layer % 2 == 0:
            m, sh, wkv, gl = even_mixer(h, st_shift[i], st_wkv[i], st_gla[i], prm, i)
            new_shift.append(sh)
            new_wkv.append(wkv)
            new_gla.append(gl)
        else:
            m, sre, sim = odd_mixer(h, st_re[i], st_im[i], prm, i)
            new_re.append(sre)
            new_im.append(sim)
        x = x + rmsnorm(m, prm['norm_mix_post'][layer])
        h = rmsnorm(x, prm['norm_ffn_pre'][layer])
        f = jnp.matmul(jnp.square(jax.nn.relu(jnp.matmul(h, prm['w_ff_up'][layer]))), prm['w_ff_down'][layer])
        x = x + rmsnorm(f, prm['norm_ffn_post'][layer])
    st = lambda lst: jnp.stack(lst).astype(x.dtype)
    return x, st(new_shift), st(new_wkv), st(new_gla), st(new_re), st(new_im)


def setup_inputs(seed: int = 0) -> dict:
    key = jax.random.key(seed)
    keys = iter(jax.random.split(key, 64))

    def nrm(shape, scale):
        return jax.random.normal(next(keys), shape, jnp.float32) * scale

    def gain(shape):
        return 1.0 + nrm(shape, 0.02)

    def unif(shape, lo, hi):
        return jax.random.uniform(next(keys), shape, jnp.float32, minval=lo, maxval=hi)

    E, L = N_EVEN, N_ODD
    G, P = S5_GROUPS, S5_STATE
    inp = {}
    inp['x_prompt'] = nrm((BATCH, SEQ, D_MODEL), 1.0)
    inp['x_sample'] = nrm((DEC_BATCH, DEC_SEQ, D_MODEL), 1.0)
    inp['state_rwkv_shift'] = nrm((E, DEC_BATCH, RWKV_PROJ), 1.0)
    inp['state_rwkv_wkv'] = nrm((E, DEC_BATCH, RWKV_HEADS, RWKV_HEAD, RWKV_HEAD), 0.5)
    inp['state_gla'] = nrm((E, DEC_BATCH, GLA_HEADS, GLA_DK, GLA_DV), 0.5)
    inp['state_s5_re'] = nrm((L, DEC_BATCH, G, P), 0.5)
    inp['state_s5_im'] = nrm((L, DEC_BATCH, G, P), 0.5)
    inp['norm_mix_pre'] = gain((DEPTH, D_MODEL))
    inp['norm_mix_post'] = gain((DEPTH, D_MODEL))
    inp['norm_ffn_pre'] = gain((DEPTH, D_MODEL))
    inp['norm_ffn_post'] = gain((DEPTH, D_MODEL))
    inp['w_mix_in'] = nrm((E, D_MODEL, MIX_IN), D_MODEL ** -0.5)
    inp['w_mix_out'] = nrm((E, MIX_OUT, D_MODEL), MIX_OUT ** -0.5)
    inp['rwkv_mu'] = unif((E, RWKV_PROJ), 0.0, 1.0)
    inp['rwkv_w0'] = unif((E, RWKV_W), -6.0, -1.0)
    inp['rwkv_w2'] = nrm((E, DECAY_LORA, RWKV_W), 0.1)
    inp['rwkv_a0'] = nrm((E, RWKV_W), 0.1)
    inp['rwkv_a2'] = nrm((E, AAA_LORA, RWKV_W), 0.1)
    inp['rwkv_g2'] = nrm((E, GATE_LORA, RWKV_W), GATE_LORA ** -0.5)
    inp['rwkv_k_k'] = 0.85 + nrm((E, RWKV_W), 0.02)
    inp['rwkv_k_a'] = 1.0 + nrm((E, RWKV_W), 0.02)
    inp['rwkv_r_k'] = nrm((E, RWKV_W), 0.1)
    inp['rwkv_gn_g'] = gain((E, RWKV_W))
    inp['rwkv_gn_b'] = nrm((E, RWKV_W), 0.02)
    inp['gla_wa2'] = nrm((E, GLA_GATE_LORA, GLA_K), GLA_GATE_LORA ** -0.5)
    inp['gla_ba'] = nrm((E, GLA_K), 0.1)
    inp['gla_norm'] = gain((E, GLA_DV))
    inp['s5_w_in'] = nrm((L, D_MODEL, S5_WIDTH), D_MODEL ** -0.5)
    inp['s5_lam_re'] = -0.5 + nrm((L, G, P), 0.01)
    inp['s5_lam_im'] = math.pi * jnp.arange(P, dtype=jnp.float32)[None, None, :] + nrm((L, G, P), 0.01)
    inp['s5_log_dt'] = unif((L, G), math.log(DT_MIN), math.log(DT_MAX))
    inp['s5_b_re'] = nrm((L, G, P, S5_GROUP), (2.0 * S5_GROUP) ** -0.5)
    inp['s5_b_im'] = nrm((L, G, P, S5_GROUP), (2.0 * S5_GROUP) ** -0.5)
    inp['s5_c_re'] = nrm((L, G, S5_GROUP, P), (2.0 * P) ** -0.5)
    inp['s5_c_im'] = nrm((L, G, S5_GROUP, P), (2.0 * P) ** -0.5)
    inp['s5_d'] = nrm((L, S5_WIDTH), 1.0)
    inp['s5_w_out'] = nrm((L, S5_WIDTH, 2 * D_MODEL), S5_WIDTH ** -0.5)
    inp['w_ff_up'] = nrm((DEPTH, D_MODEL, D_FF), D_MODEL ** -0.5)
    inp['w_ff_down'] = nrm((DEPTH, D_FF, D_MODEL), D_FF ** -0.5)
    return inp


def reference(x_prompt, x_sample, state_rwkv_shift, state_rwkv_wkv, state_gla, state_s5_re, state_s5_im,
              norm_mix_pre, norm_mix_post, norm_ffn_pre, norm_ffn_post,
              w_mix_in, w_mix_out, rwkv_mu, rwkv_w0, rwkv_w2, rwkv_a0, rwkv_a2, rwkv_g2,
              rwkv_k_k, rwkv_k_a, rwkv_r_k, rwkv_gn_g, rwkv_gn_b, gla_wa2, gla_ba, gla_norm,
              s5_w_in, s5_lam_re, s5_lam_im, s5_log_dt, s5_b_re, s5_b_im, s5_c_re, s5_c_im, s5_d, s5_w_out,
              w_ff_up, w_ff_down):
    prm = dict(norm_mix_pre=norm_mix_pre, norm_mix_post=norm_mix_post, norm_ffn_pre=norm_ffn_pre,
               norm_ffn_post=norm_ffn_post, w_mix_in=w_mix_in, w_mix_out=w_mix_out, rwkv_mu=rwkv_mu,
               rwkv_w0=rwkv_w0, rwkv_w2=rwkv_w2, rwkv_a0=rwkv_a0, rwkv_a2=rwkv_a2, rwkv_g2=rwkv_g2,
               rwkv_k_k=rwkv_k_k, rwkv_k_a=rwkv_k_a, rwkv_r_k=rwkv_r_k, rwkv_gn_g=rwkv_gn_g,
               rwkv_gn_b=rwkv_gn_b, gla_wa2=gla_wa2, gla_ba=gla_ba, gla_norm=gla_norm,
               s5_w_in=s5_w_in, s5_lam_re=s5_lam_re, s5_lam_im=s5_lam_im, s5_log_dt=s5_log_dt,
               s5_b_re=s5_b_re, s5_b_im=s5_b_im, s5_c_re=s5_c_re, s5_c_im=s5_c_im, s5_d=s5_d,
               s5_w_out=s5_w_out, w_ff_up=w_ff_up, w_ff_down=w_ff_down)
    Bp = x_prompt.shape[0]
    dt_p = x_prompt.dtype
    z_shift = jnp.zeros((N_EVEN, Bp) + state_rwkv_shift.shape[2:], dt_p)
    z_wkv = jnp.zeros((N_EVEN, Bp) + state_rwkv_wkv.shape[2:], dt_p)
    z_gla = jnp.zeros((N_EVEN, Bp) + state_gla.shape[2:], dt_p)
    z_re = jnp.zeros((N_ODD, Bp) + state_s5_re.shape[2:], dt_p)
    z_im = jnp.zeros((N_ODD, Bp) + state_s5_im.shape[2:], dt_p)
    y_prompt, sh_p, wkv_p, gla_p, re_p, im_p = trunk(x_prompt, z_shift, z_wkv, z_gla, z_re, z_im, prm)
    y_sample, sh_s, wkv_s, gla_s, re_s, im_s = trunk(x_sample, state_rwkv_shift, state_rwkv_wkv, state_gla,
                                                     state_s5_re, state_s5_im, prm)
    return (y_prompt, y_sample, sh_p, sh_s, wkv_p, wkv_s, gla_p, gla_s, re_p, re_s, im_p, im_s)
```

```python
import functools
import math

import jax
import jax.numpy as jnp
from jax import lax
from jax.experimental import pallas as pl
from jax.experimental.pallas import tpu as pltpu

F32 = jnp.float32
BF16 = jnp.bfloat16

D_MODEL = 1024
D_FF = 4096
NORM_EPS = 1e-6
RWKV_HEADS = 8
RWKV_HEAD = 64
RWKV_W = RWKV_HEADS * RWKV_HEAD
RWKV_PROJ = 1792
RWKV_CHUNK = 64
GN_EPS = 64e-5
GLA_HEADS = 4
GLA_DK = 64
GLA_DV = 128
GLA_K = GLA_HEADS * GLA_DK
GLA_V = GLA_HEADS * GLA_DV
GLA_LORA = 16
GLA_TAU = 16.0
GLA_CHUNK = 32
GLA_MAIN = 2 * GLA_K + 2 * GLA_V
GLA_TILE = 256
S5_GROUPS = 64
S5_GROUP = 16
S5_STATE = 64
S5_CHUNK = 16
S5_PAIRS = S5_GROUPS // 2
LANES = 128
VMEM_LIMIT = 48 * 1024 * 1024

_NN = (((1,), (0,)), ((), ()))
_NT = (((1,), (1,)), ((), ()))
_TN = (((0,), (0,)), ((), ()))


def _dg(a, b, dn=_NN):
    return lax.dot_general(a.astype(BF16), b.astype(BF16), dn, preferred_element_type=F32)


def _split2(x):
    hi = x.astype(BF16)
    lo = (x - hi.astype(F32)).astype(BF16)
    return hi, lo


def _split3(x):
    hi = x.astype(BF16)
    r1 = x - hi.astype(F32)
    mid = r1.astype(BF16)
    lo = (r1 - mid.astype(F32)).astype(BF16)
    return hi, mid, lo


def _dg3(a, b, dn=_NN):
    ah, al = _split2(a)
    bh, bl = _split2(b)
    d = lambda x, y: lax.dot_general(x, y, dn, preferred_element_type=F32)
    return d(ah, bh) + (d(ah, bl) + d(al, bh))


def _dg_sel(sel, x, dn=_NN):
    d = lambda y: lax.dot_general(sel, y, dn, preferred_element_type=F32)
    h, m, l = _split3(x)
    return d(h) + (d(m) + d(l))


def _dg_selr(x, sel, dn=_NN):
    d = lambda y: lax.dot_general(y, sel, dn, preferred_element_type=F32)
    h, m, l = _split3(x)
    return d(h) + (d(m) + d(l))


def _rms(x, g):
    return x * lax.rsqrt(jnp.mean(x * x, axis=-1, keepdims=True) + NORM_EPS) * g


def _row_tile(m, want):
    t = min(m, want)
    assert m % t == 0, (m, t)
    return t


def _full(shape):
    nd = len(shape)
    return pl.BlockSpec(shape, lambda *_: (0,) * nd)


def _norm_proj_kernel(x_ref, g_ref, *refs):
    n = len(refs) // 2
    h = _rms(x_ref[...], g_ref[...]).astype(BF16)
    for w_ref, o_ref in zip(refs[:n], refs[n:]):
        o_ref[...] = jnp.dot(h, w_ref[...], preferred_element_type=F32).astype(o_ref.dtype)


def norm_proj(x, g, ws, out_dtypes, tm=512):
    m, d = x.shape
    tm = _row_tile(m, tm)
    return pl.pallas_call(
        _norm_proj_kernel,
        grid=(m // tm,),
        in_specs=[pl.BlockSpec((tm, d), lambda i: (i, 0)), _full((1, d))]
        + [_full(w.shape) for w in ws],
        out_specs=[pl.BlockSpec((tm, w.shape[1]), lambda i: (i, 0)) for w in ws],
        out_shape=[jax.ShapeDtypeStruct((m, w.shape[1]), dt) for w, dt in zip(ws, out_dtypes)],
        compiler_params=pltpu.CompilerParams(dimension_semantics=("parallel",), vmem_limit_bytes=VMEM_LIMIT),
        name="norm_proj",
    )(x, g.reshape(1, d), *ws)


def _proj_post_kernel(*refs, n_in, glu):
    a_refs = refs[:n_in]
    w_refs = refs[n_in:2 * n_in]
    g_ref, x_ref, o_ref = refs[2 * n_in:]
    m = None
    for a_ref, w_ref in zip(a_refs, w_refs):
        t = jnp.dot(a_ref[...].astype(BF16), w_ref[...], preferred_element_type=F32)
        m = t if m is None else m + t
    if glu:
        m = m[:, :D_MODEL] * jax.nn.sigmoid(m[:, D_MODEL:])
    o_ref[...] = x_ref[...] + _rms(m, g_ref[...])


def proj_post(a_list, w_list, g, x, glu=False, tm=512):
    m, d = x.shape
    tm = _row_tile(m, tm)
    n_in = len(a_list)
    return pl.pallas_call(
        functools.partial(_proj_post_kernel, n_in=n_in, glu=glu),
        grid=(m // tm,),
        in_specs=[pl.BlockSpec((tm, a.shape[1]), lambda i: (i, 0)) for a in a_list]
        + [_full(w.shape) for w in w_list]
        + [_full((1, d)), pl.BlockSpec((tm, d), lambda i: (i, 0))],
        out_specs=pl.BlockSpec((tm, d), lambda i: (i, 0)),
        out_shape=jax.ShapeDtypeStruct((m, d), F32),
        compiler_params=pltpu.CompilerParams(dimension_semantics=("parallel",), vmem_limit_bytes=VMEM_LIMIT),
        name="proj_post",
    )(*a_list, *w_list, g.reshape(1, d), x)


def _ffn_kernel(x_ref, gpre_ref, wup_ref, wdn_ref, gpost_ref, o_ref, h_scr, acc_scr):
    j = pl.program_id(1)

    @pl.when(j == 0)
    def _():
        h_scr[...] = _rms(x_ref[...], gpre_ref[...]).astype(BF16)
        acc_scr[...] = jnp.zeros_like(acc_scr)

    u = jnp.dot(h_scr[...], wup_ref[...], preferred_element_type=F32)
    u = jnp.square(jnp.maximum(u, 0.0)).astype(BF16)
    acc_scr[...] += jnp.dot(u, wdn_ref[...], preferred_element_type=F32)

    @pl.when(j == pl.num_programs(1) - 1)
    def _():
        o_ref[...] = x_ref[...] + _rms(acc_scr[...], gpost_ref[...])


def ffn(x, g_pre, w_up, w_down, g_post, tm=1024, tf=512):
    m, d = x.shape
    f = w_up.shape[1]
    tm = _row_tile(m, tm)
    return pl.pallas_call(
        _ffn_kernel,
        grid=(m // tm, f // tf),
        in_specs=[
            pl.BlockSpec((tm, d), lambda i, j: (i, 0)),
            _full((1, d)),
            pl.BlockSpec((d, tf), lambda i, j: (0, j)),
            pl.BlockSpec((tf, d), lambda i, j: (j, 0)),
            _full((1, d)),
        ],
        out_specs=pl.BlockSpec((tm, d), lambda i, j: (i, 0)),
        out_shape=jax.ShapeDtypeStruct((m, d), F32),
        scratch_shapes=[pltpu.VMEM((tm, d), BF16), pltpu.VMEM((tm, d), F32)],
        compiler_params=pltpu.CompilerParams(dimension_semantics=("parallel", "arbitrary"),
                                             vmem_limit_bytes=VMEM_LIMIT),
        name="ffn",
    )(x, g_pre.reshape(1, d), w_up, w_down, g_post.reshape(1, d))


def _head_sum(x, ones_bd):
    hi, lo = _split2(x)
    d = lambda y: jnp.dot(y, ones_bd, preferred_element_type=F32)
    return d(hi) + d(lo)


def _rwkv_kernel(p_ref, sh0_ref, s0_ref, mu_ref, w0_ref, w2_ref, a0_ref, a2_ref, g2_ref, kk_ref, ka_ref,
                 rk_ref, gng_ref, gnb_ref, ones_ref, o_ref, sout_ref, state_scr, carry_scr, y_scr, *, chunk):
    c = chunk
    n = pl.program_id(1)

    @pl.when(n == 0)
    def _():
        carry_scr[...] = sh0_ref[0]
        state_scr[...] = s0_ref[0]

    p = p_ref[0]
    row = lax.broadcasted_iota(jnp.int32, (c, 1), 0)
    prev = jnp.where(row == 0, carry_scr[...], pltpu.roll(p, 1, 0))
    carry_scr[...] = p[c - 1:c, :]
    z = p + (prev - p) * mu_ref[...]

    w_ = RWKV_W
    r = z[:, 0:w_]
    k = z[:, w_:2 * w_]
    v = z[:, 2 * w_:3 * w_]
    xw = z[:, 3 * w_:3 * w_ + 64]
    xa = z[:, 3 * w_ + 64:3 * w_ + 128]
    xg = z[:, 3 * w_ + 128:3 * w_ + 256]

    wlog = -jax.nn.softplus(-(w0_ref[...] + _dg(jnp.tanh(xw), w2_ref[...]))) - 0.5
    logd = -jnp.exp(wlog)
    a = jax.nn.sigmoid(a0_ref[...] + _dg(xa, a2_ref[...]))
    g = _dg(jax.nn.sigmoid(xg), g2_ref[...])

    ones_bd = ones_ref[...]
    kk = k * kk_ref[...]
    kk = kk * lax.rsqrt(jnp.maximum(_head_sum(kk * kk, ones_bd), 1e-24))
    k2 = k * (1.0 + (a - 1.0) * ka_ref[...])
    bb = kk * a

    ri = lax.broadcasted_iota(jnp.int32, (c, c), 0)
    ci = lax.broadcasted_iota(jnp.int32, (c, c), 1)
    lower = ri > ci
    lower_eq = ri >= ci
    eye = (ri == ci).astype(F32)
    cum = _dg_sel(lower_eq.astype(BF16), logd)
    g_in = jnp.exp(cum)
    g_ex = jnp.exp(cum - logd)
    g_inv = jnp.exp(-cum)
    last = cum[c - 1:c, :]
    g_hat = jnp.exp(last - cum)
    g_all = jnp.exp(last)
    rt = r * g_in
    kt = kk * g_ex
    ktil = k2 * g_inv
    btil = bb * g_inv
    khat = k2 * g_hat
    bhat = bb * g_hat

    for h in range(RWKV_HEADS):
        sl = slice(h * RWKV_HEAD, (h + 1) * RWKV_HEAD)
        s_prev = state_scr[h]
        kt_h, rt_h, v_h = kt[:, sl], rt[:, sl], v[:, sl]
        a_b = jnp.where(lower, _dg3(kt_h, btil[:, sl], _NT), 0.0)
        a_k = jnp.where(lower, _dg3(kt_h, ktil[:, sl], _NT), 0.0)
        a_rk = jnp.where(lower_eq, _dg3(rt_h, ktil[:, sl], _NT), 0.0)
        a_rb = jnp.where(lower_eq, _dg3(rt_h, btil[:, sl], _NT), 0.0)
        pw = -a_b
        tinv = eye + pw
        span = 2
        while span < c:
            pw = _dg3(pw, pw)
            tinv = tinv + _dg3(tinv, pw)
            span *= 2
        u = _dg3(tinv, _dg3(kt_h, s_prev, _NT) + _dg3(a_k, v_h))
        y_scr[:, sl] = _dg3(rt_h, s_prev, _NT) + _dg3(a_rk, v_h) - _dg3(a_rb, u)
        state_scr[h] = (s_prev * g_all[:, sl] + _dg3(v_h, khat[:, sl], _TN) - _dg3(u, bhat[:, sl], _TN))

    y = y_scr[...]
    inv_n = 1.0 / RWKV_HEAD
    mean = _head_sum(y, ones_bd) * inv_n
    yc = y - mean
    var = _head_sum(yc * yc, ones_bd) * inv_n
    yn = yc * lax.rsqrt(var + GN_EPS) * gng_ref[...] + gnb_ref[...]
    bonus = _head_sum(r * k2 * rk_ref[...], ones_bd) * v
    o_ref[0] = (yn + bonus) * g

    @pl.when(n == pl.num_programs(1) - 1)
    def _():
        sout_ref[0] = state_scr[...]


def rwkv_mix(p, shift0, s0, prm):
    b, t, _ = p.shape
    c = min(RWKV_CHUNK, t)
    assert t % c == 0
    w_ = RWKV_W
    row = lambda x: x.reshape(1, -1)
    ones_bd = jnp.kron(jnp.eye(RWKV_HEADS, dtype=F32), jnp.ones((RWKV_HEAD, RWKV_HEAD), F32)).astype(BF16)
    consts = [row(prm['mu']), row(prm['w0']), prm['w2'].astype(BF16), row(prm['a0']), prm['a2'].astype(BF16),
              prm['g2'].astype(BF16), row(prm['k_k']), row(prm['k_a']), row(prm['r_k']), row(prm['gn_g']),
              row(prm['gn_b']), ones_bd]
    return pl.pallas_call(
        functools.partial(_rwkv_kernel, chunk=c),
        grid=(b, t // c),
        in_specs=[
            pl.BlockSpec((1, c, RWKV_PROJ), lambda i, j: (i, j, 0)),
            pl.BlockSpec((1, 1, RWKV_PROJ), lambda i, j: (i, 0, 0)),
            pl.BlockSpec((1, RWKV_HEADS, RWKV_HEAD, RWKV_HEAD), lambda i, j: (i, 0, 0, 0)),
        ] + [_full(x.shape) for x in consts],
        out_specs=[
            pl.BlockSpec((1, c, w_), lambda i, j: (i, j, 0)),
            pl.BlockSpec((1, RWKV_HEADS, RWKV_HEAD, RWKV_HEAD), lambda i, j: (i, 0, 0, 0)),
        ],
        out_shape=[jax.ShapeDtypeStruct((b, t, w_), F32),
                   jax.ShapeDtypeStruct((b, RWKV_HEADS, RWKV_HEAD, RWKV_HEAD), F32)],
        scratch_shapes=[pltpu.VMEM((RWKV_HEADS, RWKV_HEAD, RWKV_HEAD), F32),
                        pltpu.VMEM((1, RWKV_PROJ), F32),
                        pltpu.VMEM((c, w_), F32)],
        compiler_params=pltpu.CompilerParams(dimension_semantics=("parallel", "arbitrary")),
        name="rwkv_mix",
    )(p, shift0.reshape(b, 1, RWKV_PROJ), s0, *consts)


def _gla_kernel(pg_ref, xa_ref, s0_ref, wa2_ref, ba_ref, gn_ref, o_ref, sout_ref, state_scr, o_scr, *, chunk):
    c = chunk
    n = pl.program_id(1)

    @pl.when(n == 0)
    def _():
        state_scr[...] = s0_ref[0]

    pg = pg_ref[0]
    tt = pg.shape[0]
    q = pg[:, 0:GLA_K] * (GLA_DK ** -0.5)
    k = pg[:, GLA_K:2 * GLA_K]
    v = pg[:, 2 * GLA_K:2 * GLA_K + GLA_V]
    gz = pg[:, 2 * GLA_K + GLA_V:]
    gk = jax.nn.log_sigmoid(_dg(xa_ref[0], wa2_ref[...]) + ba_ref[...]) * (1.0 / GLA_TAU)

    shift = int(math.log2(c))
    ri = lax.broadcasted_iota(jnp.int32, (tt, tt), 0)
    ci = lax.broadcasted_iota(jnp.int32, (tt, tt), 1)
    same = lax.shift_right_logical(ri, shift) == lax.shift_right_logical(ci, shift)
    causal = jnp.logical_and(same, ri >= ci)
    bc = _dg_sel(causal.astype(BF16), gk)
    bl = _dg_sel(same.astype(BF16), gk)
    qt = q * jnp.exp(bc)
    kt = k * jnp.exp(-bc)
    ks = k * jnp.exp(bl - bc)
    ones_c = jnp.ones((c, GLA_DV), BF16)

    for h in range(GLA_HEADS):
        sl = slice(h * GLA_DK, (h + 1) * GLA_DK)
        vs = slice(h * GLA_DV, (h + 1) * GLA_DV)
        v_h = v[:, vs]
        att = jnp.where(causal, _dg(qt[:, sl], kt[:, sl], _NT), 0.0)
        o_intra = _dg(att, v_h)
        s = state_scr[h]
        for j in range(tt // c):
            rs = slice(j * c, (j + 1) * c)
            o_scr[rs, vs] = o_intra[rs] + _dg(qt[rs, sl], s)
            decay = jnp.exp(_dg_selr(gk[rs, sl], ones_c, _TN))
            s = decay * s + _dg(ks[rs, sl], v_h[rs], _TN)
        state_scr[h] = s

    for h in range(GLA_HEADS):
        vs = slice(h * GLA_DV, (h + 1) * GLA_DV)
        o_h = o_scr[:, vs]
        o_h = o_h * lax.rsqrt(jnp.mean(o_h * o_h, axis=-1, keepdims=True) + NORM_EPS) * gn_ref[...]
        o_ref[0, :, vs] = o_h * jax.nn.silu(gz[:, vs])

    @pl.when(n == pl.num_programs(1) - 1)
    def _():
        sout_ref[0] = state_scr[...]


def gla_mix(pg, pxa, s0, wa2_pad, ba, g_norm):
    b, t, _ = pg.shape
    tt = min(GLA_TILE, t)
    c = min(GLA_CHUNK, t)
    assert t % tt == 0 and tt % c == 0
    return pl.pallas_call(
        functools.partial(_gla_kernel, chunk=c),
        grid=(b, t // tt),
        in_specs=[
            pl.BlockSpec((1, tt, GLA_MAIN), lambda i, j: (i, j, 0)),
            pl.BlockSpec((1, tt, LANES), lambda i, j: (i, j, 0)),
            pl.BlockSpec((1, GLA_HEADS, GLA_DK, GLA_DV), lambda i, j: (i, 0, 0, 0)),
            _full(wa2_pad.shape), _full((1, GLA_K)), _full((1, GLA_DV)),
        ],
        out_specs=[
            pl.BlockSpec((1, tt, GLA_V), lambda i, j: (i, j, 0)),
            pl.BlockSpec((1, GLA_HEADS, GLA_DK, GLA_DV), lambda i, j: (i, 0, 0, 0)),
        ],
        out_shape=[jax.ShapeDtypeStruct((b, t, GLA_V), F32),
                   jax.ShapeDtypeStruct((b, GLA_HEADS, GLA_DK, GLA_DV), F32)],
        scratch_shapes=[pltpu.VMEM((GLA_HEADS, GLA_DK, GLA_DV), F32), pltpu.VMEM((tt, GLA_V), F32)],
        compiler_params=pltpu.CompilerParams(dimension_semantics=("parallel", "arbitrary")),
        name="gla_mix",
    )(pg, pxa, s0, wa2_pad, ba.reshape(1, GLA_K), g_norm.reshape(1, GLA_DV))


def _cmul(ar, ai, br, bi):
    return ar * br - ai * bi, ar * bi + ai * br


def _s5_prep_kernel(lre_ref, lim_ref, ldt_ref, cre_ref, cim_ref, btre_ref, btim_ref, bre_ref, bim_ref, d_ref,
                    m_ref, bsre_ref, bsim_ref, csre_ref, csim_ref, lamre_ref, lamim_ref, *, chunk):
    c = chunk
    kc = c * S5_GROUP
    p_ = S5_STATE
    bsre_ref[...] = jnp.zeros_like(bsre_ref)
    bsim_ref[...] = jnp.zeros_like(bsim_ref)
    csre_ref[...] = jnp.zeros_like(csre_ref)
    csim_ref[...] = jnp.zeros_like(csim_ref)
    r16 = lax.broadcasted_iota(jnp.int32, (S5_GROUP, S5_GROUP), 0)
    c16 = lax.broadcasted_iota(jnp.int32, (S5_GROUP, S5_GROUP), 1)
    for gi in range(2):
        lr = lre_ref[0, gi:gi + 1, :]
        li = lim_ref[0, gi:gi + 1, :]
        dt = jnp.exp(ldt_ref[0, gi:gi + 1, :])
        mag = jnp.exp(lr * dt)
        ang = li * dt
        abr, abi = mag * jnp.cos(ang), mag * jnp.sin(ang)
        den = lr * lr + li * li
        fr = ((abr - 1.0) * lr + abi * li) / den
        fi = (abi * lr - (abr - 1.0) * li) / den
        pows = [(jnp.ones_like(abr), jnp.zeros_like(abr))]
        for _ in range(c):
            pows.append(_cmul(pows[-1][0], pows[-1][1], abr, abi))
        cre, cim = cre_ref[0, gi], cim_ref[0, gi]
        btre, btim = btre_ref[0, gi], btim_ref[0, gi]
        lhs_re, lhs_im = [], []
        col = slice(gi * p_, (gi + 1) * p_)
        for t in range(c):
            er, ei = _cmul(pows[t][0], pows[t][1], fr, fi)
            qr, qi = _cmul(cre, cim, er, ei)
            lhs_re.append(qr)
            lhs_im.append(qi)
            er, ei = _cmul(pows[c - 1 - t][0], pows[c - 1 - t][1], fr, fi)
            zr, zi = _cmul(btre, btim, er, ei)
            rows = slice(gi * kc + t * S5_GROUP, gi * kc + (t + 1) * S5_GROUP)
            bsre_ref[0, rows, col] = zr.astype(BF16)
            bsim_ref[0, rows, col] = zi.astype(BF16)
            kr, ki = pows[t + 1]
            csre_ref[0, rows, col] = (cre * kr - cim * ki).astype(BF16)
            csim_ref[0, rows, col] = (-cre * ki - cim * kr).astype(BF16)
        lhs_re = jnp.concatenate(lhs_re, axis=0)
        lhs_im = jnp.concatenate(lhs_im, axis=0)
        m = _dg3(lhs_re, bre_ref[0, gi]) - _dg3(lhs_im, bim_ref[0, gi])
        m_ref[0, gi] = m
        m_ref[0, gi, 0:S5_GROUP, :] = m[0:S5_GROUP] + jnp.where(r16 == c16, d_ref[0, gi], 0.0)
        lamre_ref[0, :, col] = pows[c][0]
        lamim_ref[0, :, col] = pows[c][1]


def s5_prep(prm, chunk):
    g, p_ = S5_GROUPS, S5_STATE
    kc = chunk * S5_GROUP
    pair = lambda x: x.reshape((S5_PAIRS, 2) + x.shape[1:])
    args = [pair(prm['lam_re']), pair(prm['lam_im']), pair(prm['log_dt'].reshape(g, 1)),
            pair(prm['c_re']), pair(prm['c_im']),
            pair(jnp.swapaxes(prm['b_re'], 1, 2)), pair(jnp.swapaxes(prm['b_im'], 1, 2)),
            pair(prm['b_re']), pair(prm['b_im']), pair(prm['d'].reshape(g, S5_GROUP, 1))]
    blk = lambda x: pl.BlockSpec((1,) + x.shape[1:], lambda i: (i,) + (0,) * (x.ndim - 1))
    out_shape = [jax.ShapeDtypeStruct((S5_PAIRS, 2, kc, S5_GROUP), F32)] + \
        [jax.ShapeDtypeStruct((S5_PAIRS, 2 * kc, 2 * p_), BF16)] * 4 + \
        [jax.ShapeDtypeStruct((S5_PAIRS, 1, 2 * p_), F32)] * 2
    return pl.pallas_call(
        functools.partial(_s5_prep_kernel, chunk=chunk),
        grid=(S5_PAIRS,),
        in_specs=[blk(a) for a in args],
        out_specs=[blk(o) for o in out_shape],
        out_shape=out_shape,
        compiler_params=pltpu.CompilerParams(dimension_semantics=("parallel",)),
        name="s5_prep",
    )(*args)


def _toeplitz(m, chunk):
    g = m.shape[0] * m.shape[1]
    m4 = m.reshape(g, chunk, S5_GROUP, S5_GROUP)
    idx = jnp.arange(chunk)[None, :] - jnp.arange(chunk)[:, None]
    t5 = jnp.where((idx >= 0)[None, :, :, None, None], m4[:, jnp.clip(idx, 0)], 0.0)
    kc = chunk * S5_GROUP
    return t5.transpose(0, 1, 4, 2, 3).reshape(g // 2, 2, kc, kc).astype(BF16)


def _s5_main_kernel(u_ref, t_ref, bsre_ref, bsim_ref, csre_ref, csim_ref, lamre_ref, lamim_ref, x0re_ref, x0im_ref,
                    y_ref, xre_ref, xim_ref, zre_scr, zim_scr, sre_scr, sim_scr, *, nb):
    rows, kc2 = u_ref.shape
    kc = kc2 // 2
    u = u_ref[...]
    zre_scr[...] = jnp.dot(u, bsre_ref[0], preferred_element_type=F32)
    zim_scr[...] = jnp.dot(u, bsim_ref[0], preferred_element_type=F32)
    lr, li = lamre_ref[0], lamim_ref[0]

    def step(i, carry):
        xr, xi = carry
        rs = pl.ds(pl.multiple_of(i * nb, nb), nb)
        sre_scr[rs, :] = xr
        sim_scr[rs, :] = xi
        return (lr * xr - li * xi + zre_scr[rs, :], li * xr + lr * xi + zim_scr[rs, :])

    xr, xi = lax.fori_loop(0, rows // nb, step, (x0re_ref[...], x0im_ref[...]))
    xre_ref[...] = xr
    xim_ref[...] = xi
    ys = _dg(sre_scr[...], csre_ref[0], _NT) + _dg(sim_scr[...], csim_ref[0], _NT)
    for gi in range(2):
        cols = slice(gi * kc, (gi + 1) * kc)
        y = ys[:, cols] + jnp.dot(u[:, cols], t_ref[0, gi], preferred_element_type=F32)
        y_ref[:, cols] = jax.nn.gelu(y).astype(y_ref.dtype)


def s5_main(uc, tmat, bsre, bsim, csre, csim, lamre, lamim, x0re, x0im):
    rows, width = uc.shape
    kc2 = width // S5_PAIRS
    nb = x0re.shape[0]
    p2 = 2 * S5_STATE
    blk3 = lambda x: pl.BlockSpec((1,) + x.shape[1:], lambda i: (i,) + (0,) * (x.ndim - 1))
    return pl.pallas_call(
        functools.partial(_s5_main_kernel, nb=nb),
        grid=(S5_PAIRS,),
        in_specs=[pl.BlockSpec((rows, kc2), lambda i: (0, i)), blk3(tmat), blk3(bsre), blk3(bsim), blk3(csre),
                  blk3(csim), blk3(lamre), blk3(lamim),
                  pl.BlockSpec((nb, p2), lambda i: (0, i)), pl.BlockSpec((nb, p2), lambda i: (0, i))],
        out_specs=[pl.BlockSpec((rows, kc2), lambda i: (0, i)),
                   pl.BlockSpec((nb, p2), lambda i: (0, i)), pl.BlockSpec((nb, p2), lambda i: (0, i))],
        out_shape=[jax.ShapeDtypeStruct((rows, width), BF16),
                   jax.ShapeDtypeStruct((nb, S5_GROUPS * S5_STATE), F32),
                   jax.ShapeDtypeStruct((nb, S5_GROUPS * S5_STATE), F32)],
        scratch_shapes=[pltpu.VMEM((rows, p2), F32)] * 4,
        compiler_params=pltpu.CompilerParams(dimension_semantics=("parallel",), vmem_limit_bytes=VMEM_LIMIT),
        name="s5_main",
    )(uc, tmat, bsre, bsim, csre, csim, lamre, lamim, x0re, x0im)


def s5_mix(u, x0re, x0im, tables, chunk):
    b, t, e = u.shape
    n = t // chunk
    g, cg = S5_GROUPS, S5_GROUP
    uc = u.reshape(b, n, chunk, g, cg).transpose(1, 0, 3, 2, 4).reshape(n * b, g * chunk * cg)
    yc, xre, xim = s5_main(uc, *tables, x0re.reshape(b, -1), x0im.reshape(b, -1))
    y = yc.reshape(n, b, g, chunk, cg).transpose(1, 0, 3, 2, 4).reshape(b * t, e)
    return y, xre.reshape(b, g, S5_STATE), xim.reshape(b, g, S5_STATE)


def _trunk(x, st_shift, st_wkv, st_gla, st_re, st_im, w):
    b, t, d = x.shape
    m = b * t
    x2 = x.reshape(m, d)

    p_r, p_g, p_xa = norm_proj(x2, w['norm_mix_pre'][0], [w['w_in_rwkv'], w['w_in_gla'], w['w_in_xa']],
                               [F32, F32, F32])
    o_r, wkv = rwkv_mix(p_r.reshape(b, t, -1), st_shift[0], st_wkv[0], w['rwkv'])
    o_g, gla = gla_mix(p_g.reshape(b, t, -1), p_xa.reshape(b, t, -1), st_gla[0], w['gla_wa2'], w['gla_ba'],
                       w['gla_norm'])
    shift = p_r.reshape(b, t, -1)[:, -1]
    x2 = proj_post([o_r.reshape(m, -1), o_g.reshape(m, -1)], [w['w_out_rwkv'], w['w_out_gla']],
                   w['norm_mix_post'][0], x2)
    x2 = ffn(x2, w['norm_ffn_pre'][0], w['w_ff_up'][0], w['w_ff_down'][0], w['norm_ffn_post'][0])

    (u,) = norm_proj(x2, w['norm_mix_pre'][1], [w['s5_w_in']], [BF16])
    chunk = min(S5_CHUNK, t)
    y, s_re, s_im = s5_mix(u.reshape(b, t, d), st_re[0], st_im[0], w['s5_tables'][chunk], chunk)
    x2 = proj_post([y], [w['s5_w_out']], w['norm_mix_post'][1], x2, glu=True)
    x2 = ffn(x2, w['norm_ffn_pre'][1], w['w_ff_up'][1], w['w_ff_down'][1], w['norm_ffn_post'][1])
    return x2.reshape(b, t, d), shift[None], wkv[None], gla[None], s_re[None], s_im[None]


def kernel(x_prompt, x_sample, state_rwkv_shift, state_rwkv_wkv, state_gla, state_s5_re, state_s5_im,
           norm_mix_pre, norm_mix_post, norm_ffn_pre, norm_ffn_post,
           w_mix_in, w_mix_out, rwkv_mu, rwkv_w0, rwkv_w2, rwkv_a0, rwkv_a2, rwkv_g2,
           rwkv_k_k, rwkv_k_a, rwkv_r_k, rwkv_gn_g, rwkv_gn_b, gla_wa2, gla_ba, gla_norm,
           s5_w_in, s5_lam_re, s5_lam_im, s5_log_dt, s5_b_re, s5_b_im, s5_c_re, s5_c_im, s5_d, s5_w_out,
           w_ff_up, w_ff_down):
    assert norm_mix_pre.shape[0] == 2, "two layers: one RWKV-7/GLA layer, one S5 layer"
    bf = lambda a: a.astype(BF16)
    w_in = w_mix_in[0]
    gla0 = RWKV_PROJ
    xa0 = gla0 + 2 * GLA_K + GLA_V
    w_in_gla = jnp.concatenate([w_in[:, gla0:xa0], w_in[:, xa0 + GLA_LORA:]], axis=1)
    w_in_xa = jnp.pad(w_in[:, xa0:xa0 + GLA_LORA], ((0, 0), (0, LANES - GLA_LORA)))
    s5_prm = dict(lam_re=s5_lam_re[0], lam_im=s5_lam_im[0], log_dt=s5_log_dt[0], b_re=s5_b_re[0],
                  b_im=s5_b_im[0], c_re=s5_c_re[0], c_im=s5_c_im[0], d=s5_d[0])
    tables = {}
    for chunk in sorted({min(S5_CHUNK, x_prompt.shape[1]), min(S5_CHUNK, x_sample.shape[1])}):
        m_blk, bsre, bsim, csre, csim, lamre, lamim = s5_prep(s5_prm, chunk)
        tables[chunk] = (_toeplitz(m_blk, chunk), bsre, bsim, csre, csim, lamre, lamim)
    w = dict(
        norm_mix_pre=norm_mix_pre, norm_mix_post=norm_mix_post, norm_ffn_pre=norm_ffn_pre,
        norm_ffn_post=norm_ffn_post,
        w_in_rwkv=bf(w_in[:, :RWKV_PROJ]), w_in_gla=bf(w_in_gla), w_in_xa=bf(w_in_xa),
        w_out_rwkv=bf(w_mix_out[0, :RWKV_W]), w_out_gla=bf(w_mix_out[0, RWKV_W:]),
        rwkv=dict(mu=rwkv_mu[0], w0=rwkv_w0[0], w2=rwkv_w2[0], a0=rwkv_a0[0], a2=rwkv_a2[0], g2=rwkv_g2[0],
                  k_k=rwkv_k_k[0], k_a=rwkv_k_a[0], r_k=rwkv_r_k[0], gn_g=rwkv_gn_g[0], gn_b=rwkv_gn_b[0]),
        gla_wa2=bf(jnp.pad(gla_wa2[0], ((0, LANES - GLA_LORA), (0, 0)))), gla_ba=gla_ba[0], gla_norm=gla_norm[0],
        s5_w_in=bf(s5_w_in[0]), s5_w_out=bf(s5_w_out[0]), s5_tables=tables,
        w_ff_up=bf(w_ff_up), w_ff_down=bf(w_ff_down),
    )
    bp = x_prompt.shape[0]
    zeros = lambda s: jnp.zeros((s.shape[0], bp) + s.shape[2:], x_prompt.dtype)
    y_p, sh_p, wkv_p, gla_p, re_p, im_p = _trunk(x_prompt, zeros(state_rwkv_shift), zeros(state_rwkv_wkv),
                                                 zeros(state_gla), zeros(state_s5_re), zeros(state_s5_im), w)
    y_s, sh_s, wkv_s, gla_s, re_s, im_s = _trunk(x_sample, state_rwkv_shift, state_rwkv_wkv, state_gla,
                                                 state_s5_re, state_s5_im, w)
    return (y_p, y_s, sh_p, sh_s, wkv_p, wkv_s, gla_p, gla_s, re_p, re_s, im_p, im_s)
```

```python
import functools
import math

import jax
import jax.numpy as jnp
from jax import lax
from jax.experimental import pallas as pl
from jax.experimental.pallas import tpu as pltpu

F32 = jnp.float32
BF16 = jnp.bfloat16

D_MODEL = 1024
D_FF = 4096
NORM_EPS = 1e-6
RWKV_HEADS = 8
RWKV_HEAD = 64
RWKV_W = RWKV_HEADS * RWKV_HEAD
RWKV_PROJ = 1792
RWKV_CHUNK = 64
RWKV_ROWS = 64
GN_EPS = 64e-5
_P_GRAM = 1
_P_INV = 1
_P_SOLVE = 1
_P_MIX = 1
_P_STATE = 1
GLA_HEADS = 4
GLA_DK = 64
GLA_DV = 128
GLA_K = GLA_HEADS * GLA_DK
GLA_V = GLA_HEADS * GLA_DV
GLA_LORA = 16
GLA_TAU = 16.0
GLA_CHUNK = 32
GLA_MAIN = 2 * GLA_K + 2 * GLA_V
GLA_TILE = 256
S5_GROUPS = 64
S5_GROUP = 16
S5_STATE = 64
S5_CHUNK = 16
S5_PAIRS = S5_GROUPS // 2
LANES = 128
VMEM_LIMIT = 48 * 1024 * 1024

_NN = (((1,), (0,)), ((), ()))
_NT = (((1,), (1,)), ((), ()))
_TN = (((0,), (0,)), ((), ()))


def _dg(a, b, dn=_NN):
    return lax.dot_general(a.astype(BF16), b.astype(BF16), dn, preferred_element_type=F32)


def _split2(x):
    hi = x.astype(BF16)
    lo = (x - hi.astype(F32)).astype(BF16)
    return hi, lo


def _split3(x):
    hi = x.astype(BF16)
    r1 = x - hi.astype(F32)
    mid = r1.astype(BF16)
    lo = (r1 - mid.astype(F32)).astype(BF16)
    return hi, mid, lo


def _dg3(a, b, dn=_NN):
    ah, al = _split2(a)
    bh, bl = _split2(b)
    d = lambda x, y: lax.dot_general(x, y, dn, preferred_element_type=F32)
    return d(ah, bh) + (d(ah, bl) + d(al, bh))


def _mp(a, b, dn, passes):
    return _dg(a, b, dn) if passes == 1 else _dg3(a, b, dn)


def _dg_sel(sel, x, dn=_NN):
    d = lambda y: lax.dot_general(sel, y, dn, preferred_element_type=F32)
    h, m, l = _split3(x)
    return d(h) + (d(m) + d(l))


def _dg_selr(x, sel, dn=_NN):
    d = lambda y: lax.dot_general(y, sel, dn, preferred_element_type=F32)
    h, m, l = _split3(x)
    return d(h) + (d(m) + d(l))


def _rms(x, g):
    return x * lax.rsqrt(jnp.mean(x * x, axis=-1, keepdims=True) + NORM_EPS) * g


def _row_tile(m, want):
    t = min(m, want)
    assert m % t == 0, (m, t)
    return t


def _full(shape):
    nd = len(shape)
    return pl.BlockSpec(shape, lambda *_: (0,) * nd)


def _norm_proj_kernel(x_ref, g_ref, *refs):
    n = len(refs) // 2
    h = _rms(x_ref[...], g_ref[...]).astype(BF16)
    for w_ref, o_ref in zip(refs[:n], refs[n:]):
        o_ref[...] = jnp.dot(h, w_ref[...], preferred_element_type=F32).astype(o_ref.dtype)


def norm_proj(x, g, ws, out_dtypes, tm=512):
    m, d = x.shape
    tm = _row_tile(m, tm)
    return pl.pallas_call(
        _norm_proj_kernel,
        grid=(m // tm,),
        in_specs=[pl.BlockSpec((tm, d), lambda i: (i, 0)), _full((1, d))]
        + [_full(w.shape) for w in ws],
        out_specs=[pl.BlockSpec((tm, w.shape[1]), lambda i: (i, 0)) for w in ws],
        out_shape=[jax.ShapeDtypeStruct((m, w.shape[1]), dt) for w, dt in zip(ws, out_dtypes)],
        compiler_params=pltpu.CompilerParams(dimension_semantics=("parallel",), vmem_limit_bytes=VMEM_LIMIT),
        name="norm_proj",
    )(x, g.reshape(1, d), *ws)


def _proj_post_kernel(*refs, n_in, glu):
    a_refs = refs[:n_in]
    w_refs = refs[n_in:2 * n_in]
    g_ref, x_ref, o_ref = refs[2 * n_in:]
    m = None
    for a_ref, w_ref in zip(a_refs, w_refs):
        t = jnp.dot(a_ref[...].astype(BF16), w_ref[...], preferred_element_type=F32)
        m = t if m is None else m + t
    if glu:
        m = m[:, :D_MODEL] * jax.nn.sigmoid(m[:, D_MODEL:])
    o_ref[...] = x_ref[...] + _rms(m, g_ref[...])


def proj_post(a_list, w_list, g, x, glu=False, tm=512):
    m, d = x.shape
    tm = _row_tile(m, tm)
    n_in = len(a_list)
    return pl.pallas_call(
        functools.partial(_proj_post_kernel, n_in=n_in, glu=glu),
        grid=(m // tm,),
        in_specs=[pl.BlockSpec((tm, a.shape[1]), lambda i: (i, 0)) for a in a_list]
        + [_full(w.shape) for w in w_list]
        + [_full((1, d)), pl.BlockSpec((tm, d), lambda i: (i, 0))],
        out_specs=pl.BlockSpec((tm, d), lambda i: (i, 0)),
        out_shape=jax.ShapeDtypeStruct((m, d), F32),
        compiler_params=pltpu.CompilerParams(dimension_semantics=("parallel",), vmem_limit_bytes=VMEM_LIMIT),
        name="proj_post",
    )(*a_list, *w_list, g.reshape(1, d), x)


def _ffn_kernel(x_ref, gpre_ref, wup_ref, wdn_ref, gpost_ref, o_ref, h_scr, acc_scr):
    j = pl.program_id(1)

    @pl.when(j == 0)
    def _():
        h_scr[...] = _rms(x_ref[...], gpre_ref[...]).astype(BF16)
        acc_scr[...] = jnp.zeros_like(acc_scr)

    u = jnp.dot(h_scr[...], wup_ref[...], preferred_element_type=F32)
    u = jnp.square(jnp.maximum(u, 0.0)).astype(BF16)
    acc_scr[...] += jnp.dot(u, wdn_ref[...], preferred_element_type=F32)

    @pl.when(j == pl.num_programs(1) - 1)
    def _():
        o_ref[...] = x_ref[...] + _rms(acc_scr[...], gpost_ref[...])


def ffn(x, g_pre, w_up, w_down, g_post, tm=1024, tf=512):
    m, d = x.shape
    f = w_up.shape[1]
    tm = _row_tile(m, tm)
    return pl.pallas_call(
        _ffn_kernel,
        grid=(m // tm, f // tf),
        in_specs=[
            pl.BlockSpec((tm, d), lambda i, j: (i, 0)),
            _full((1, d)),
            pl.BlockSpec((d, tf), lambda i, j: (0, j)),
            pl.BlockSpec((tf, d), lambda i, j: (j, 0)),
            _full((1, d)),
        ],
        out_specs=pl.BlockSpec((tm, d), lambda i, j: (i, 0)),
        out_shape=jax.ShapeDtypeStruct((m, d), F32),
        scratch_shapes=[pltpu.VMEM((tm, d), BF16), pltpu.VMEM((tm, d), F32)],
        compiler_params=pltpu.CompilerParams(dimension_semantics=("parallel", "arbitrary"),
                                             vmem_limit_bytes=VMEM_LIMIT),
        name="ffn",
    )(x, g_pre.reshape(1, d), w_up, w_down, g_post.reshape(1, d))


def _head_sum(x, ones_bd):
    hi, lo = _split2(x)
    d = lambda y: jnp.dot(y, ones_bd, preferred_element_type=F32)
    return d(hi) + d(lo)


def _rwkv_kernel(p_ref, sh0_ref, s0_ref, mu_ref, w0_ref, w2_ref, a0_ref, a2_ref, g2_ref, kk_ref, ka_ref,
                 rk_ref, gng_ref, gnb_ref, ones_ref, o_ref, sout_ref, state_scr, carry_scr, y_scr, *, chunk):
    c = chunk
    nb = p_ref.shape[0]
    rows = nb * c
    shift = int(math.log2(c))
    n = pl.program_id(1)

    @pl.when(n == 0)
    def _():
        carry_scr[...] = sh0_ref[...]
        state_scr[...] = s0_ref[...]

    p3 = p_ref[...]
    p = p3.reshape(rows, RWKV_PROJ)
    t_idx = jnp.bitwise_and(lax.broadcasted_iota(jnp.int32, (rows, 1), 0), c - 1)
    carry = jnp.broadcast_to(carry_scr[...], (nb, c, RWKV_PROJ)).reshape(rows, RWKV_PROJ)
    prev = jnp.where(t_idx == 0, carry, pltpu.roll(p, 1, 0))
    carry_scr[...] = p3[:, c - 1:c, :]
    z = p + (prev - p) * mu_ref[...]

    w_ = RWKV_W
    r = z[:, 0:w_]
    k = z[:, w_:2 * w_]
    v = z[:, 2 * w_:3 * w_]
    xw = z[:, 3 * w_:3 * w_ + 64]
    xa = z[:, 3 * w_ + 64:3 * w_ + 128]
    xg = z[:, 3 * w_ + 128:3 * w_ + 256]

    wlog = -jax.nn.softplus(-(w0_ref[...] + _dg(jnp.tanh(xw), w2_ref[...]))) - 0.5
    logd = -jnp.exp(wlog)
    a = jax.nn.sigmoid(a0_ref[...] + _dg(xa, a2_ref[...]))
    g = _dg(jax.nn.sigmoid(xg), g2_ref[...])

    ones_bd = ones_ref[...]
    kk = k * kk_ref[...]
    kk = kk * lax.rsqrt(jnp.maximum(_head_sum(kk * kk, ones_bd), 1e-24))
    k2 = k * (1.0 + (a - 1.0) * ka_ref[...])
    bb = kk * a

    ri = lax.broadcasted_iota(jnp.int32, (rows, rows), 0)
    ci = lax.broadcasted_iota(jnp.int32, (rows, rows), 1)
    same = lax.shift_right_logical(ri, shift) == lax.shift_right_logical(ci, shift)
    lower = jnp.logical_and(same, ri > ci)
    lower_eq = jnp.logical_and(same, ri >= ci)
    eye = (ri == ci).astype(F32)
    cum = _dg_sel(lower_eq.astype(BF16), logd)
    tot = _dg_sel(same.astype(BF16), logd)
    g_in = jnp.exp(cum)
    g_ex = jnp.exp(cum - logd)
    g_inv = jnp.exp(-cum)
    g_hat = jnp.exp(tot - cum)
    g_all = jnp.exp(tot)
    rt = r * g_in
    kt = kk * g_ex
    ktil = k2 * g_inv
    btil = bb * g_inv
    khat = k2 * g_hat
    bhat = bb * g_hat

    heads = range(RWKV_HEADS)
    seqs = range(nb)
    hs = [slice(h * RWKV_HEAD, (h + 1) * RWKV_HEAD) for h in heads]
    kr = [jnp.concatenate([kt[:, s], rt[:, s]], axis=0) for s in hs]
    gb = [_mp(kr[h], btil[:, hs[h]], _NT, _P_GRAM) for h in heads]
    gk = [_mp(kr[h], ktil[:, hs[h]], _NT, _P_GRAM) for h in heads]
    a_b = [jnp.where(lower, x[:rows], 0.0) for x in gb]
    a_rb = [jnp.where(lower_eq, x[rows:], 0.0) for x in gb]
    a_kk = [jnp.concatenate([jnp.where(lower, x[:rows], 0.0), jnp.where(lower_eq, x[rows:], 0.0)], axis=0)
            for x in gk]
    ks = [[_mp(jnp.concatenate([kt[i * c:(i + 1) * c, hs[h]], rt[i * c:(i + 1) * c, hs[h]]], axis=0),
               state_scr[i, h], _NT, _P_STATE) for i in seqs] for h in heads]
    x0 = [jnp.concatenate([ks[h][i][:c] for i in seqs], axis=0) if nb > 1 else ks[h][0][:c] for h in heads]
    r0 = [jnp.concatenate([ks[h][i][c:] for i in seqs], axis=0) if nb > 1 else ks[h][0][c:] for h in heads]
    av = [_mp(a_kk[h], v[:, hs[h]], _NN, _P_MIX) for h in heads]
    tinv = [eye - x for x in a_b]
    pw = [_mp(x, x, _NN, _P_INV) for x in a_b]
    span = 2
    while span < c:
        span *= 2
        if span < c:
            pr = [_mp(jnp.concatenate([tinv[h], pw[h]], axis=0), pw[h], _NN, _P_INV) for h in heads]
            tinv = [tinv[h] + pr[h][:rows] for h in heads]
            pw = [pr[h][rows:] for h in heads]
        else:
            tinv = [tinv[h] + _mp(tinv[h], pw[h], _NN, _P_INV) for h in heads]
    u =[_mp(tinv[h], x0[h] + av[h][:rows], _NN, _P_SOLVE) for h in heads]
    for h in heads:
        y_scr[:, hs[h]] = r0[h] + av[h][rows:] - _mp(a_rb[h], u[h], _NN, _P_MIX)
    for h in heads:
        for i in seqs:
            rs = slice(i * c, (i + 1) * c)
            vu = jnp.concatenate([v[rs, hs[h]], -u[h][rs]], axis=0)
            kb = jnp.concatenate([khat[rs, hs[h]], bhat[rs, hs[h]]], axis=0)
            state_scr[i, h] = state_scr[i, h] * g_all[i * c:i * c + 1, hs[h]] + _mp(vu, kb, _TN, _P_STATE)

    y = y_scr[...]
    inv_n = 1.0 / RWKV_HEAD
    mean = _head_sum(y, ones_bd) * inv_n
    yc = y - mean
    var = _head_sum(yc * yc, ones_bd) * inv_n
    yn = yc * lax.rsqrt(var + GN_EPS) * gng_ref[...] + gnb_ref[...]
    bonus = _head_sum(r * k2 * rk_ref[...], ones_bd) * v
    o_ref[...] = ((yn + bonus) * g).reshape(nb, c, RWKV_W)

    @pl.when(n == pl.num_programs(1) - 1)
    def _():
        sout_ref[...] = state_scr[...]


def rwkv_mix(p, shift0, s0, prm):
    b, t, _ = p.shape
    c = min(RWKV_CHUNK, t)
    assert t % c == 0
    nb = min(b, RWKV_ROWS // c)
    assert b % nb == 0
    w_ = RWKV_W
    row = lambda x: x.reshape(1, -1)
    ones_bd = jnp.kron(jnp.eye(RWKV_HEADS, dtype=F32), jnp.ones((RWKV_HEAD, RWKV_HEAD), F32)).astype(BF16)
    consts = [row(prm['mu']), row(prm['w0']), prm['w2'].astype(BF16), row(prm['a0']), prm['a2'].astype(BF16),
              prm['g2'].astype(BF16), row(prm['k_k']), row(prm['k_a']), row(prm['r_k']), row(prm['gn_g']),
              row(prm['gn_b']), ones_bd]
    return pl.pallas_call(
        functools.partial(_rwkv_kernel, chunk=c),
        grid=(b // nb, t // c),
        in_specs=[
            pl.BlockSpec((nb, c, RWKV_PROJ), lambda i, j: (i, j, 0)),
            pl.BlockSpec((nb, 1, RWKV_PROJ), lambda i, j: (i, 0, 0)),
            pl.BlockSpec((nb, RWKV_HEADS, RWKV_HEAD, RWKV_HEAD), lambda i, j: (i, 0, 0, 0)),
        ] + [_full(x.shape) for x in consts],
        out_specs=[
            pl.BlockSpec((nb, c, w_), lambda i, j: (i, j, 0)),
            pl.BlockSpec((nb, RWKV_HEADS, RWKV_HEAD, RWKV_HEAD), lambda i, j: (i, 0, 0, 0)),
        ],
        out_shape=[jax.ShapeDtypeStruct((b, t, w_), F32),
                   jax.ShapeDtypeStruct((b, RWKV_HEADS, RWKV_HEAD, RWKV_HEAD), F32)],
        scratch_shapes=[pltpu.VMEM((nb, RWKV_HEADS, RWKV_HEAD, RWKV_HEAD), F32),
                        pltpu.VMEM((nb, 1, RWKV_PROJ), F32),
                        pltpu.VMEM((nb * c, w_), F32)],
        compiler_params=pltpu.CompilerParams(dimension_semantics=("parallel", "arbitrary")),
        name="rwkv_mix",
    )(p, shift0.reshape(b, 1, RWKV_PROJ), s0, *consts)


def _gla_kernel(pg_ref, xa_ref, s0_ref, wa2_ref, ba_ref, gn_ref, o_ref, sout_ref, state_scr, o_scr, *, chunk):
    c = chunk
    n = pl.program_id(1)

    @pl.when(n == 0)
    def _():
        state_scr[...] = s0_ref[0]

    pg = pg_ref[0]
    tt = pg.shape[0]
    q = pg[:, 0:GLA_K] * (GLA_DK ** -0.5)
    k = pg[:, GLA_K:2 * GLA_K]
    v = pg[:, 2 * GLA_K:2 * GLA_K + GLA_V]
    gz = pg[:, 2 * GLA_K + GLA_V:]
    gk = jax.nn.log_sigmoid(_dg(xa_ref[0], wa2_ref[...]) + ba_ref[...]) * (1.0 / GLA_TAU)

    shift = int(math.log2(c))
    ri = lax.broadcasted_iota(jnp.int32, (tt, tt), 0)
    ci = lax.broadcasted_iota(jnp.int32, (tt, tt), 1)
    same = lax.shift_right_logical(ri, shift) == lax.shift_right_logical(ci, shift)
    causal = jnp.logical_and(same, ri >= ci)
    bc = _dg_sel(causal.astype(BF16), gk)
    bl = _dg_sel(same.astype(BF16), gk)
    qt = q * jnp.exp(bc)
    kt = k * jnp.exp(-bc)
    ks = k * jnp.exp(bl - bc)
    ones_c = jnp.ones((c, GLA_DV), BF16)

    for h in range(GLA_HEADS):
        sl = slice(h * GLA_DK, (h + 1) * GLA_DK)
        vs = slice(h * GLA_DV, (h + 1) * GLA_DV)
        v_h = v[:, vs]
        att = jnp.where(causal, _dg(qt[:, sl], kt[:, sl], _NT), 0.0)
        o_intra = _dg(att, v_h)
        s = state_scr[h]
        for j in range(tt // c):
            rs = slice(j * c, (j + 1) * c)
            o_scr[rs, vs] = o_intra[rs] + _dg(qt[rs, sl], s)
            decay = jnp.exp(_dg_selr(gk[rs, sl], ones_c, _TN))
            s = decay * s + _dg(ks[rs, sl], v_h[rs], _TN)
        state_scr[h] = s

    for h in range(GLA_HEADS):
        vs = slice(h * GLA_DV, (h + 1) * GLA_DV)
        o_h = o_scr[:, vs]
        o_h = o_h * lax.rsqrt(jnp.mean(o_h * o_h, axis=-1, keepdims=True) + NORM_EPS) * gn_ref[...]
        o_ref[0, :, vs] = o_h * jax.nn.silu(gz[:, vs])

    @pl.when(n == pl.num_programs(1) - 1)
    def _():
        sout_ref[0] = state_scr[...]


def gla_mix(pg, pxa, s0, wa2_pad, ba, g_norm):
    b, t, _ = pg.shape
    tt = min(GLA_TILE, t)
    c = min(GLA_CHUNK, t)
    assert t % tt == 0 and tt % c == 0
    return pl.pallas_call(
        functools.partial(_gla_kernel, chunk=c),
        grid=(b, t // tt),
        in_specs=[
            pl.BlockSpec((1, tt, GLA_MAIN), lambda i, j: (i, j, 0)),
            pl.BlockSpec((1, tt, LANES), lambda i, j: (i, j, 0)),
            pl.BlockSpec((1, GLA_HEADS, GLA_DK, GLA_DV), lambda i, j: (i, 0, 0, 0)),
            _full(wa2_pad.shape), _full((1, GLA_K)), _full((1, GLA_DV)),
        ],
        out_specs=[
            pl.BlockSpec((1, tt, GLA_V), lambda i, j: (i, j, 0)),
            pl.BlockSpec((1, GLA_HEADS, GLA_DK, GLA_DV), lambda i, j: (i, 0, 0, 0)),
        ],
        out_shape=[jax.ShapeDtypeStruct((b, t, GLA_V), F32),
                   jax.ShapeDtypeStruct((b, GLA_HEADS, GLA_DK, GLA_DV), F32)],
        scratch_shapes=[pltpu.VMEM((GLA_HEADS, GLA_DK, GLA_DV), F32), pltpu.VMEM((tt, GLA_V), F32)],
        compiler_params=pltpu.CompilerParams(dimension_semantics=("parallel", "arbitrary")),
        name="gla_mix",
    )(pg, pxa, s0, wa2_pad, ba.reshape(1, GLA_K), g_norm.reshape(1, GLA_DV))


def _cmul(ar, ai, br, bi):
    return ar * br - ai * bi, ar * bi + ai * br


def _s5_prep_kernel(lre_ref, lim_ref, ldt_ref, cre_ref, cim_ref, btre_ref, btim_ref, bre_ref, bim_ref, d_ref,
                    m_ref, bsre_ref, bsim_ref, csre_ref, csim_ref, lamre_ref, lamim_ref, *, chunk):
    c = chunk
    kc = c * S5_GROUP
    p_ = S5_STATE
    bsre_ref[...] = jnp.zeros_like(bsre_ref)
    bsim_ref[...] = jnp.zeros_like(bsim_ref)
    csre_ref[...] = jnp.zeros_like(csre_ref)
    csim_ref[...] = jnp.zeros_like(csim_ref)
    r16 = lax.broadcasted_iota(jnp.int32, (S5_GROUP, S5_GROUP), 0)
    c16 = lax.broadcasted_iota(jnp.int32, (S5_GROUP, S5_GROUP), 1)
    for gi in range(2):
        lr = lre_ref[0, gi:gi + 1, :]
        li = lim_ref[0, gi:gi + 1, :]
        dt = jnp.exp(ldt_ref[0, gi:gi + 1, :])
        mag = jnp.exp(lr * dt)
        ang = li * dt
        abr, abi = mag * jnp.cos(ang), mag * jnp.sin(ang)
        den = lr * lr + li * li
        fr = ((abr - 1.0) * lr + abi * li) / den
        fi = (abi * lr - (abr - 1.0) * li) / den
        pows = [(jnp.ones_like(abr), jnp.zeros_like(abr))]
        for _ in range(c):
            pows.append(_cmul(pows[-1][0], pows[-1][1], abr, abi))
        cre, cim = cre_ref[0, gi], cim_ref[0, gi]
        btre, btim = btre_ref[0, gi], btim_ref[0, gi]
        lhs_re, lhs_im = [], []
        col = slice(gi * p_, (gi + 1) * p_)
        for t in range(c):
            er, ei = _cmul(pows[t][0], pows[t][1], fr, fi)
            qr, qi = _cmul(cre, cim, er, ei)
            lhs_re.append(qr)
            lhs_im.append(qi)
            er, ei = _cmul(pows[c - 1 - t][0], pows[c - 1 - t][1], fr, fi)
            zr, zi = _cmul(btre, btim, er, ei)
            rows = slice(gi * kc + t * S5_GROUP, gi * kc + (t + 1) * S5_GROUP)
            bsre_ref[0, rows, col] = zr.astype(BF16)
            bsim_ref[0, rows, col] = zi.astype(BF16)
            kr, ki = pows[t + 1]
            csre_ref[0, rows, col] = (cre * kr - cim * ki).astype(BF16)
            csim_ref[0, rows, col] = (-cre * ki - cim * kr).astype(BF16)
        lhs_re = jnp.concatenate(lhs_re, axis=0)
        lhs_im = jnp.concatenate(lhs_im, axis=0)
        m = _dg3(lhs_re, bre_ref[0, gi]) - _dg3(lhs_im, bim_ref[0, gi])
        m_ref[0, gi] = m
        m_ref[0, gi, 0:S5_GROUP, :] = m[0:S5_GROUP] + jnp.where(r16 == c16, d_ref[0, gi], 0.0)
        lamre_ref[0, :, col] = pows[c][0]
        lamim_ref[0, :, col] = pows[c][1]


def s5_prep(prm, chunk):
    g, p_ = S5_GROUPS, S5_STATE
    kc = chunk * S5_GROUP
    pair = lambda x: x.reshape((S5_PAIRS, 2) + x.shape[1:])
    args = [pair(prm['lam_re']), pair(prm['lam_im']), pair(prm['log_dt'].reshape(g, 1)),
            pair(prm['c_re']), pair(prm['c_im']),
            pair(jnp.swapaxes(prm['b_re'], 1, 2)), pair(jnp.swapaxes(prm['b_im'], 1, 2)),
            pair(prm['b_re']), pair(prm['b_im']), pair(prm['d'].reshape(g, S5_GROUP, 1))]
    blk = lambda x: pl.BlockSpec((1,) + x.shape[1:], lambda i: (i,) + (0,) * (x.ndim - 1))
    out_shape = [jax.ShapeDtypeStruct((S5_PAIRS, 2, kc, S5_GROUP), F32)] + \
        [jax.ShapeDtypeStruct((S5_PAIRS, 2 * kc, 2 * p_), BF16)] * 4 + \
        [jax.ShapeDtypeStruct((S5_PAIRS, 1, 2 * p_), F32)] * 2
    return pl.pallas_call(
        functools.partial(_s5_prep_kernel, chunk=chunk),
        grid=(S5_PAIRS,),
        in_specs=[blk(a) for a in args],
        out_specs=[blk(o) for o in out_shape],
        out_shape=out_shape,
        compiler_params=pltpu.CompilerParams(dimension_semantics=("parallel",)),
        name="s5_prep",
    )(*args)


def _toeplitz(m, chunk):
    g = m.shape[0] * m.shape[1]
    m4 = m.reshape(g, chunk, S5_GROUP, S5_GROUP)
    idx = jnp.arange(chunk)[None, :] - jnp.arange(chunk)[:, None]
    t5 = jnp.where((idx >= 0)[None, :, :, None, None], m4[:, jnp.clip(idx, 0)], 0.0)
    kc = chunk * S5_GROUP
    return t5.transpose(0, 1, 4, 2, 3).reshape(g // 2, 2, kc, kc).astype(BF16)


def _s5_main_kernel(u_ref, t_ref, bsre_ref, bsim_ref, csre_ref, csim_ref, lamre_ref, lamim_ref, x0re_ref, x0im_ref,
                    y_ref, xre_ref, xim_ref, zre_scr, zim_scr, sre_scr, sim_scr, *, nb):
    rows, kc2 = u_ref.shape
    kc = kc2 // 2
    u = u_ref[...]
    zre_scr[...] = jnp.dot(u, bsre_ref[0], preferred_element_type=F32)
    zim_scr[...] = jnp.dot(u, bsim_ref[0], preferred_element_type=F32)
    lr, li = lamre_ref[0], lamim_ref[0]

    def step(i, carry):
        xr, xi = carry
        rs = pl.ds(pl.multiple_of(i * nb, nb), nb)
        sre_scr[rs, :] = xr
        sim_scr[rs, :] = xi
        return (lr * xr - li * xi + zre_scr[rs, :], li * xr + lr * xi + zim_scr[rs, :])

    xr, xi = lax.fori_loop(0, rows // nb, step, (x0re_ref[...], x0im_ref[...]))
    xre_ref[...] = xr
    xim_ref[...] = xi
    ys = _dg(sre_scr[...], csre_ref[0], _NT) + _dg(sim_scr[...], csim_ref[0], _NT)
    for gi in range(2):
        cols = slice(gi * kc, (gi + 1) * kc)
        y = ys[:, cols] + jnp.dot(u[:, cols], t_ref[0, gi], preferred_element_type=F32)
        y_ref[:, cols] = jax.nn.gelu(y).astype(y_ref.dtype)


def s5_main(uc, tmat, bsre, bsim, csre, csim, lamre, lamim, x0re, x0im):
    rows, width = uc.shape
    kc2 = width // S5_PAIRS
    nb = x0re.shape[0]
    p2 = 2 * S5_STATE
    blk3 = lambda x: pl.BlockSpec((1,) + x.shape[1:], lambda i: (i,) + (0,) * (x.ndim - 1))
    return pl.pallas_call(
        functools.partial(_s5_main_kernel, nb=nb),
        grid=(S5_PAIRS,),
        in_specs=[pl.BlockSpec((rows, kc2), lambda i: (0, i)), blk3(tmat), blk3(bsre), blk3(bsim), blk3(csre),
                  blk3(csim), blk3(lamre), blk3(lamim),
                  pl.BlockSpec((nb, p2), lambda i: (0, i)), pl.BlockSpec((nb, p2), lambda i: (0, i))],
        out_specs=[pl.BlockSpec((rows, kc2), lambda i: (0, i)),
                   pl.BlockSpec((nb, p2), lambda i: (0, i)), pl.BlockSpec((nb, p2), lambda i: (0, i))],
        out_shape=[jax.ShapeDtypeStruct((rows, width), BF16),
                   jax.ShapeDtypeStruct((nb, S5_GROUPS * S5_STATE), F32),
                   jax.ShapeDtypeStruct((nb, S5_GROUPS * S5_STATE), F32)],
        scratch_shapes=[pltpu.VMEM((rows, p2), F32)] * 4,
        compiler_params=pltpu.CompilerParams(dimension_semantics=("parallel",), vmem_limit_bytes=VMEM_LIMIT),
        name="s5_main",
    )(uc, tmat, bsre, bsim, csre, csim, lamre, lamim, x0re, x0im)


def s5_mix(u, x0re, x0im, tables, chunk):
    b, t, e = u.shape
    n = t // chunk
    g, cg = S5_GROUPS, S5_GROUP
    uc = u.reshape(b, n, chunk, g, cg).transpose(1, 0, 3, 2, 4).reshape(n * b, g * chunk * cg)
    yc, xre, xim = s5_main(uc, *tables, x0re.reshape(b, -1), x0im.reshape(b, -1))
    y = yc.reshape(n, b, g, chunk, cg).transpose(1, 0, 3, 2, 4).reshape(b * t, e)
    return y, xre.reshape(b, g, S5_STATE), xim.reshape(b, g, S5_STATE)


def _trunk(x, st_shift, st_wkv, st_gla, st_re, st_im, w):
    b, t, d = x.shape
    m = b * t
    x2 = x.reshape(m, d)

    p_r, p_g, p_xa = norm_proj(x2, w['norm_mix_pre'][0], [w['w_in_rwkv'], w['w_in_gla'], w['w_in_xa']],
                               [F32, F32, F32])
    o_r, wkv = rwkv_mix(p_r.reshape(b, t, -1), st_shift[0], st_wkv[0], w['rwkv'])
    o_g, gla = gla_mix(p_g.reshape(b, t, -1), p_xa.reshape(b, t, -1), st_gla[0], w['gla_wa2'], w['gla_ba'],
                       w['gla_norm'])
    shift = p_r.reshape(b, t, -1)[:, -1]
    x2 = proj_post([o_r.reshape(m, -1), o_g.reshape(m, -1)], [w['w_out_rwkv'], w['w_out_gla']],
                   w['norm_mix_post'][0], x2)
    x2 = ffn(x2, w['norm_ffn_pre'][0], w['w_ff_up'][0], w['w_ff_down'][0], w['norm_ffn_post'][0])

    (u,) = norm_proj(x2, w['norm_mix_pre'][1], [w['s5_w_in']], [BF16])
    chunk = min(S5_CHUNK, t)
    y, s_re, s_im = s5_mix(u.reshape(b, t, d), st_re[0], st_im[0], w['s5_tables'][chunk], chunk)
    x2 = proj_post([y], [w['s5_w_out']], w['norm_mix_post'][1], x2, glu=True)
    x2 = ffn(x2, w['norm_ffn_pre'][1], w['w_ff_up'][1], w['w_ff_down'][1], w['norm_ffn_post'][1])
    return x2.reshape(b, t, d), shift[None], wkv[None], gla[None], s_re[None], s_im[None]


def kernel(x_prompt, x_sample, state_rwkv_shift, state_rwkv_wkv, state_gla, state_s5_re, state_s5_im,
           norm_mix_pre, norm_mix_post, norm_ffn_pre, norm_ffn_post,
           w_mix_in, w_mix_out, rwkv_mu, rwkv_w0, rwkv_w2, rwkv_a0, rwkv_a2, rwkv_g2,
           rwkv_k_k, rwkv_k_a, rwkv_r_k, rwkv_gn_g, rwkv_gn_b, gla_wa2, gla_ba, gla_norm,
           s5_w_in, s5_lam_re, s5_lam_im, s5_log_dt, s5_b_re, s5_b_im, s5_c_re, s5_c_im, s5_d, s5_w_out,
           w_ff_up, w_ff_down):
    assert norm_mix_pre.shape[0] == 2, "two layers: one RWKV-7/GLA layer, one S5 layer"
    bf = lambda a: a.astype(BF16)
    w_in = w_mix_in[0]
    gla0 = RWKV_PROJ
    xa0 = gla0 + 2 * GLA_K + GLA_V
    w_in_gla = jnp.concatenate([w_in[:, gla0:xa0], w_in[:, xa0 + GLA_LORA:]], axis=1)
    w_in_xa = jnp.pad(w_in[:, xa0:xa0 + GLA_LORA], ((0, 0), (0, LANES - GLA_LORA)))
    s5_prm = dict(lam_re=s5_lam_re[0], lam_im=s5_lam_im[0], log_dt=s5_log_dt[0], b_re=s5_b_re[0],
                  b_im=s5_b_im[0], c_re=s5_c_re[0], c_im=s5_c_im[0], d=s5_d[0])
    tables = {}
    for chunk in sorted({min(S5_CHUNK, x_prompt.shape[1]), min(S5_CHUNK, x_sample.shape[1])}):
        m_blk, bsre, bsim, csre, csim, lamre, lamim = s5_prep(s5_prm, chunk)
        tables[chunk] = (_toeplitz(m_blk, chunk), bsre, bsim, csre, csim, lamre, lamim)
    w = dict(
        norm_mix_pre=norm_mix_pre, norm_mix_post=norm_mix_post, norm_ffn_pre=norm_ffn_pre,
        norm_ffn_post=norm_ffn_post,
        w_in_rwkv=bf(w_in[:, :RWKV_PROJ]), w_in_gla=bf(w_in_gla), w_in_xa=bf(w_in_xa),
        w_out_rwkv=bf(w_mix_out[0, :RWKV_W]), w_out_gla=bf(w_mix_out[0, RWKV_W:]),
        rwkv=dict(mu=rwkv_mu[0], w0=rwkv_w0[0], w2=rwkv_w2[0], a0=rwkv_a0[0], a2=rwkv_a2[0], g2=rwkv_g2[0],
                  k_k=rwkv_k_k[0], k_a=rwkv_k_a[0], r_k=rwkv_r_k[0], gn_g=rwkv_gn_g[0], gn_b=rwkv_gn_b[0]),
        gla_wa2=bf(jnp.pad(gla_wa2[0], ((0, LANES - GLA_LORA), (0, 0)))), gla_ba=gla_ba[0], gla_norm=gla_norm[0],
        s5_w_in=bf(s5_w_in[0]), s5_w_out=bf(s5_w_out[0]), s5_tables=tables,
        w_ff_up=bf(w_ff_up), w_ff_down=bf(w_ff_down),
    )
    bp = x_prompt.shape[0]
    zeros = lambda s: jnp.zeros((s.shape[0], bp) + s.shape[2:], x_prompt.dtype)
    y_p, sh_p, wkv_p, gla_p, re_p, im_p = _trunk(x_prompt, zeros(state_rwkv_shift), zeros(state_rwkv_wkv),
                                                 zeros(state_gla), zeros(state_s5_re), zeros(state_s5_im), w)
    y_s, sh_s, wkv_s, gla_s, re_s, im_s = _trunk(x_sample, state_rwkv_shift, state_rwkv_wkv, state_gla,
                                                 state_s5_re, state_s5_im, w)
    return (y_p, y_s, sh_p, sh_s, wkv_p, wkv_s, gla_p, gla_s, re_p, re_s, im_p, im_s)
```

```python
import functools
import math

import jax
import jax.numpy as jnp
from jax import lax
from jax.experimental import pallas as pl
from jax.experimental.pallas import tpu as pltpu

F32 = jnp.float32
BF16 = jnp.bfloat16

D_MODEL = 1024
D_FF = 4096
NORM_EPS = 1e-6
RWKV_HEADS = 8
RWKV_HEAD = 64
RWKV_W = RWKV_HEADS * RWKV_HEAD
RWKV_PROJ = 1792
RWKV_CHUNK = 64
RWKV_ROWS = 64
GN_EPS = 64e-5
_P_GRAM = 1
_P_INV = 1
_P_SOLVE = 1
_P_MIX = 1
_P_STATE = 1
GLA_HEADS = 4
GLA_DK = 64
GLA_DV = 128
GLA_K = GLA_HEADS * GLA_DK
GLA_V = GLA_HEADS * GLA_DV
GLA_LORA = 16
GLA_TAU = 16.0
GLA_CHUNK = 32
GLA_MAIN = 2 * GLA_K + 2 * GLA_V
GLA_TILE = 256
GLA_SMALL_ROWS = 64
S5_GROUPS = 64
S5_GROUP = 16
S5_STATE = 64
S5_CHUNK = 16
S5_ROWS = 256
LANES = 128
VMEM_LIMIT = 48 * 1024 * 1024

_NN = (((1,), (0,)), ((), ()))
_NT = (((1,), (1,)), ((), ()))
_TN = (((0,), (0,)), ((), ()))


def _dg(a, b, dn=_NN):
    return lax.dot_general(a.astype(BF16), b.astype(BF16), dn, preferred_element_type=F32)


def _split2(x):
    hi = x.astype(BF16)
    lo = (x - hi.astype(F32)).astype(BF16)
    return hi, lo


def _split3(x):
    hi = x.astype(BF16)
    r1 = x - hi.astype(F32)
    mid = r1.astype(BF16)
    lo = (r1 - mid.astype(F32)).astype(BF16)
    return hi, mid, lo


def _dg3(a, b, dn=_NN):
    ah, al = _split2(a)
    bh, bl = _split2(b)
    d = lambda x, y: lax.dot_general(x, y, dn, preferred_element_type=F32)
    return d(ah, bh) + (d(ah, bl) + d(al, bh))


def _mp(a, b, dn, passes):
    return _dg(a, b, dn) if passes == 1 else _dg3(a, b, dn)


def _dg_sel(sel, x, dn=_NN):
    d = lambda y: lax.dot_general(sel, y, dn, preferred_element_type=F32)
    h, m, l = _split3(x)
    return d(h) + (d(m) + d(l))


def _dg_selr(x, sel, dn=_NN):
    d = lambda y: lax.dot_general(y, sel, dn, preferred_element_type=F32)
    h, m, l = _split3(x)
    return d(h) + (d(m) + d(l))


def _rms(x, g):
    return x * lax.rsqrt(jnp.mean(x * x, axis=-1, keepdims=True) + NORM_EPS) * g


def _row_tile(m, want):
    t = min(m, want)
    assert m % t == 0, (m, t)
    return t


def _full(shape):
    nd = len(shape)
    return pl.BlockSpec(shape, lambda *_: (0,) * nd)


def _norm_proj_kernel(x_ref, g_ref, *refs):
    n = len(refs) // 2
    h = _rms(x_ref[...], g_ref[...]).astype(BF16)
    for w_ref, o_ref in zip(refs[:n], refs[n:]):
        o_ref[...] = jnp.dot(h, w_ref[...], preferred_element_type=F32).astype(o_ref.dtype)


def norm_proj(x, g, ws, out_dtypes, tm=512):
    m, d = x.shape
    tm = _row_tile(m, tm)
    return pl.pallas_call(
        _norm_proj_kernel,
        grid=(m // tm,),
        in_specs=[pl.BlockSpec((tm, d), lambda i: (i, 0)), _full((1, d))]
        + [_full(w.shape) for w in ws],
        out_specs=[pl.BlockSpec((tm, w.shape[1]), lambda i: (i, 0)) for w in ws],
        out_shape=[jax.ShapeDtypeStruct((m, w.shape[1]), dt) for w, dt in zip(ws, out_dtypes)],
        compiler_params=pltpu.CompilerParams(dimension_semantics=("parallel",), vmem_limit_bytes=VMEM_LIMIT),
        name="norm_proj",
    )(x, g.reshape(1, d), *ws)


def _proj_post_kernel(*refs, n_in, glu):
    a_refs = refs[:n_in]
    w_refs = refs[n_in:2 * n_in]
    g_ref, x_ref, o_ref = refs[2 * n_in:]
    m = None
    for a_ref, w_ref in zip(a_refs, w_refs):
        t = jnp.dot(a_ref[...].astype(BF16), w_ref[...], preferred_element_type=F32)
        m = t if m is None else m + t
    if glu:
        m = m[:, :D_MODEL] * jax.nn.sigmoid(m[:, D_MODEL:])
    o_ref[...] = x_ref[...] + _rms(m, g_ref[...])


def proj_post(a_list, w_list, g, x, glu=False, tm=512):
    m, d = x.shape
    tm = _row_tile(m, tm)
    n_in = len(a_list)
    return pl.pallas_call(
        functools.partial(_proj_post_kernel, n_in=n_in, glu=glu),
        grid=(m // tm,),
        in_specs=[pl.BlockSpec((tm, a.shape[1]), lambda i: (i, 0)) for a in a_list]
        + [_full(w.shape) for w in w_list]
        + [_full((1, d)), pl.BlockSpec((tm, d), lambda i: (i, 0))],
        out_specs=pl.BlockSpec((tm, d), lambda i: (i, 0)),
        out_shape=jax.ShapeDtypeStruct((m, d), F32),
        compiler_params=pltpu.CompilerParams(dimension_semantics=("parallel",), vmem_limit_bytes=VMEM_LIMIT),
        name="proj_post",
    )(*a_list, *w_list, g.reshape(1, d), x)


def _ffn_kernel(x_ref, gpre_ref, wup_ref, wdn_ref, gpost_ref, o_ref, h_scr, acc_scr):
    j = pl.program_id(1)

    @pl.when(j == 0)
    def _():
        h_scr[...] = _rms(x_ref[...], gpre_ref[...]).astype(BF16)
        acc_scr[...] = jnp.zeros_like(acc_scr)

    u = jnp.dot(h_scr[...], wup_ref[...], preferred_element_type=F32)
    u = jnp.square(jnp.maximum(u, 0.0)).astype(BF16)
    acc_scr[...] += jnp.dot(u, wdn_ref[...], preferred_element_type=F32)

    @pl.when(j == pl.num_programs(1) - 1)
    def _():
        o_ref[...] = x_ref[...] + _rms(acc_scr[...], gpost_ref[...])


def ffn(x, g_pre, w_up, w_down, g_post, tm=1024, tf=512):
    m, d = x.shape
    f = w_up.shape[1]
    tm = _row_tile(m, tm)
    return pl.pallas_call(
        _ffn_kernel,
        grid=(m // tm, f // tf),
        in_specs=[
            pl.BlockSpec((tm, d), lambda i, j: (i, 0)),
            _full((1, d)),
            pl.BlockSpec((d, tf), lambda i, j: (0, j)),
            pl.BlockSpec((tf, d), lambda i, j: (j, 0)),
            _full((1, d)),
        ],
        out_specs=pl.BlockSpec((tm, d), lambda i, j: (i, 0)),
        out_shape=jax.ShapeDtypeStruct((m, d), F32),
        scratch_shapes=[pltpu.VMEM((tm, d), BF16), pltpu.VMEM((tm, d), F32)],
        compiler_params=pltpu.CompilerParams(dimension_semantics=("parallel", "arbitrary"),
                                             vmem_limit_bytes=VMEM_LIMIT),
        name="ffn",
    )(x, g_pre.reshape(1, d), w_up, w_down, g_post.reshape(1, d))


def _head_sum(x, ones_bd):
    hi, lo = _split2(x)
    d = lambda y: jnp.dot(y, ones_bd, preferred_element_type=F32)
    return d(hi) + d(lo)


def _rwkv_kernel(p_ref, sh0_ref, s0_ref, mu_ref, w0_ref, w2_ref, a0_ref, a2_ref, g2_ref, kk_ref, ka_ref,
                 rk_ref, gng_ref, gnb_ref, ones_ref, o_ref, sout_ref, state_scr, carry_scr, y_scr, *, chunk):
    c = chunk
    nb = p_ref.shape[0]
    rows = nb * c
    shift = int(math.log2(c))
    n = pl.program_id(1)

    @pl.when(n == 0)
    def _():
        carry_scr[...] = sh0_ref[...]
        state_scr[...] = s0_ref[...]

    p3 = p_ref[...]
    p = p3.reshape(rows, RWKV_PROJ)
    t_idx = jnp.bitwise_and(lax.broadcasted_iota(jnp.int32, (rows, 1), 0), c - 1)
    carry = jnp.broadcast_to(carry_scr[...], (nb, c, RWKV_PROJ)).reshape(rows, RWKV_PROJ)
    prev = jnp.where(t_idx == 0, carry, pltpu.roll(p, 1, 0))
    carry_scr[...] = p3[:, c - 1:c, :]
    z = p + (prev - p) * mu_ref[...]

    w_ = RWKV_W
    r = z[:, 0:w_]
    k = z[:, w_:2 * w_]
    v = z[:, 2 * w_:3 * w_]
    xw = z[:, 3 * w_:3 * w_ + 64]
    xa = z[:, 3 * w_ + 64:3 * w_ + 128]
    xg = z[:, 3 * w_ + 128:3 * w_ + 256]

    wlog = -jax.nn.softplus(-(w0_ref[...] + _dg(jnp.tanh(xw), w2_ref[...]))) - 0.5
    logd = -jnp.exp(wlog)
    a = jax.nn.sigmoid(a0_ref[...] + _dg(xa, a2_ref[...]))
    g = _dg(jax.nn.sigmoid(xg), g2_ref[...])

    ones_bd = ones_ref[...]
    kk = k * kk_ref[...]
    kk = kk * lax.rsqrt(jnp.maximum(_head_sum(kk * kk, ones_bd), 1e-24))
    k2 = k * (1.0 + (a - 1.0) * ka_ref[...])
    bb = kk * a

    ri = lax.broadcasted_iota(jnp.int32, (rows, rows), 0)
    ci = lax.broadcasted_iota(jnp.int32, (rows, rows), 1)
    same = lax.shift_right_logical(ri, shift) == lax.shift_right_logical(ci, shift)
    lower = jnp.logical_and(same, ri > ci)
    lower_eq = jnp.logical_and(same, ri >= ci)
    eye = (ri == ci).astype(F32)
    cum = _dg_sel(lower_eq.astype(BF16), logd)
    tot = _dg_sel(same.astype(BF16), logd)
    g_in = jnp.exp(cum)
    g_ex = jnp.exp(cum - logd)
    g_inv = jnp.exp(-cum)
    g_hat = jnp.exp(tot - cum)
    g_all = jnp.exp(tot)
    rt = r * g_in
    kt = kk * g_ex
    ktil = k2 * g_inv
    btil = bb * g_inv
    khat = k2 * g_hat
    bhat = bb * g_hat

    heads = range(RWKV_HEADS)
    seqs = range(nb)
    hs = [slice(h * RWKV_HEAD, (h + 1) * RWKV_HEAD) for h in heads]
    kr = [jnp.concatenate([kt[:, s], rt[:, s]], axis=0) for s in hs]
    gb = [_mp(kr[h], btil[:, hs[h]], _NT, _P_GRAM) for h in heads]
    gk = [_mp(kr[h], ktil[:, hs[h]], _NT, _P_GRAM) for h in heads]
    a_b = [jnp.where(lower, x[:rows], 0.0) for x in gb]
    a_rb = [jnp.where(lower_eq, x[rows:], 0.0) for x in gb]
    a_kk = [jnp.concatenate([jnp.where(lower, x[:rows], 0.0), jnp.where(lower_eq, x[rows:], 0.0)], axis=0)
            for x in gk]
    ks = [[_mp(jnp.concatenate([kt[i * c:(i + 1) * c, hs[h]], rt[i * c:(i + 1) * c, hs[h]]], axis=0),
               state_scr[i, h], _NT, _P_STATE) for i in seqs] for h in heads]
    x0 = [jnp.concatenate([ks[h][i][:c] for i in seqs], axis=0) if nb > 1 else ks[h][0][:c] for h in heads]
    r0 = [jnp.concatenate([ks[h][i][c:] for i in seqs], axis=0) if nb > 1 else ks[h][0][c:] for h in heads]
    av = [_mp(a_kk[h], v[:, hs[h]], _NN, _P_MIX) for h in heads]
    tinv = [eye - x for x in a_b]
    pw = [_mp(x, x, _NN, _P_INV) for x in a_b]
    span = 2
    while span < c:
        span *= 2
        if span < c:
            pr = [_mp(jnp.concatenate([tinv[h], pw[h]], axis=0), pw[h], _NN, _P_INV) for h in heads]
            tinv = [tinv[h] + pr[h][:rows] for h in heads]
            pw = [pr[h][rows:] for h in heads]
        else:
            tinv = [tinv[h] + _mp(tinv[h], pw[h], _NN, _P_INV) for h in heads]
    u =[_mp(tinv[h], x0[h] + av[h][:rows], _NN, _P_SOLVE) for h in heads]
    for h in heads:
        y_scr[:, hs[h]] = r0[h] + av[h][rows:] - _mp(a_rb[h], u[h], _NN, _P_MIX)
    for h in heads:
        for i in seqs:
            rs = slice(i * c, (i + 1) * c)
            vu = jnp.concatenate([v[rs, hs[h]], -u[h][rs]], axis=0)
            kb = jnp.concatenate([khat[rs, hs[h]], bhat[rs, hs[h]]], axis=0)
            state_scr[i, h] = state_scr[i, h] * g_all[i * c:i * c + 1, hs[h]] + _mp(vu, kb, _TN, _P_STATE)

    y = y_scr[...]
    inv_n = 1.0 / RWKV_HEAD
    mean = _head_sum(y, ones_bd) * inv_n
    yc = y - mean
    var = _head_sum(yc * yc, ones_bd) * inv_n
    yn = yc * lax.rsqrt(var + GN_EPS) * gng_ref[...] + gnb_ref[...]
    bonus = _head_sum(r * k2 * rk_ref[...], ones_bd) * v
    o_ref[...] = ((yn + bonus) * g).reshape(nb, c, RWKV_W)

    @pl.when(n == pl.num_programs(1) - 1)
    def _():
        sout_ref[...] = state_scr[...]


def rwkv_mix(p, shift0, s0, prm):
    b, t, _ = p.shape
    c = min(RWKV_CHUNK, t)
    assert t % c == 0
    nb = min(b, RWKV_ROWS // c)
    assert b % nb == 0
    w_ = RWKV_W
    row = lambda x: x.reshape(1, -1)
    ones_bd = jnp.kron(jnp.eye(RWKV_HEADS, dtype=F32), jnp.ones((RWKV_HEAD, RWKV_HEAD), F32)).astype(BF16)
    consts = [row(prm['mu']), row(prm['w0']), prm['w2'].astype(BF16), row(prm['a0']), prm['a2'].astype(BF16),
              prm['g2'].astype(BF16), row(prm['k_k']), row(prm['k_a']), row(prm['r_k']), row(prm['gn_g']),
              row(prm['gn_b']), ones_bd]
    return pl.pallas_call(
        functools.partial(_rwkv_kernel, chunk=c),
        grid=(b // nb, t // c),
        in_specs=[
            pl.BlockSpec((nb, c, RWKV_PROJ), lambda i, j: (i, j, 0)),
            pl.BlockSpec((nb, 1, RWKV_PROJ), lambda i, j: (i, 0, 0)),
            pl.BlockSpec((nb, RWKV_HEADS, RWKV_HEAD, RWKV_HEAD), lambda i, j: (i, 0, 0, 0)),
        ] + [_full(x.shape) for x in consts],
        out_specs=[
            pl.BlockSpec((nb, c, w_), lambda i, j: (i, j, 0)),
            pl.BlockSpec((nb, RWKV_HEADS, RWKV_HEAD, RWKV_HEAD), lambda i, j: (i, 0, 0, 0)),
        ],
        out_shape=[jax.ShapeDtypeStruct((b, t, w_), F32),
                   jax.ShapeDtypeStruct((b, RWKV_HEADS, RWKV_HEAD, RWKV_HEAD), F32)],
        scratch_shapes=[pltpu.VMEM((nb, RWKV_HEADS, RWKV_HEAD, RWKV_HEAD), F32),
                        pltpu.VMEM((nb, 1, RWKV_PROJ), F32),
                        pltpu.VMEM((nb * c, w_), F32)],
        compiler_params=pltpu.CompilerParams(dimension_semantics=("parallel", "arbitrary")),
        name="rwkv_mix",
    )(p, shift0.reshape(b, 1, RWKV_PROJ), s0, *consts)


def _gla_kernel(pg_ref, xa_ref, s0_ref, wa2_ref, ba_ref, gn_ref, o_ref, sout_ref, state_scr, o_scr, *, chunk):
    c = chunk
    nb, tt, _ = pg_ref.shape
    rows = nb * tt
    nc = tt // c
    n = pl.program_id(1)
    heads = range(GLA_HEADS)
    seqs = range(nb)
    ident = lambda m: (lax.broadcasted_iota(jnp.int32, (m, m), 0)
                       == lax.broadcasted_iota(jnp.int32, (m, m), 1)).astype(BF16)

    @pl.when(n == 0)
    def _():
        eye_k = ident(GLA_DK)
        for i in seqs:
            for h in heads:
                state_scr[i, h] = _dg_selr(s0_ref[i, h], eye_k, _TN)

    pg = pg_ref[...].reshape(rows, GLA_MAIN)
    q = pg[:, 0:GLA_K] * (GLA_DK ** -0.5)
    k = pg[:, GLA_K:2 * GLA_K]
    v = pg[:, 2 * GLA_K:2 * GLA_K + GLA_V]
    gz = pg[:, 2 * GLA_K + GLA_V:]
    xa = xa_ref[...].reshape(rows, LANES)
    gk = jax.nn.log_sigmoid(_dg(xa, wa2_ref[...]) + ba_ref[...]) * (1.0 / GLA_TAU)

    shift = int(math.log2(c))
    ri = lax.broadcasted_iota(jnp.int32, (rows, rows), 0)
    ci = lax.broadcasted_iota(jnp.int32, (rows, rows), 1)
    same = lax.shift_right_logical(ri, shift) == lax.shift_right_logical(ci, shift)
    causal = jnp.logical_and(same, ri >= ci)
    bc = _dg_sel(causal.astype(BF16), gk)
    bl = _dg_sel(same.astype(BF16), gk)
    qt = q * jnp.exp(bc)
    kt = k * jnp.exp(-bc)
    ks = k * jnp.exp(bl - bc)
    ebl = jnp.exp(bl)

    ksl = [slice(h * GLA_DK, (h + 1) * GLA_DK) for h in heads]
    vsl = [slice(h * GLA_DV, (h + 1) * GLA_DV) for h in heads]
    chunks = [(i, slice(i * tt + j * c, i * tt + (j + 1) * c)) for i in seqs for j in range(nc)]
    att = [jnp.where(causal, _dg(qt[:, ksl[h]], kt[:, ksl[h]], _NT), 0.0) for h in heads]
    o_intra = [_dg(att[h], v[:, vsl[h]]) for h in heads]
    kv = [[_dg(v[rs, vsl[h]], ks[rs, ksl[h]], _TN) for _, rs in chunks] for h in heads]
    st = [[state_scr[i, h] for i in seqs] for h in heads]
    for ci_, (i, rs) in enumerate(chunks):
        for h in heads:
            o_scr[rs, vsl[h]] = o_intra[h][rs] + _dg(qt[rs, ksl[h]], st[h][i], _NT)
            st[h][i] = st[h][i] * ebl[rs.start:rs.start + 1, ksl[h]] + kv[h][ci_]
    for h in heads:
        for i in seqs:
            state_scr[i, h] = st[h][i]

    for h in heads:
        o_h = o_scr[:, vsl[h]]
        o_h = o_h * lax.rsqrt(jnp.mean(o_h * o_h, axis=-1, keepdims=True) + NORM_EPS) * gn_ref[...]
        o_scr[:, vsl[h]] = o_h * jax.nn.silu(gz[:, vsl[h]])
    o_ref[...] = o_scr[...].reshape(nb, tt, GLA_V)

    @pl.when(n == pl.num_programs(1) - 1)
    def _():
        eye_v = ident(GLA_DV)
        for i in seqs:
            for h in heads:
                sout_ref[i, h] = _dg_selr(state_scr[i, h], eye_v, _TN)


def gla_mix(pg, pxa, s0, wa2_pad, ba, g_norm):
    b, t, _ = pg.shape
    tt = min(GLA_TILE, t)
    c = min(GLA_CHUNK, t)
    nb = max(1, min(b, GLA_SMALL_ROWS // tt))
    assert t % tt == 0 and tt % c == 0 and b % nb == 0
    return pl.pallas_call(
        functools.partial(_gla_kernel, chunk=c),
        grid=(b // nb, t // tt),
        in_specs=[
            pl.BlockSpec((nb, tt, GLA_MAIN), lambda i, j: (i, j, 0)),
            pl.BlockSpec((nb, tt, LANES), lambda i, j: (i, j, 0)),
            pl.BlockSpec((nb, GLA_HEADS, GLA_DK, GLA_DV), lambda i, j: (i, 0, 0, 0)),
            _full(wa2_pad.shape), _full((1, GLA_K)), _full((1, GLA_DV)),
        ],
        out_specs=[
            pl.BlockSpec((nb, tt, GLA_V), lambda i, j: (i, j, 0)),
            pl.BlockSpec((nb, GLA_HEADS, GLA_DK, GLA_DV), lambda i, j: (i, 0, 0, 0)),
        ],
        out_shape=[jax.ShapeDtypeStruct((b, t, GLA_V), F32),
                   jax.ShapeDtypeStruct((b, GLA_HEADS, GLA_DK, GLA_DV), F32)],
        scratch_shapes=[pltpu.VMEM((nb, GLA_HEADS, GLA_DV, GLA_DK), F32), pltpu.VMEM((nb * tt, GLA_V), F32)],
        compiler_params=pltpu.CompilerParams(dimension_semantics=("parallel", "arbitrary")),
        name="gla_mix",
    )(pg, pxa, s0, wa2_pad, ba.reshape(1, GLA_K), g_norm.reshape(1, GLA_DV))


def _cmul(ar, ai, br, bi):
    return ar * br - ai * bi, ar * bi + ai * br


def _s5_prep_kernel(lre_ref, lim_ref, ldt_ref, cre_ref, cim_ref, btre_ref, btim_ref, d_ref,
                    tiles_ref, bsre_ref, bsim_ref, csre_ref, csim_ref, lamre_ref, lamim_ref, *, chunk):
    c = chunk
    p_ = S5_STATE
    gb = LANES // S5_GROUP
    tiles_ref[...] = jnp.zeros_like(tiles_ref)
    bsre_ref[...] = jnp.zeros_like(bsre_ref)
    bsim_ref[...] = jnp.zeros_like(bsim_ref)
    csre_ref[...] = jnp.zeros_like(csre_ref)
    csim_ref[...] = jnp.zeros_like(csim_ref)
    r16 = lax.broadcasted_iota(jnp.int32, (S5_GROUP, S5_GROUP), 0)
    c16 = lax.broadcasted_iota(jnp.int32, (S5_GROUP, S5_GROUP), 1)
    for gi in range(gb):
        lr = lre_ref[0, gi:gi + 1, :]
        li = lim_ref[0, gi:gi + 1, :]
        dt = jnp.exp(ldt_ref[0, gi:gi + 1, :])
        mag = jnp.exp(lr * dt)
        ang = li * dt
        abr, abi = mag * jnp.cos(ang), mag * jnp.sin(ang)
        den = lr * lr + li * li
        fr = ((abr - 1.0) * lr + abi * li) / den
        fi = (abi * lr - (abr - 1.0) * li) / den
        pows = [(jnp.ones_like(abr), jnp.zeros_like(abr))]
        for _ in range(c):
            pows.append(_cmul(pows[-1][0], pows[-1][1], abr, abi))
        cre, cim = cre_ref[0, gi], cim_ref[0, gi]
        btre, btim = btre_ref[0, gi], btim_ref[0, gi]
        q_re, q_im = [], []
        col = slice(gi * p_, (gi + 1) * p_)
        ch = slice(gi * S5_GROUP, (gi + 1) * S5_GROUP)
        for t in range(c):
            er, ei = _cmul(pows[t][0], pows[t][1], fr, fi)
            qr, qi = _cmul(cre, cim, er, ei)
            q_re.append(qr)
            q_im.append(qi)
            er, ei = _cmul(pows[c - 1 - t][0], pows[c - 1 - t][1], fr, fi)
            zr, zi = _cmul(btre, btim, er, ei)
            rows = slice(t * LANES + gi * S5_GROUP, t * LANES + (gi + 1) * S5_GROUP)
            bsre_ref[0, rows, col] = zr.astype(BF16)
            bsim_ref[0, rows, col] = zi.astype(BF16)
            kr, ki = pows[t + 1]
            csre_ref[0, rows, col] = (cre * kr - cim * ki).astype(BF16)
            csim_ref[0, rows, col] = (-cre * ki - cim * kr).astype(BF16)
        q_re = jnp.concatenate(q_re, axis=0)
        q_im = jnp.concatenate(q_im, axis=0)
        m = _dg3(btre, q_re, _NT) - _dg3(btim, q_im, _NT)
        for t in range(c):
            blk = m[:, t * S5_GROUP:(t + 1) * S5_GROUP]
            if t == 0:
                blk = blk + jnp.where(r16 == c16, d_ref[0, gi], 0.0)
            tiles_ref[0, t, ch, ch] = blk.astype(BF16)
        lamre_ref[0, :, col] = pows[c][0]
        lamim_ref[0, :, col] = pows[c][1]


def s5_prep(prm, chunk):
    g, p_ = S5_GROUPS, S5_STATE
    gb = LANES // S5_GROUP
    nblk = g // gb
    kc = chunk * LANES
    grp = lambda x: x.reshape((nblk, gb) + x.shape[1:])
    args = [grp(prm['lam_re']), grp(prm['lam_im']), grp(prm['log_dt'].reshape(g, 1)),
            grp(prm['c_re']), grp(prm['c_im']),
            grp(jnp.swapaxes(prm['b_re'], 1, 2)), grp(jnp.swapaxes(prm['b_im'], 1, 2)),
            grp(prm['d'].reshape(g, S5_GROUP, 1))]
    blk = lambda x: pl.BlockSpec((1,) + x.shape[1:], lambda i: (i,) + (0,) * (x.ndim - 1))
    out_shape = [jax.ShapeDtypeStruct((nblk, chunk, LANES, LANES), BF16)] + \
        [jax.ShapeDtypeStruct((nblk, kc, gb * p_), BF16)] * 4 + \
        [jax.ShapeDtypeStruct((nblk, 1, gb * p_), F32)] * 2
    return pl.pallas_call(
        functools.partial(_s5_prep_kernel, chunk=chunk),
        grid=(nblk,),
        in_specs=[blk(a) for a in args],
        out_specs=[blk(o) for o in out_shape],
        out_shape=out_shape,
        compiler_params=pltpu.CompilerParams(dimension_semantics=("parallel",)),
        name="s5_prep",
    )(*args)


def _s5_main_kernel(u_ref, tiles_ref, bsre_ref, bsim_ref, csre_ref, csim_ref, lamre_ref, lamim_ref, x0re_ref,
                    x0im_ref, y_ref, xre_ref, xim_ref, uc_scr, yc_scr, tf_scr, zre_scr, zim_scr, sre_scr, sim_scr,
                    xr_scr, xi_scr, *, chunk):
    c = chunk
    nb, tt, _ = u_ref.shape
    nct = tt // c
    rows = nct * nb
    j = pl.program_id(1)

    @pl.when(j == 0)
    def _():
        xr_scr[...] = x0re_ref[...]
        xi_scr[...] = x0im_ref[...]
        tf_scr[...] = jnp.zeros_like(tf_scr)
        for t in range(c):
            for t2 in range(t, c):
                tf_scr[t * LANES:(t + 1) * LANES, t2 * LANES:(t2 + 1) * LANES] = tiles_ref[0, t2 - t]

    for t in range(c):
        if nct == 1:
            uc_scr[t] = u_ref[:, t, :]
        else:
            for b in range(nb):
                uc_scr[t, pl.ds(b, nct, stride=nb), :] = u_ref[b, pl.ds(t, nct, stride=c), :]
    u = jnp.concatenate([uc_scr[t].astype(BF16) for t in range(c)], axis=1)
    zre_scr[...] = jnp.dot(u, bsre_ref[0], preferred_element_type=F32)
    zim_scr[...] = jnp.dot(u, bsim_ref[0], preferred_element_type=F32)
    lr, li = lamre_ref[0], lamim_ref[0]

    def step(i, carry):
        xr, xi = carry
        rs = pl.ds(pl.multiple_of(i * nb, nb), nb)
        sre_scr[rs, :] = xr
        sim_scr[rs, :] = xi
        return (lr * xr - li * xi + zre_scr[rs, :], li * xr + lr * xi + zim_scr[rs, :])

    xr, xi = lax.fori_loop(0, nct, step, (xr_scr[...], xi_scr[...]))
    xr_scr[...] = xr
    xi_scr[...] = xi
    sre = sre_scr[...].astype(BF16)
    sim = sim_scr[...].astype(BF16)
    for k in range(c // 2):
        cols = slice(2 * k * LANES, (2 * k + 2) * LANES)
        kk_ = (2 * k + 2) * LANES
        y = (jnp.dot(u[:, :kk_], tf_scr[:kk_, cols], preferred_element_type=F32)
             + lax.dot_general(sre, csre_ref[0, cols, :], _NT, preferred_element_type=F32)
             + lax.dot_general(sim, csim_ref[0, cols, :], _NT, preferred_element_type=F32))
        y = jax.nn.gelu(y)
        yc_scr[2 * k] = y[:, :LANES]
        yc_scr[2 * k + 1] = y[:, LANES:]
    for t in range(c):
        if nct == 1:
            y_ref[:, t, :] = yc_scr[t]
        else:
            for b in range(nb):
                y_ref[b, pl.ds(t, nct, stride=c), :] = yc_scr[t, pl.ds(b, nct, stride=nb), :]

    @pl.when(j == pl.num_programs(1) - 1)
    def _():
        xre_ref[...] = xr
        xim_ref[...] = xi


def s5_mix(u, x0re, x0im, tables, chunk):
    b, t, e = u.shape
    tiles, bsre, bsim, csre, csim, lamre, lamim = tables
    nblk = e // LANES
    gp = (LANES // S5_GROUP) * S5_STATE
    nct = min(t // chunk, max(1, S5_ROWS // b))
    assert (t // chunk) % nct == 0
    tt = nct * chunk
    rows = nct * b
    kc = chunk * LANES
    blk3 = lambda x: pl.BlockSpec((1,) + x.shape[1:], lambda i, j: (i,) + (0,) * (x.ndim - 1))
    y, xre, xim = pl.pallas_call(
        functools.partial(_s5_main_kernel, chunk=chunk),
        grid=(nblk, t // tt),
        in_specs=[pl.BlockSpec((b, tt, LANES), lambda i, j: (0, j, i)), blk3(tiles), blk3(bsre), blk3(bsim),
                  blk3(csre), blk3(csim), blk3(lamre), blk3(lamim),
                  pl.BlockSpec((b, gp), lambda i, j: (0, i)), pl.BlockSpec((b, gp), lambda i, j: (0, i))],
        out_specs=[pl.BlockSpec((b, tt, LANES), lambda i, j: (0, j, i)),
                   pl.BlockSpec((b, gp), lambda i, j: (0, i)), pl.BlockSpec((b, gp), lambda i, j: (0, i))],
        out_shape=[jax.ShapeDtypeStruct((b, t, e), F32),
                   jax.ShapeDtypeStruct((b, S5_GROUPS * S5_STATE), F32),
                   jax.ShapeDtypeStruct((b, S5_GROUPS * S5_STATE), F32)],
        scratch_shapes=[pltpu.VMEM((chunk, rows, LANES), F32), pltpu.VMEM((chunk, rows, LANES), F32),
                        pltpu.VMEM((kc, kc), BF16)]
        + [pltpu.VMEM((rows, gp), F32)] * 4 + [pltpu.VMEM((b, gp), F32)] * 2,
        compiler_params=pltpu.CompilerParams(dimension_semantics=("parallel", "arbitrary"),
                                             vmem_limit_bytes=VMEM_LIMIT),
        name="s5_main",
    )(u, tiles, bsre, bsim, csre, csim, lamre, lamim, x0re.reshape(b, -1), x0im.reshape(b, -1))
    return y, xre.reshape(b, S5_GROUPS, S5_STATE), xim.reshape(b, S5_GROUPS, S5_STATE)


def _trunk(x, st_shift, st_wkv, st_gla, st_re, st_im, w):
    b, t, d = x.shape
    m = b * t
    x2 = x.reshape(m, d)

    p_r, p_g, p_xa = norm_proj(x2, w['norm_mix_pre'][0], [w['w_in_rwkv'], w['w_in_gla'], w['w_in_xa']],
                               [F32, F32, F32])
    o_r, wkv = rwkv_mix(p_r.reshape(b, t, -1), st_shift[0], st_wkv[0], w['rwkv'])
    o_g, gla = gla_mix(p_g.reshape(b, t, -1), p_xa.reshape(b, t, -1), st_gla[0], w['gla_wa2'], w['gla_ba'],
                       w['gla_norm'])
    shift = p_r.reshape(b, t, -1)[:, -1]
    x2 = proj_post([o_r.reshape(m, -1), o_g.reshape(m, -1)], [w['w_out_rwkv'], w['w_out_gla']],
                   w['norm_mix_post'][0], x2)
    x2 = ffn(x2, w['norm_ffn_pre'][0], w['w_ff_up'][0], w['w_ff_down'][0], w['norm_ffn_post'][0])

    (u,) = norm_proj(x2, w['norm_mix_pre'][1], [w['s5_w_in']], [F32])
    chunk = min(S5_CHUNK, t)
    y, s_re, s_im = s5_mix(u.reshape(b, t, d), st_re[0], st_im[0], w['s5_tables'][chunk], chunk)
    x2 = proj_post([y.reshape(m, d)], [w['s5_w_out']], w['norm_mix_post'][1], x2, glu=True)
    x2 = ffn(x2, w['norm_ffn_pre'][1], w['w_ff_up'][1], w['w_ff_down'][1], w['norm_ffn_post'][1])
    return x2.reshape(b, t, d), shift[None], wkv[None], gla[None], s_re[None], s_im[None]


def kernel(x_prompt, x_sample, state_rwkv_shift, state_rwkv_wkv, state_gla, state_s5_re, state_s5_im,
           norm_mix_pre, norm_mix_post, norm_ffn_pre, norm_ffn_post,
           w_mix_in, w_mix_out, rwkv_mu, rwkv_w0, rwkv_w2, rwkv_a0, rwkv_a2, rwkv_g2,
           rwkv_k_k, rwkv_k_a, rwkv_r_k, rwkv_gn_g, rwkv_gn_b, gla_wa2, gla_ba, gla_norm,
           s5_w_in, s5_lam_re, s5_lam_im, s5_log_dt, s5_b_re, s5_b_im, s5_c_re, s5_c_im, s5_d, s5_w_out,
           w_ff_up, w_ff_down):
    assert norm_mix_pre.shape[0] == 2, "two layers: one RWKV-7/GLA layer, one S5 layer"
    bf = lambda a: a.astype(BF16)
    w_in = w_mix_in[0]
    gla0 = RWKV_PROJ
    xa0 = gla0 + 2 * GLA_K + GLA_V
    w_in_gla = jnp.concatenate([w_in[:, gla0:xa0], w_in[:, xa0 + GLA_LORA:]], axis=1)
    w_in_xa = jnp.pad(w_in[:, xa0:xa0 + GLA_LORA], ((0, 0), (0, LANES - GLA_LORA)))
    s5_prm = dict(lam_re=s5_lam_re[0], lam_im=s5_lam_im[0], log_dt=s5_log_dt[0], b_re=s5_b_re[0],
                  b_im=s5_b_im[0], c_re=s5_c_re[0], c_im=s5_c_im[0], d=s5_d[0])
    tables = {}
    for chunk in sorted({min(S5_CHUNK, x_prompt.shape[1]), min(S5_CHUNK, x_sample.shape[1])}):
        tables[chunk] = s5_prep(s5_prm, chunk)
    w = dict(
        norm_mix_pre=norm_mix_pre, norm_mix_post=norm_mix_post, norm_ffn_pre=norm_ffn_pre,
        norm_ffn_post=norm_ffn_post,
        w_in_rwkv=bf(w_in[:, :RWKV_PROJ]), w_in_gla=bf(w_in_gla), w_in_xa=bf(w_in_xa),
        w_out_rwkv=bf(w_mix_out[0, :RWKV_W]), w_out_gla=bf(w_mix_out[0, RWKV_W:]),
        rwkv=dict(mu=rwkv_mu[0], w0=rwkv_w0[0], w2=rwkv_w2[0], a0=rwkv_a0[0], a2=rwkv_a2[0], g2=rwkv_g2[0],
                  k_k=rwkv_k_k[0], k_a=rwkv_k_a[0], r_k=rwkv_r_k[0], gn_g=rwkv_gn_g[0], gn_b=rwkv_gn_b[0]),
        gla_wa2=bf(jnp.pad(gla_wa2[0], ((0, LANES - GLA_LORA), (0, 0)))), gla_ba=gla_ba[0], gla_norm=gla_norm[0],
        s5_w_in=bf(s5_w_in[0]), s5_w_out=bf(s5_w_out[0]), s5_tables=tables,
        w_ff_up=bf(w_ff_up), w_ff_down=bf(w_ff_down),
    )
    bp = x_prompt.shape[0]
    zeros = lambda s: jnp.zeros((s.shape[0], bp) + s.shape[2:], x_prompt.dtype)
    y_p, sh_p, wkv_p, gla_p, re_p, im_p = _trunk(x_prompt, zeros(state_rwkv_shift), zeros(state_rwkv_wkv),
                                                 zeros(state_gla), zeros(state_s5_re), zeros(state_s5_im), w)
    y_s, sh_s, wkv_s, gla_s, re_s, im_s = _trunk(x_sample, state_rwkv_shift, state_rwkv_wkv, state_gla,
                                                 state_s5_re, state_s5_im, w)
    return (y_p, y_s, sh_p, sh_s, wkv_p, wkv_s, gla_p, gla_s, re_p, re_s, im_p, im_s)
```

```python
import functools
import math

import jax
import jax.numpy as jnp
from jax import lax
from jax.experimental import pallas as pl
from jax.experimental.pallas import tpu as pltpu

F32 = jnp.float32
BF16 = jnp.bfloat16

D_MODEL = 1024
D_FF = 4096
NORM_EPS = 1e-6
RWKV_HEADS = 8
RWKV_HEAD = 64
RWKV_W = RWKV_HEADS * RWKV_HEAD
RWKV_PROJ = 1792
RWKV_CHUNK = 64
RWKV_ROWS = 64
RWKV_STEP_ROWS = 256
GN_EPS = 64e-5
_P_GRAM = 1
_P_INV = 1
_P_SOLVE = 1
_P_MIX = 1
_P_STATE = 1
GLA_HEADS = 4
GLA_DK = 64
GLA_DV = 128
GLA_K = GLA_HEADS * GLA_DK
GLA_V = GLA_HEADS * GLA_DV
GLA_LORA = 16
GLA_TAU = 16.0
GLA_CHUNK = 32
GLA_MAIN = 2 * GLA_K + 2 * GLA_V
GLA_TILE = 256
GLA_SMALL_ROWS = 64
S5_GROUPS = 64
S5_GROUP = 16
S5_STATE = 64
S5_CHUNK = 16
S5_ROWS = 512
LANES = 128
VMEM_LIMIT = 48 * 1024 * 1024

_NN = (((1,), (0,)), ((), ()))
_NT = (((1,), (1,)), ((), ()))
_TN = (((0,), (0,)), ((), ()))


def _dg(a, b, dn=_NN):
    return lax.dot_general(a.astype(BF16), b.astype(BF16), dn, preferred_element_type=F32)


def _split2(x):
    hi = x.astype(BF16)
    lo = (x - hi.astype(F32)).astype(BF16)
    return hi, lo


def _split3(x):
    hi = x.astype(BF16)
    r1 = x - hi.astype(F32)
    mid = r1.astype(BF16)
    lo = (r1 - mid.astype(F32)).astype(BF16)
    return hi, mid, lo


def _dg3(a, b, dn=_NN):
    ah, al = _split2(a)
    bh, bl = _split2(b)
    d = lambda x, y: lax.dot_general(x, y, dn, preferred_element_type=F32)
    return d(ah, bh) + (d(ah, bl) + d(al, bh))


def _mp(a, b, dn, passes):
    return _dg(a, b, dn) if passes == 1 else _dg3(a, b, dn)


def _dg_sel(sel, x, dn=_NN):
    d = lambda y: lax.dot_general(sel, y, dn, preferred_element_type=F32)
    h, m, l = _split3(x)
    return d(h) + (d(m) + d(l))


def _dg_selr(x, sel, dn=_NN):
    d = lambda y: lax.dot_general(y, sel, dn, preferred_element_type=F32)
    h, m, l = _split3(x)
    return d(h) + (d(m) + d(l))


def _rms(x, g):
    return x * lax.rsqrt(jnp.mean(x * x, axis=-1, keepdims=True) + NORM_EPS) * g


def _row_tile(m, want):
    t = min(m, want)
    assert m % t == 0, (m, t)
    return t


def _full(shape):
    nd = len(shape)
    return pl.BlockSpec(shape, lambda *_: (0,) * nd)


def _norm_proj_kernel(x_ref, g_ref, *refs):
    n = len(refs) // 2
    h = _rms(x_ref[...], g_ref[...]).astype(BF16)
    for w_ref, o_ref in zip(refs[:n], refs[n:]):
        o_ref[...] = jnp.dot(h, w_ref[...], preferred_element_type=F32).astype(o_ref.dtype)


def norm_proj(x, g, ws, out_dtypes, tm=512):
    m, d = x.shape
    tm = _row_tile(m, tm)
    return pl.pallas_call(
        _norm_proj_kernel,
        grid=(m // tm,),
        in_specs=[pl.BlockSpec((tm, d), lambda i: (i, 0)), _full((1, d))]
        + [_full(w.shape) for w in ws],
        out_specs=[pl.BlockSpec((tm, w.shape[1]), lambda i: (i, 0)) for w in ws],
        out_shape=[jax.ShapeDtypeStruct((m, w.shape[1]), dt) for w, dt in zip(ws, out_dtypes)],
        compiler_params=pltpu.CompilerParams(dimension_semantics=("parallel",), vmem_limit_bytes=VMEM_LIMIT),
        name="norm_proj",
    )(x, g.reshape(1, d), *ws)


def _proj_post_kernel(*refs, n_in, glu):
    a_refs = refs[:n_in]
    w_refs = refs[n_in:2 * n_in]
    g_ref, x_ref, o_ref = refs[2 * n_in:]
    m = None
    for a_ref, w_ref in zip(a_refs, w_refs):
        t = jnp.dot(a_ref[...].astype(BF16), w_ref[...], preferred_element_type=F32)
        m = t if m is None else m + t
    if glu:
        m = m[:, :D_MODEL] * jax.nn.sigmoid(m[:, D_MODEL:])
    o_ref[...] = x_ref[...] + _rms(m, g_ref[...])


def proj_post(a_list, w_list, g, x, glu=False, tm=512):
    m, d = x.shape
    tm = _row_tile(m, tm)
    n_in = len(a_list)
    return pl.pallas_call(
        functools.partial(_proj_post_kernel, n_in=n_in, glu=glu),
        grid=(m // tm,),
        in_specs=[pl.BlockSpec((tm, a.shape[1]), lambda i: (i, 0)) for a in a_list]
        + [_full(w.shape) for w in w_list]
        + [_full((1, d)), pl.BlockSpec((tm, d), lambda i: (i, 0))],
        out_specs=pl.BlockSpec((tm, d), lambda i: (i, 0)),
        out_shape=jax.ShapeDtypeStruct((m, d), F32),
        compiler_params=pltpu.CompilerParams(dimension_semantics=("parallel",), vmem_limit_bytes=VMEM_LIMIT),
        name="proj_post",
    )(*a_list, *w_list, g.reshape(1, d), x)


def _ffn_kernel(x_ref, gpre_ref, wup_ref, wdn_ref, gpost_ref, o_ref, h_scr, acc_scr):
    j = pl.program_id(1)

    @pl.when(j == 0)
    def _():
        h_scr[...] = _rms(x_ref[...], gpre_ref[...]).astype(BF16)
        acc_scr[...] = jnp.zeros_like(acc_scr)

    u = jnp.dot(h_scr[...], wup_ref[...], preferred_element_type=F32)
    u = jnp.square(jnp.maximum(u, 0.0)).astype(BF16)
    acc_scr[...] += jnp.dot(u, wdn_ref[...], preferred_element_type=F32)

    @pl.when(j == pl.num_programs(1) - 1)
    def _():
        o_ref[...] = x_ref[...] + _rms(acc_scr[...], gpost_ref[...])


def ffn(x, g_pre, w_up, w_down, g_post, tm=1024, tf=512):
    m, d = x.shape
    f = w_up.shape[1]
    tm = _row_tile(m, tm)
    return pl.pallas_call(
        _ffn_kernel,
        grid=(m // tm, f // tf),
        in_specs=[
            pl.BlockSpec((tm, d), lambda i, j: (i, 0)),
            _full((1, d)),
            pl.BlockSpec((d, tf), lambda i, j: (0, j)),
            pl.BlockSpec((tf, d), lambda i, j: (j, 0)),
            _full((1, d)),
        ],
        out_specs=pl.BlockSpec((tm, d), lambda i, j: (i, 0)),
        out_shape=jax.ShapeDtypeStruct((m, d), F32),
        scratch_shapes=[pltpu.VMEM((tm, d), BF16), pltpu.VMEM((tm, d), F32)],
        compiler_params=pltpu.CompilerParams(dimension_semantics=("parallel", "arbitrary"),
                                             vmem_limit_bytes=VMEM_LIMIT),
        name="ffn",
    )(x, g_pre.reshape(1, d), w_up, w_down, g_post.reshape(1, d))


def _head_sum(x, ones_bd):
    hi, lo = _split2(x)
    d = lambda y: jnp.dot(y, ones_bd, preferred_element_type=F32)
    return d(hi) + d(lo)


def _rwkv_kernel(p_ref, sh0_ref, s0_ref, mu_ref, w0_ref, w2_ref, a0_ref, a2_ref, g2_ref, kk_ref, ka_ref,
                 rk_ref, gng_ref, gnb_ref, ones_ref, o_ref, sout_ref, state_scr, carry_scr, y_scr, *, chunk):
    c = chunk
    nb = p_ref.shape[0]
    rows = nb * c
    shift = int(math.log2(c))
    n = pl.program_id(1)

    @pl.when(n == 0)
    def _():
        carry_scr[...] = sh0_ref[...]
        state_scr[...] = s0_ref[...]

    p3 = p_ref[...]
    p = p3.reshape(rows, RWKV_PROJ)
    t_idx = jnp.bitwise_and(lax.broadcasted_iota(jnp.int32, (rows, 1), 0), c - 1)
    carry = jnp.broadcast_to(carry_scr[...], (nb, c, RWKV_PROJ)).reshape(rows, RWKV_PROJ)
    prev = jnp.where(t_idx == 0, carry, pltpu.roll(p, 1, 0))
    carry_scr[...] = p3[:, c - 1:c, :]
    z = p + (prev - p) * mu_ref[...]

    w_ = RWKV_W
    r = z[:, 0:w_]
    k = z[:, w_:2 * w_]
    v = z[:, 2 * w_:3 * w_]
    xw = z[:, 3 * w_:3 * w_ + 64]
    xa = z[:, 3 * w_ + 64:3 * w_ + 128]
    xg = z[:, 3 * w_ + 128:3 * w_ + 256]

    wlog = -jax.nn.softplus(-(w0_ref[...] + _dg(jnp.tanh(xw), w2_ref[...]))) - 0.5
    logd = -jnp.exp(wlog)
    a = jax.nn.sigmoid(a0_ref[...] + _dg(xa, a2_ref[...]))
    g = _dg(jax.nn.sigmoid(xg), g2_ref[...])

    ones_bd = ones_ref[...]
    kk = k * kk_ref[...]
    kk = kk * lax.rsqrt(jnp.maximum(_head_sum(kk * kk, ones_bd), 1e-24))
    k2 = k * (1.0 + (a - 1.0) * ka_ref[...])
    bb = kk * a

    def masks(m):
        ri = lax.broadcasted_iota(jnp.int32, (m, m), 0)
        ci = lax.broadcasted_iota(jnp.int32, (m, m), 1)
        same = lax.shift_right_logical(ri, shift) == lax.shift_right_logical(ci, shift)
        return same, jnp.logical_and(same, ri > ci), jnp.logical_and(same, ri >= ci), (ri == ci).astype(F32)

    same_all, _, lower_eq_all, _ = masks(rows)
    cum = _dg_sel(lower_eq_all.astype(BF16), logd)
    tot = _dg_sel(same_all.astype(BF16), logd)
    g_in = jnp.exp(cum)
    g_ex = jnp.exp(cum - logd)
    g_inv = jnp.exp(-cum)
    g_hat = jnp.exp(tot - cum)
    g_all = jnp.exp(tot)
    rt = r * g_in
    kt = kk * g_ex
    ktil = k2 * g_inv
    btil = bb * g_inv
    khat = k2 * g_hat
    bhat = bb * g_hat

    gr = min(rows, RWKV_ROWS)
    spg = gr // c
    _, lower, lower_eq, eye = masks(gr)
    probs = [(g_, h) for g_ in range(rows // gr) for h in range(RWKV_HEADS)]
    hsl = lambda h: slice(h * RWKV_HEAD, (h + 1) * RWKV_HEAD)
    gsl = lambda g_: slice(g_ * gr, (g_ + 1) * gr)
    pick = lambda x, g_, h: x[gsl(g_), hsl(h)]
    seq_rows = lambda g_, i: slice(g_ * gr + i * c, g_ * gr + (i + 1) * c)
    kr = [jnp.concatenate([pick(kt, *q), pick(rt, *q)], axis=0) for q in probs]
    gb = [_mp(kr[j], pick(btil, *q), _NT, _P_GRAM) for j, q in enumerate(probs)]
    gk = [_mp(kr[j], pick(ktil, *q), _NT, _P_GRAM) for j, q in enumerate(probs)]
    a_b = [jnp.where(lower, x[:gr], 0.0) for x in gb]
    a_rb = [jnp.where(lower_eq, x[gr:], 0.0) for x in gb]
    a_kk = [jnp.concatenate([jnp.where(lower, x[:gr], 0.0), jnp.where(lower_eq, x[gr:], 0.0)], axis=0)
            for x in gk]
    ks = [[_mp(jnp.concatenate([kt[seq_rows(g_, i), hsl(h)], rt[seq_rows(g_, i), hsl(h)]], axis=0),
               state_scr[g_ * spg + i, h], _NT, _P_STATE) for i in range(spg)] for g_, h in probs]
    cat = lambda xs: jnp.concatenate(xs, axis=0) if len(xs) > 1 else xs[0]
    x0 = [cat([x[:c] for x in ks[j]]) for j in range(len(probs))]
    r0 = [cat([x[c:] for x in ks[j]]) for j in range(len(probs))]
    av = [_mp(a_kk[j], pick(v, *q), _NN, _P_MIX) for j, q in enumerate(probs)]
    tinv = [eye - x for x in a_b]
    pw = [_mp(x, x, _NN, _P_INV) for x in a_b]
    span = 2
    while span < c:
        span *= 2
        if span < c:
            pr = [_mp(jnp.concatenate([t_, p_], axis=0), p_, _NN, _P_INV) for t_, p_ in zip(tinv, pw)]
            tinv = [t_ + x[:gr] for t_, x in zip(tinv, pr)]
            pw = [x[gr:] for x in pr]
        else:
            tinv = [t_ + _mp(t_, p_, _NN, _P_INV) for t_, p_ in zip(tinv, pw)]
    u = [_mp(tinv[j], x0[j] + av[j][:gr], _NN, _P_SOLVE) for j in range(len(probs))]
    for j, (g_, h) in enumerate(probs):
        y_scr[gsl(g_), hsl(h)] = r0[j] + av[j][gr:] - _mp(a_rb[j], u[j], _NN, _P_MIX)
    for j, (g_, h) in enumerate(probs):
        for i in range(spg):
            rs = seq_rows(g_, i)
            vu = jnp.concatenate([v[rs, hsl(h)], -u[j][i * c:(i + 1) * c]], axis=0)
            kb = jnp.concatenate([khat[rs, hsl(h)], bhat[rs, hsl(h)]], axis=0)
            s_i = g_ * spg + i
            state_scr[s_i, h] = (state_scr[s_i, h] * g_all[rs.start:rs.start + 1, hsl(h)]
                                 + _mp(vu, kb, _TN, _P_STATE))

    y = y_scr[...]
    inv_n = 1.0 / RWKV_HEAD
    mean = _head_sum(y, ones_bd) * inv_n
    yc = y - mean
    var = _head_sum(yc * yc, ones_bd) * inv_n
    yn = yc * lax.rsqrt(var + GN_EPS) * gng_ref[...] + gnb_ref[...]
    bonus = _head_sum(r * k2 * rk_ref[...], ones_bd) * v
    o_ref[...] = ((yn + bonus) * g).reshape(nb, c, RWKV_W)

    @pl.when(n == pl.num_programs(1) - 1)
    def _():
        sout_ref[...] = state_scr[...]


def rwkv_mix(p, shift0, s0, prm):
    b, t, _ = p.shape
    c = min(RWKV_CHUNK, t)
    assert t % c == 0
    nb = min(b, RWKV_STEP_ROWS // c)
    assert b % nb == 0
    w_ = RWKV_W
    row = lambda x: x.reshape(1, -1)
    ones_bd = jnp.kron(jnp.eye(RWKV_HEADS, dtype=F32), jnp.ones((RWKV_HEAD, RWKV_HEAD), F32)).astype(BF16)
    consts = [row(prm['mu']), row(prm['w0']), prm['w2'].astype(BF16), row(prm['a0']), prm['a2'].astype(BF16),
              prm['g2'].astype(BF16), row(prm['k_k']), row(prm['k_a']), row(prm['r_k']), row(prm['gn_g']),
              row(prm['gn_b']), ones_bd]
    return pl.pallas_call(
        functools.partial(_rwkv_kernel, chunk=c),
        grid=(b // nb, t // c),
        in_specs=[
            pl.BlockSpec((nb, c, RWKV_PROJ), lambda i, j: (i, j, 0)),
            pl.BlockSpec((nb, 1, RWKV_PROJ), lambda i, j: (i, 0, 0)),
            pl.BlockSpec((nb, RWKV_HEADS, RWKV_HEAD, RWKV_HEAD), lambda i, j: (i, 0, 0, 0)),
        ] + [_full(x.shape) for x in consts],
        out_specs=[
            pl.BlockSpec((nb, c, w_), lambda i, j: (i, j, 0)),
            pl.BlockSpec((nb, RWKV_HEADS, RWKV_HEAD, RWKV_HEAD), lambda i, j: (i, 0, 0, 0)),
        ],
        out_shape=[jax.ShapeDtypeStruct((b, t, w_), F32),
                   jax.ShapeDtypeStruct((b, RWKV_HEADS, RWKV_HEAD, RWKV_HEAD), F32)],
        scratch_shapes=[pltpu.VMEM((nb, RWKV_HEADS, RWKV_HEAD, RWKV_HEAD), F32),
                        pltpu.VMEM((nb, 1, RWKV_PROJ), F32),
                        pltpu.VMEM((nb * c, w_), F32)],
        compiler_params=pltpu.CompilerParams(dimension_semantics=("parallel", "arbitrary")),
        name="rwkv_mix",
    )(p, shift0.reshape(b, 1, RWKV_PROJ), s0, *consts)


def _gla_kernel(pg_ref, xa_ref, s0_ref, wa2_ref, ba_ref, gn_ref, o_ref, sout_ref, state_scr, o_scr, *, chunk):
    c = chunk
    nb, tt, _ = pg_ref.shape
    rows = nb * tt
    nc = tt // c
    n = pl.program_id(1)
    heads = range(GLA_HEADS)
    seqs = range(nb)
    ident = lambda m: (lax.broadcasted_iota(jnp.int32, (m, m), 0)
                       == lax.broadcasted_iota(jnp.int32, (m, m), 1)).astype(BF16)

    @pl.when(n == 0)
    def _():
        eye_k = ident(GLA_DK)
        for i in seqs:
            for h in heads:
                state_scr[i, h] = _dg_selr(s0_ref[i, h], eye_k, _TN)

    pg = pg_ref[...].reshape(rows, GLA_MAIN)
    q = pg[:, 0:GLA_K] * (GLA_DK ** -0.5)
    k = pg[:, GLA_K:2 * GLA_K]
    v = pg[:, 2 * GLA_K:2 * GLA_K + GLA_V]
    gz = pg[:, 2 * GLA_K + GLA_V:]
    xa = xa_ref[...].reshape(rows, LANES)
    gk = jax.nn.log_sigmoid(_dg(xa, wa2_ref[...]) + ba_ref[...]) * (1.0 / GLA_TAU)

    shift = int(math.log2(c))
    ri = lax.broadcasted_iota(jnp.int32, (rows, rows), 0)
    ci = lax.broadcasted_iota(jnp.int32, (rows, rows), 1)
    same = lax.shift_right_logical(ri, shift) == lax.shift_right_logical(ci, shift)
    causal = jnp.logical_and(same, ri >= ci)
    bc = _dg_sel(causal.astype(BF16), gk)
    bl = _dg_sel(same.astype(BF16), gk)
    qt = q * jnp.exp(bc)
    kt = k * jnp.exp(-bc)
    ks = k * jnp.exp(bl - bc)
    ebl = jnp.exp(bl)

    ksl = [slice(h * GLA_DK, (h + 1) * GLA_DK) for h in heads]
    vsl = [slice(h * GLA_DV, (h + 1) * GLA_DV) for h in heads]
    chunks = [(i, slice(i * tt + j * c, i * tt + (j + 1) * c)) for i in seqs for j in range(nc)]
    att = [jnp.where(causal, _dg(qt[:, ksl[h]], kt[:, ksl[h]], _NT), 0.0) for h in heads]
    o_intra = [_dg(att[h], v[:, vsl[h]]) for h in heads]
    kv = [[_dg(v[rs, vsl[h]], ks[rs, ksl[h]], _TN) for _, rs in chunks] for h in heads]
    st = [[state_scr[i, h] for i in seqs] for h in heads]
    for ci_, (i, rs) in enumerate(chunks):
        for h in heads:
            o_scr[rs, vsl[h]] = o_intra[h][rs] + _dg(qt[rs, ksl[h]], st[h][i], _NT)
            st[h][i] = st[h][i] * ebl[rs.start:rs.start + 1, ksl[h]] + kv[h][ci_]
    for h in heads:
        for i in seqs:
            state_scr[i, h] = st[h][i]

    for h in heads:
        o_h = o_scr[:, vsl[h]]
        o_h = o_h * lax.rsqrt(jnp.mean(o_h * o_h, axis=-1, keepdims=True) + NORM_EPS) * gn_ref[...]
        o_scr[:, vsl[h]] = o_h * jax.nn.silu(gz[:, vsl[h]])
    o_ref[...] = o_scr[...].reshape(nb, tt, GLA_V)

    @pl.when(n == pl.num_programs(1) - 1)
    def _():
        eye_v = ident(GLA_DV)
        for i in seqs:
            for h in heads:
                sout_ref[i, h] = _dg_selr(state_scr[i, h], eye_v, _TN)


def gla_mix(pg, pxa, s0, wa2_pad, ba, g_norm):
    b, t, _ = pg.shape
    tt = min(GLA_TILE, t)
    c = min(GLA_CHUNK, t)
    nb = max(1, min(b, GLA_SMALL_ROWS // tt))
    assert t % tt == 0 and tt % c == 0 and b % nb == 0
    return pl.pallas_call(
        functools.partial(_gla_kernel, chunk=c),
        grid=(b // nb, t // tt),
        in_specs=[
            pl.BlockSpec((nb, tt, GLA_MAIN), lambda i, j: (i, j, 0)),
            pl.BlockSpec((nb, tt, LANES), lambda i, j: (i, j, 0)),
            pl.BlockSpec((nb, GLA_HEADS, GLA_DK, GLA_DV), lambda i, j: (i, 0, 0, 0)),
            _full(wa2_pad.shape), _full((1, GLA_K)), _full((1, GLA_DV)),
        ],
        out_specs=[
            pl.BlockSpec((nb, tt, GLA_V), lambda i, j: (i, j, 0)),
            pl.BlockSpec((nb, GLA_HEADS, GLA_DK, GLA_DV), lambda i, j: (i, 0, 0, 0)),
        ],
        out_shape=[jax.ShapeDtypeStruct((b, t, GLA_V), F32),
                   jax.ShapeDtypeStruct((b, GLA_HEADS, GLA_DK, GLA_DV), F32)],
        scratch_shapes=[pltpu.VMEM((nb, GLA_HEADS, GLA_DV, GLA_DK), F32), pltpu.VMEM((nb * tt, GLA_V), F32)],
        compiler_params=pltpu.CompilerParams(dimension_semantics=("parallel", "arbitrary")),
        name="gla_mix",
    )(pg, pxa, s0, wa2_pad, ba.reshape(1, GLA_K), g_norm.reshape(1, GLA_DV))


def _cmul(ar, ai, br, bi):
    return ar * br - ai * bi, ar * bi + ai * br


def _s5_prep_kernel(lre_ref, lim_ref, ldt_ref, cre_ref, cim_ref, btre_ref, btim_ref, d_ref,
                    tiles_ref, bsre_ref, bsim_ref, csre_ref, csim_ref, lamre_ref, lamim_ref, *, chunk):
    c = chunk
    p_ = S5_STATE
    gb = LANES // S5_GROUP
    tiles_ref[...] = jnp.zeros_like(tiles_ref)
    bsre_ref[...] = jnp.zeros_like(bsre_ref)
    bsim_ref[...] = jnp.zeros_like(bsim_ref)
    csre_ref[...] = jnp.zeros_like(csre_ref)
    csim_ref[...] = jnp.zeros_like(csim_ref)
    r16 = lax.broadcasted_iota(jnp.int32, (S5_GROUP, S5_GROUP), 0)
    c16 = lax.broadcasted_iota(jnp.int32, (S5_GROUP, S5_GROUP), 1)
    for gi in range(gb):
        lr = lre_ref[0, gi:gi + 1, :]
        li = lim_ref[0, gi:gi + 1, :]
        dt = jnp.exp(ldt_ref[0, gi:gi + 1, :])
        mag = jnp.exp(lr * dt)
        ang = li * dt
        abr, abi = mag * jnp.cos(ang), mag * jnp.sin(ang)
        den = lr * lr + li * li
        fr = ((abr - 1.0) * lr + abi * li) / den
        fi = (abi * lr - (abr - 1.0) * li) / den
        pows = [(jnp.ones_like(abr), jnp.zeros_like(abr))]
        for _ in range(c):
            pows.append(_cmul(pows[-1][0], pows[-1][1], abr, abi))
        cre, cim = cre_ref[0, gi], cim_ref[0, gi]
        btre, btim = btre_ref[0, gi], btim_ref[0, gi]
        q_re, q_im = [], []
        col = slice(gi * p_, (gi + 1) * p_)
        ch = slice(gi * S5_GROUP, (gi + 1) * S5_GROUP)
        for t in range(c):
            er, ei = _cmul(pows[t][0], pows[t][1], fr, fi)
            qr, qi = _cmul(cre, cim, er, ei)
            q_re.append(qr)
            q_im.append(qi)
            er, ei = _cmul(pows[c - 1 - t][0], pows[c - 1 - t][1], fr, fi)
            zr, zi = _cmul(btre, btim, er, ei)
            rows = slice(t * LANES + gi * S5_GROUP, t * LANES + (gi + 1) * S5_GROUP)
            bsre_ref[0, rows, col] = zr.astype(BF16)
            bsim_ref[0, rows, col] = zi.astype(BF16)
            kr, ki = pows[t + 1]
            csre_ref[0, rows, col] = (cre * kr - cim * ki).astype(BF16)
            csim_ref[0, rows, col] = (-cre * ki - cim * kr).astype(BF16)
        q_re = jnp.concatenate(q_re, axis=0)
        q_im = jnp.concatenate(q_im, axis=0)
        m = _dg3(btre, q_re, _NT) - _dg3(btim, q_im, _NT)
        for t in range(c):
            blk = m[:, t * S5_GROUP:(t + 1) * S5_GROUP]
            if t == 0:
                blk = blk + jnp.where(r16 == c16, d_ref[0, gi], 0.0)
            tiles_ref[0, t, ch, ch] = blk.astype(BF16)
        lamre_ref[0, :, col] = pows[c][0]
        lamim_ref[0, :, col] = pows[c][1]


def s5_prep(prm, chunk):
    g, p_ = S5_GROUPS, S5_STATE
    gb = LANES // S5_GROUP
    nblk = g // gb
    kc = chunk * LANES
    grp = lambda x: x.reshape((nblk, gb) + x.shape[1:])
    args = [grp(prm['lam_re']), grp(prm['lam_im']), grp(prm['log_dt'].reshape(g, 1)),
            grp(prm['c_re']), grp(prm['c_im']),
            grp(jnp.swapaxes(prm['b_re'], 1, 2)), grp(jnp.swapaxes(prm['b_im'], 1, 2)),
            grp(prm['d'].reshape(g, S5_GROUP, 1))]
    blk = lambda x: pl.BlockSpec((1,) + x.shape[1:], lambda i: (i,) + (0,) * (x.ndim - 1))
    out_shape = [jax.ShapeDtypeStruct((nblk, chunk, LANES, LANES), BF16)] + \
        [jax.ShapeDtypeStruct((nblk, kc, gb * p_), BF16)] * 4 + \
        [jax.ShapeDtypeStruct((nblk, 1, gb * p_), F32)] * 2
    return pl.pallas_call(
        functools.partial(_s5_prep_kernel, chunk=chunk),
        grid=(nblk,),
        in_specs=[blk(a) for a in args],
        out_specs=[blk(o) for o in out_shape],
        out_shape=out_shape,
        compiler_params=pltpu.CompilerParams(dimension_semantics=("parallel",)),
        name="s5_prep",
    )(*args)


def _s5_main_kernel(u_ref, tiles_ref, bsre_ref, bsim_ref, csre_ref, csim_ref, lamre_ref, lamim_ref, x0re_ref,
                    x0im_ref, y_ref, xre_ref, xim_ref, uc_scr, tp_scr, zre_scr, zim_scr, sre_scr, sim_scr,
                    xr_scr, xi_scr, *, chunk):
    c = chunk
    nb, tt, _ = u_ref.shape
    nct = tt // c
    rows = nct * nb
    j = pl.program_id(1)

    @pl.when(j == 0)
    def _():
        xr_scr[...] = x0re_ref[...]
        xi_scr[...] = x0im_ref[...]
        for d in range(c // 2):
            tp_scr[d, :LANES, :LANES] = tiles_ref[0, 2 * d]
            tp_scr[d, :LANES, LANES:] = tiles_ref[0, 2 * d + 1]
            tp_scr[d, LANES:, LANES:] = tiles_ref[0, 2 * d]
            tp_scr[d, LANES:, :LANES] = tiles_ref[0, 2 * d - 1] if d > 0 else jnp.zeros((LANES, LANES), BF16)

    for t in range(c):
        if nct == 1:
            uc_scr[t] = u_ref[:, t, :]
        else:
            for b in range(nb):
                uc_scr[t, pl.ds(b, nct, stride=nb), :] = u_ref[b, pl.ds(t, nct, stride=c), :]
    u = jnp.concatenate([uc_scr[t].astype(BF16) for t in range(c)], axis=1)
    zre_scr[...] = jnp.dot(u, bsre_ref[0], preferred_element_type=F32)
    zim_scr[...] = jnp.dot(u, bsim_ref[0], preferred_element_type=F32)
    lr, li = lamre_ref[0], lamim_ref[0]

    def step(i, carry):
        xr, xi = carry
        rs = pl.ds(pl.multiple_of(i * nb, nb), nb)
        sre_scr[rs, :] = xr
        sim_scr[rs, :] = xi
        return (lr * xr - li * xi + zre_scr[rs, :], li * xr + lr * xi + zim_scr[rs, :])

    xr, xi = lax.fori_loop(0, nct, step, (xr_scr[...], xi_scr[...]))
    xr_scr[...] = xr
    xi_scr[...] = xi
    sre = sre_scr[...].astype(BF16)
    sim = sim_scr[...].astype(BF16)
    pair = lambda k: slice(2 * k * LANES, (2 * k + 2) * LANES)
    for k in range(c // 2):
        y = (lax.dot_general(sre, csre_ref[0, pair(k), :], _NT, preferred_element_type=F32)
             + lax.dot_general(sim, csim_ref[0, pair(k), :], _NT, preferred_element_type=F32))
        for kin in range(k + 1):
            y = y + jnp.dot(u[:, pair(kin)], tp_scr[k - kin], preferred_element_type=F32)
        y = jax.nn.gelu(y)
        uc_scr[2 * k] = y[:, :LANES]
        uc_scr[2 * k + 1] = y[:, LANES:]
    for t in range(c):
        if nct == 1:
            y_ref[:, t, :] = uc_scr[t]
        else:
            for b in range(nb):
                y_ref[b, pl.ds(t, nct, stride=c), :] = uc_scr[t, pl.ds(b, nct, stride=nb), :]

    @pl.when(j == pl.num_programs(1) - 1)
    def _():
        xre_ref[...] = xr
        xim_ref[...] = xi


def s5_mix(u, x0re, x0im, tables, chunk):
    b, t, e = u.shape
    tiles, bsre, bsim, csre, csim, lamre, lamim = tables
    nblk = e // LANES
    gp = (LANES // S5_GROUP) * S5_STATE
    nct = min(t // chunk, max(1, S5_ROWS // b))
    assert (t // chunk) % nct == 0
    tt = nct * chunk
    rows = nct * b
    kc = chunk * LANES
    blk3 = lambda x: pl.BlockSpec((1,) + x.shape[1:], lambda i, j: (i,) + (0,) * (x.ndim - 1))
    y, xre, xim = pl.pallas_call(
        functools.partial(_s5_main_kernel, chunk=chunk),
        grid=(nblk, t // tt),
        in_specs=[pl.BlockSpec((b, tt, LANES), lambda i, j: (0, j, i)), blk3(tiles), blk3(bsre), blk3(bsim),
                  blk3(csre), blk3(csim), blk3(lamre), blk3(lamim),
                  pl.BlockSpec((b, gp), lambda i, j: (0, i)), pl.BlockSpec((b, gp), lambda i, j: (0, i))],
        out_specs=[pl.BlockSpec((b, tt, LANES), lambda i, j: (0, j, i)),
                   pl.BlockSpec((b, gp), lambda i, j: (0, i)), pl.BlockSpec((b, gp), lambda i, j: (0, i))],
        out_shape=[jax.ShapeDtypeStruct((b, t, e), F32),
                   jax.ShapeDtypeStruct((b, S5_GROUPS * S5_STATE), F32),
                   jax.ShapeDtypeStruct((b, S5_GROUPS * S5_STATE), F32)],
        scratch_shapes=[pltpu.VMEM((chunk, rows, LANES), F32), pltpu.VMEM((chunk // 2, 2 * LANES, 2 * LANES), BF16)]
        + [pltpu.VMEM((rows, gp), F32)] * 4 + [pltpu.VMEM((b, gp), F32)] * 2,
        compiler_params=pltpu.CompilerParams(dimension_semantics=("parallel", "arbitrary"),
                                             vmem_limit_bytes=VMEM_LIMIT),
        name="s5_main",
    )(u, tiles, bsre, bsim, csre, csim, lamre, lamim, x0re.reshape(b, -1), x0im.reshape(b, -1))
    return y, xre.reshape(b, S5_GROUPS, S5_STATE), xim.reshape(b, S5_GROUPS, S5_STATE)


def _trunk(x, st_shift, st_wkv, st_gla, st_re, st_im, w):
    b, t, d = x.shape
    m = b * t
    x2 = x.reshape(m, d)

    p_r, p_g, p_xa = norm_proj(x2, w['norm_mix_pre'][0], [w['w_in_rwkv'], w['w_in_gla'], w['w_in_xa']],
                               [F32, F32, F32])
    o_r, wkv = rwkv_mix(p_r.reshape(b, t, -1), st_shift[0], st_wkv[0], w['rwkv'])
    o_g, gla = gla_mix(p_g.reshape(b, t, -1), p_xa.reshape(b, t, -1), st_gla[0], w['gla_wa2'], w['gla_ba'],
                       w['gla_norm'])
    shift = p_r.reshape(b, t, -1)[:, -1]
    x2 = proj_post([o_r.reshape(m, -1), o_g.reshape(m, -1)], [w['w_out_rwkv'], w['w_out_gla']],
                   w['norm_mix_post'][0], x2)
    x2 = ffn(x2, w['norm_ffn_pre'][0], w['w_ff_up'][0], w['w_ff_down'][0], w['norm_ffn_post'][0])

    (u,) = norm_proj(x2, w['norm_mix_pre'][1], [w['s5_w_in']], [F32])
    chunk = min(S5_CHUNK, t)
    y, s_re, s_im = s5_mix(u.reshape(b, t, d), st_re[0], st_im[0], w['s5_tables'][chunk], chunk)
    x2 = proj_post([y.reshape(m, d)], [w['s5_w_out']], w['norm_mix_post'][1], x2, glu=True)
    x2 = ffn(x2, w['norm_ffn_pre'][1], w['w_ff_up'][1], w['w_ff_down'][1], w['norm_ffn_post'][1])
    return x2.reshape(b, t, d), shift[None], wkv[None], gla[None], s_re[None], s_im[None]


def kernel(x_prompt, x_sample, state_rwkv_shift, state_rwkv_wkv, state_gla, state_s5_re, state_s5_im,
           norm_mix_pre, norm_mix_post, norm_ffn_pre, norm_ffn_post,
           w_mix_in, w_mix_out, rwkv_mu, rwkv_w0, rwkv_w2, rwkv_a0, rwkv_a2, rwkv_g2,
           rwkv_k_k, rwkv_k_a, rwkv_r_k, rwkv_gn_g, rwkv_gn_b, gla_wa2, gla_ba, gla_norm,
           s5_w_in, s5_lam_re, s5_lam_im, s5_log_dt, s5_b_re, s5_b_im, s5_c_re, s5_c_im, s5_d, s5_w_out,
           w_ff_up, w_ff_down):
    assert norm_mix_pre.shape[0] == 2, "two layers: one RWKV-7/GLA layer, one S5 layer"
    bf = lambda a: a.astype(BF16)
    w_in = w_mix_in[0]
    gla0 = RWKV_PROJ
    xa0 = gla0 + 2 * GLA_K + GLA_V
    w_in_gla = jnp.concatenate([w_in[:, gla0:xa0], w_in[:, xa0 + GLA_LORA:]], axis=1)
    w_in_xa = jnp.pad(w_in[:, xa0:xa0 + GLA_LORA], ((0, 0), (0, LANES - GLA_LORA)))
    s5_prm = dict(lam_re=s5_lam_re[0], lam_im=s5_lam_im[0], log_dt=s5_log_dt[0], b_re=s5_b_re[0],
                  b_im=s5_b_im[0], c_re=s5_c_re[0], c_im=s5_c_im[0], d=s5_d[0])
    tables = {}
    for chunk in sorted({min(S5_CHUNK, x_prompt.shape[1]), min(S5_CHUNK, x_sample.shape[1])}):
        tables[chunk] = s5_prep(s5_prm, chunk)
    w = dict(
        norm_mix_pre=norm_mix_pre, norm_mix_post=norm_mix_post, norm_ffn_pre=norm_ffn_pre,
        norm_ffn_post=norm_ffn_post,
        w_in_rwkv=bf(w_in[:, :RWKV_PROJ]), w_in_gla=bf(w_in_gla), w_in_xa=bf(w_in_xa),
        w_out_rwkv=bf(w_mix_out[0, :RWKV_W]), w_out_gla=bf(w_mix_out[0, RWKV_W:]),
        rwkv=dict(mu=rwkv_mu[0], w0=rwkv_w0[0], w2=rwkv_w2[0], a0=rwkv_a0[0], a2=rwkv_a2[0], g2=rwkv_g2[0],
                  k_k=rwkv_k_k[0], k_a=rwkv_k_a[0], r_k=rwkv_r_k[0], gn_g=rwkv_gn_g[0], gn_b=rwkv_gn_b[0]),
        gla_wa2=bf(jnp.pad(gla_wa2[0], ((0, LANES - GLA_LORA), (0, 0)))), gla_ba=gla_ba[0], gla_norm=gla_norm[0],
        s5_w_in=bf(s5_w_in[0]), s5_w_out=bf(s5_w_out[0]), s5_tables=tables,
        w_ff_up=bf(w_ff_up), w_ff_down=bf(w_ff_down),
    )
    bp = x_prompt.shape[0]
    zeros = lambda s: jnp.zeros((s.shape[0], bp) + s.shape[2:], x_prompt.dtype)
    y_p, sh_p, wkv_p, gla_p, re_p, im_p = _trunk(x_prompt, zeros(state_rwkv_shift), zeros(state_rwkv_wkv),
                                                 zeros(state_gla), zeros(state_s5_re), zeros(state_s5_im), w)
    y_s, sh_s, wkv_s, gla_s, re_s, im_s = _trunk(x_sample, state_rwkv_shift, state_rwkv_wkv, state_gla,
                                                 state_s5_re, state_s5_im, w)
    return (y_p, y_s, sh_p, sh_s, wkv_p, wkv_s, gla_p, gla_s, re_p, re_s, im_p, im_s)
```

```python
import functools
import math

import jax
import jax.numpy as jnp
from jax import lax
from jax.experimental import pallas as pl
from jax.experimental.pallas import tpu as pltpu

F32 = jnp.float32
BF16 = jnp.bfloat16

D_MODEL = 1024
D_FF = 4096
NORM_EPS = 1e-6
RWKV_HEADS = 8
RWKV_HEAD = 64
RWKV_W = RWKV_HEADS * RWKV_HEAD
RWKV_PROJ = 1792
RWKV_CHUNK = 64
RWKV_ROWS = 64
RWKV_STEP_ROWS = 256
GN_EPS = 64e-5
_P_GRAM = 1
_P_INV = 1
_P_SOLVE = 1
_P_MIX = 1
_P_STATE = 1
GLA_HEADS = 4
GLA_DK = 64
GLA_DV = 128
GLA_K = GLA_HEADS * GLA_DK
GLA_V = GLA_HEADS * GLA_DV
GLA_LORA = 16
GLA_TAU = 16.0
GLA_CHUNK = 32
GLA_MAIN = 2 * GLA_K + 2 * GLA_V
GLA_TILE = 256
GLA_SMALL_ROWS = 64
S5_GROUPS = 64
S5_GROUP = 16
S5_STATE = 64
S5_CHUNK = 16
S5_ROWS = 512
LANES = 128
VMEM_LIMIT = 48 * 1024 * 1024

_NN = (((1,), (0,)), ((), ()))
_NT = (((1,), (1,)), ((), ()))
_TN = (((0,), (0,)), ((), ()))


def _dg(a, b, dn=_NN):
    return lax.dot_general(a.astype(BF16), b.astype(BF16), dn, preferred_element_type=F32)


def _split2(x):
    hi = x.astype(BF16)
    lo = (x - hi.astype(F32)).astype(BF16)
    return hi, lo


def _split3(x):
    hi = x.astype(BF16)
    r1 = x - hi.astype(F32)
    mid = r1.astype(BF16)
    lo = (r1 - mid.astype(F32)).astype(BF16)
    return hi, mid, lo


def _dg3(a, b, dn=_NN):
    ah, al = _split2(a)
    bh, bl = _split2(b)
    d = lambda x, y: lax.dot_general(x, y, dn, preferred_element_type=F32)
    return d(ah, bh) + (d(ah, bl) + d(al, bh))


def _mp(a, b, dn, passes):
    return _dg(a, b, dn) if passes == 1 else _dg3(a, b, dn)


def _dg_sel(sel, x, dn=_NN):
    d = lambda y: lax.dot_general(sel, y, dn, preferred_element_type=F32)
    h, m, l = _split3(x)
    return d(h) + (d(m) + d(l))


def _dg_selr(x, sel, dn=_NN):
    d = lambda y: lax.dot_general(y, sel, dn, preferred_element_type=F32)
    h, m, l = _split3(x)
    return d(h) + (d(m) + d(l))


def _rms(x, g):
    return x * lax.rsqrt(jnp.mean(x * x, axis=-1, keepdims=True) + NORM_EPS) * g


def _row_tile(m, want):
    t = min(m, want)
    assert m % t == 0, (m, t)
    return t


def _full(shape):
    nd = len(shape)
    return pl.BlockSpec(shape, lambda *_: (0,) * nd)


def _token_blocks(b, t, nb, tt):
    if tt == t:
        return (b // nb, nb * t), (1, nb * t), (lambda i, j: (i, 0, 0))
    return (b, t), (nb, tt), (lambda i, j: (i, j, 0))


def _norm_proj_kernel(x_ref, g_ref, *refs):
    n = len(refs) // 2
    h = _rms(x_ref[...], g_ref[...]).astype(BF16)
    for w_ref, o_ref in zip(refs[:n], refs[n:]):
        o_ref[...] = jnp.dot(h, w_ref[...], preferred_element_type=F32).astype(o_ref.dtype)


def norm_proj(x, g, ws, out_dtypes, tm=512):
    m, d = x.shape
    tm = _row_tile(m, tm)
    return pl.pallas_call(
        _norm_proj_kernel,
        grid=(m // tm,),
        in_specs=[pl.BlockSpec((tm, d), lambda i: (i, 0)), _full((1, d))]
        + [_full(w.shape) for w in ws],
        out_specs=[pl.BlockSpec((tm, w.shape[1]), lambda i: (i, 0)) for w in ws],
        out_shape=[jax.ShapeDtypeStruct((m, w.shape[1]), dt) for w, dt in zip(ws, out_dtypes)],
        compiler_params=pltpu.CompilerParams(dimension_semantics=("parallel",), vmem_limit_bytes=VMEM_LIMIT),
        name="norm_proj",
    )(x, g.reshape(1, d), *ws)


def _proj_post_kernel(*refs, n_in, glu):
    a_refs = refs[:n_in]
    w_refs = refs[n_in:2 * n_in]
    g_ref, x_ref, o_ref = refs[2 * n_in:]
    m = None
    for a_ref, w_ref in zip(a_refs, w_refs):
        t = jnp.dot(a_ref[...].astype(BF16), w_ref[...], preferred_element_type=F32)
        m = t if m is None else m + t
    if glu:
        m = m[:, :D_MODEL] * jax.nn.sigmoid(m[:, D_MODEL:])
    o_ref[...] = x_ref[...] + _rms(m, g_ref[...])


def proj_post(a_list, w_list, g, x, glu=False, tm=512):
    m, d = x.shape
    tm = _row_tile(m, tm)
    n_in = len(a_list)
    return pl.pallas_call(
        functools.partial(_proj_post_kernel, n_in=n_in, glu=glu),
        grid=(m // tm,),
        in_specs=[pl.BlockSpec((tm, a.shape[1]), lambda i: (i, 0)) for a in a_list]
        + [_full(w.shape) for w in w_list]
        + [_full((1, d)), pl.BlockSpec((tm, d), lambda i: (i, 0))],
        out_specs=pl.BlockSpec((tm, d), lambda i: (i, 0)),
        out_shape=jax.ShapeDtypeStruct((m, d), F32),
        compiler_params=pltpu.CompilerParams(dimension_semantics=("parallel",), vmem_limit_bytes=VMEM_LIMIT),
        name="proj_post",
    )(*a_list, *w_list, g.reshape(1, d), x)


def _ffn_kernel(x_ref, gpre_ref, wup_ref, wdn_ref, gpost_ref, o_ref, h_scr, acc_scr):
    j = pl.program_id(1)

    @pl.when(j == 0)
    def _():
        h_scr[...] = _rms(x_ref[...], gpre_ref[...]).astype(BF16)
        acc_scr[...] = jnp.zeros_like(acc_scr)

    u = jnp.dot(h_scr[...], wup_ref[...], preferred_element_type=F32)
    u = jnp.square(jnp.maximum(u, 0.0)).astype(BF16)
    acc_scr[...] += jnp.dot(u, wdn_ref[...], preferred_element_type=F32)

    @pl.when(j == pl.num_programs(1) - 1)
    def _():
        o_ref[...] = x_ref[...] + _rms(acc_scr[...], gpost_ref[...])


def ffn(x, g_pre, w_up, w_down, g_post, tm=1024, tf=1024):
    m, d = x.shape
    f = w_up.shape[1]
    tm = _row_tile(m, tm)
    return pl.pallas_call(
        _ffn_kernel,
        grid=(m // tm, f // tf),
        in_specs=[
            pl.BlockSpec((tm, d), lambda i, j: (i, 0)),
            _full((1, d)),
            pl.BlockSpec((d, tf), lambda i, j: (0, j)),
            pl.BlockSpec((tf, d), lambda i, j: (j, 0)),
            _full((1, d)),
        ],
        out_specs=pl.BlockSpec((tm, d), lambda i, j: (i, 0)),
        out_shape=jax.ShapeDtypeStruct((m, d), F32),
        scratch_shapes=[pltpu.VMEM((tm, d), BF16), pltpu.VMEM((tm, d), F32)],
        compiler_params=pltpu.CompilerParams(dimension_semantics=("parallel", "arbitrary"),
                                             vmem_limit_bytes=VMEM_LIMIT),
        name="ffn",
    )(x, g_pre.reshape(1, d), w_up, w_down, g_post.reshape(1, d))


def _head_sum(x, ones_bd):
    hi, lo = _split2(x)
    d = lambda y: jnp.dot(y, ones_bd, preferred_element_type=F32)
    return d(hi) + d(lo)


def _rwkv_kernel(p_ref, sh0_ref, s0_ref, mu_ref, w0_ref, w2_ref, a0_ref, a2_ref, g2_ref, kk_ref, ka_ref,
                 rk_ref, gng_ref, gnb_ref, ones_ref, o_ref, sout_ref, state_scr, carry_scr, y_scr, *, chunk):
    c = chunk
    nb = s0_ref.shape[0]
    rows = nb * c
    shift = int(math.log2(c))
    n = pl.program_id(1)

    @pl.when(n == 0)
    def _():
        carry_scr[...] = sh0_ref[...]
        state_scr[...] = s0_ref[...]

    p3 = p_ref[...]
    p = p3.astype(F32).reshape(rows, RWKV_PROJ)
    t_idx = jnp.bitwise_and(lax.broadcasted_iota(jnp.int32, (rows, 1), 0), c - 1)
    carry = jnp.broadcast_to(carry_scr[...], (nb, c, RWKV_PROJ)).reshape(rows, RWKV_PROJ)
    prev = jnp.where(t_idx == 0, carry, pltpu.roll(p, 1, 0))
    if p3.shape[0] == nb:
        carry_scr[...] = p3[:, c - 1:c, :].astype(F32)
    z = p + (prev - p) * mu_ref[...]

    w_ = RWKV_W
    r = z[:, 0:w_]
    k = z[:, w_:2 * w_]
    v = z[:, 2 * w_:3 * w_]
    xw = z[:, 3 * w_:3 * w_ + 64]
    xa = z[:, 3 * w_ + 64:3 * w_ + 128]
    xg = z[:, 3 * w_ + 128:3 * w_ + 256]

    wlog = -jax.nn.softplus(-(w0_ref[...] + _dg(jnp.tanh(xw), w2_ref[...]))) - 0.5
    logd = -jnp.exp(wlog)
    a = jax.nn.sigmoid(a0_ref[...] + _dg(xa, a2_ref[...]))
    g = _dg(jax.nn.sigmoid(xg), g2_ref[...])

    ones_bd = ones_ref[...]
    kk = k * kk_ref[...]
    kk = kk * lax.rsqrt(jnp.maximum(_head_sum(kk * kk, ones_bd), 1e-24))
    k2 = k * (1.0 + (a - 1.0) * ka_ref[...])
    bb = kk * a

    def masks(m):
        ri = lax.broadcasted_iota(jnp.int32, (m, m), 0)
        ci = lax.broadcasted_iota(jnp.int32, (m, m), 1)
        same = lax.shift_right_logical(ri, shift) == lax.shift_right_logical(ci, shift)
        return same, jnp.logical_and(same, ri > ci), jnp.logical_and(same, ri >= ci), (ri == ci).astype(F32)

    same_all, _, lower_eq_all, _ = masks(rows)
    cum = _dg_sel(lower_eq_all.astype(BF16), logd)
    tot = _dg_sel(same_all.astype(BF16), logd)
    g_in = jnp.exp(cum)
    g_ex = jnp.exp(cum - logd)
    g_inv = jnp.exp(-cum)
    g_hat = jnp.exp(tot - cum)
    g_all = jnp.exp(tot)
    rt = r * g_in
    kt = kk * g_ex
    ktil = k2 * g_inv
    btil = bb * g_inv
    khat = k2 * g_hat
    bhat = bb * g_hat

    gr = min(rows, RWKV_ROWS)
    spg = gr // c
    _, lower, lower_eq, eye = masks(gr)
    probs = [(g_, h) for g_ in range(rows // gr) for h in range(RWKV_HEADS)]
    hsl = lambda h: slice(h * RWKV_HEAD, (h + 1) * RWKV_HEAD)
    gsl = lambda g_: slice(g_ * gr, (g_ + 1) * gr)
    pick = lambda x, g_, h: x[gsl(g_), hsl(h)]
    seq_rows = lambda g_, i: slice(g_ * gr + i * c, g_ * gr + (i + 1) * c)
    kr = [jnp.concatenate([pick(kt, *q), pick(rt, *q)], axis=0) for q in probs]
    gb = [_mp(kr[j], pick(btil, *q), _NT, _P_GRAM) for j, q in enumerate(probs)]
    gk = [_mp(kr[j], pick(ktil, *q), _NT, _P_GRAM) for j, q in enumerate(probs)]
    a_b = [jnp.where(lower, x[:gr], 0.0) for x in gb]
    a_rb = [jnp.where(lower_eq, x[gr:], 0.0) for x in gb]
    a_kk = [jnp.concatenate([jnp.where(lower, x[:gr], 0.0), jnp.where(lower_eq, x[gr:], 0.0)], axis=0)
            for x in gk]
    ks = [[_mp(jnp.concatenate([kt[seq_rows(g_, i), hsl(h)], rt[seq_rows(g_, i), hsl(h)]], axis=0),
               state_scr[g_ * spg + i, h], _NT, _P_STATE) for i in range(spg)] for g_, h in probs]
    cat = lambda xs: jnp.concatenate(xs, axis=0) if len(xs) > 1 else xs[0]
    x0 = [cat([x[:c] for x in ks[j]]) for j in range(len(probs))]
    r0 = [cat([x[c:] for x in ks[j]]) for j in range(len(probs))]
    av = [_mp(a_kk[j], pick(v, *q), _NN, _P_MIX) for j, q in enumerate(probs)]
    tinv = [eye - x for x in a_b]
    pw = [_mp(x, x, _NN, _P_INV) for x in a_b]
    span = 2
    while span < c:
        span *= 2
        if span < c:
            pr = [_mp(jnp.concatenate([t_, p_], axis=0), p_, _NN, _P_INV) for t_, p_ in zip(tinv, pw)]
            tinv = [t_ + x[:gr] for t_, x in zip(tinv, pr)]
            pw = [x[gr:] for x in pr]
        else:
            tinv = [t_ + _mp(t_, p_, _NN, _P_INV) for t_, p_ in zip(tinv, pw)]
    u = [_mp(tinv[j], x0[j] + av[j][:gr], _NN, _P_SOLVE) for j in range(len(probs))]
    for j, (g_, h) in enumerate(probs):
        y_scr[gsl(g_), hsl(h)] = r0[j] + av[j][gr:] - _mp(a_rb[j], u[j], _NN, _P_MIX)
    for j, (g_, h) in enumerate(probs):
        for i in range(spg):
            rs = seq_rows(g_, i)
            vu = jnp.concatenate([v[rs, hsl(h)], -u[j][i * c:(i + 1) * c]], axis=0)
            kb = jnp.concatenate([khat[rs, hsl(h)], bhat[rs, hsl(h)]], axis=0)
            s_i = g_ * spg + i
            state_scr[s_i, h] = (state_scr[s_i, h] * g_all[rs.start:rs.start + 1, hsl(h)]
                                 + _mp(vu, kb, _TN, _P_STATE))

    y = y_scr[...]
    inv_n = 1.0 / RWKV_HEAD
    mean = _head_sum(y, ones_bd) * inv_n
    yc = y - mean
    var = _head_sum(yc * yc, ones_bd) * inv_n
    yn = yc * lax.rsqrt(var + GN_EPS) * gng_ref[...] + gnb_ref[...]
    bonus = _head_sum(r * k2 * rk_ref[...], ones_bd) * v
    o_ref[...] = ((yn + bonus) * g).astype(o_ref.dtype).reshape(o_ref.shape)

    @pl.when(n == pl.num_programs(1) - 1)
    def _():
        sout_ref[...] = state_scr[...]


def rwkv_mix(p, shift0, s0, prm):
    b, t, _ = p.shape
    c = min(RWKV_CHUNK, t)
    assert t % c == 0
    nb = min(b, RWKV_STEP_ROWS // c)
    assert b % nb == 0
    w_ = RWKV_W
    row = lambda x: x.reshape(1, -1)
    ones_bd = jnp.kron(jnp.eye(RWKV_HEADS, dtype=F32), jnp.ones((RWKV_HEAD, RWKV_HEAD), F32)).astype(BF16)
    consts = [row(prm['mu']), row(prm['w0']), prm['w2'].astype(BF16), row(prm['a0']), prm['a2'].astype(BF16),
              prm['g2'].astype(BF16), row(prm['k_k']), row(prm['k_a']), row(prm['r_k']), row(prm['gn_g']),
              row(prm['gn_b']), ones_bd]
    tok_shape, tok_block, tok_index = _token_blocks(b, t, nb, c)
    o, s_out = pl.pallas_call(
        functools.partial(_rwkv_kernel, chunk=c),
        grid=(b // nb, t // c),
        in_specs=[
            pl.BlockSpec(tok_block + (RWKV_PROJ,), tok_index),
            pl.BlockSpec((nb, 1, RWKV_PROJ), lambda i, j: (i, 0, 0)),
            pl.BlockSpec((nb, RWKV_HEADS, RWKV_HEAD, RWKV_HEAD), lambda i, j: (i, 0, 0, 0)),
        ] + [_full(x.shape) for x in consts],
        out_specs=[
            pl.BlockSpec(tok_block + (w_,), tok_index),
            pl.BlockSpec((nb, RWKV_HEADS, RWKV_HEAD, RWKV_HEAD), lambda i, j: (i, 0, 0, 0)),
        ],
        out_shape=[jax.ShapeDtypeStruct(tok_shape + (w_,), BF16),
                   jax.ShapeDtypeStruct((b, RWKV_HEADS, RWKV_HEAD, RWKV_HEAD), F32)],
        scratch_shapes=[pltpu.VMEM((nb, RWKV_HEADS, RWKV_HEAD, RWKV_HEAD), F32),
                        pltpu.VMEM((nb, 1, RWKV_PROJ), F32),
                        pltpu.VMEM((nb * c, w_), F32)],
        compiler_params=pltpu.CompilerParams(dimension_semantics=("parallel", "arbitrary")),
        name="rwkv_mix",
    )(p.reshape(tok_shape + (RWKV_PROJ,)), shift0.reshape(b, 1, RWKV_PROJ), s0, *consts)
    return o.reshape(b, t, w_), s_out


def _gla_kernel(pg_ref, xa_ref, s0_ref, wa2_ref, ba_ref, gn_ref, o_ref, sout_ref, state_scr, o_scr, *, chunk):
    c = chunk
    nb = s0_ref.shape[0]
    rows = pg_ref.shape[0] * pg_ref.shape[1]
    tt = rows // nb
    nc = tt // c
    n = pl.program_id(1)
    heads = range(GLA_HEADS)
    seqs = range(nb)
    ident = lambda m: (lax.broadcasted_iota(jnp.int32, (m, m), 0)
                       == lax.broadcasted_iota(jnp.int32, (m, m), 1)).astype(BF16)

    @pl.when(n == 0)
    def _():
        eye_k = ident(GLA_DK)
        for i in seqs:
            for h in heads:
                state_scr[i, h] = _dg_selr(s0_ref[i, h], eye_k, _TN)

    pg = pg_ref[...].astype(F32).reshape(rows, GLA_MAIN)
    q = pg[:, 0:GLA_K] * (GLA_DK ** -0.5)
    k = pg[:, GLA_K:2 * GLA_K]
    v = pg[:, 2 * GLA_K:2 * GLA_K + GLA_V]
    gz = pg[:, 2 * GLA_K + GLA_V:]
    xa = xa_ref[...].astype(F32).reshape(rows, LANES)
    gk = jax.nn.log_sigmoid(_dg(xa, wa2_ref[...]) + ba_ref[...]) * (1.0 / GLA_TAU)

    shift = int(math.log2(c))
    ri = lax.broadcasted_iota(jnp.int32, (rows, rows), 0)
    ci = lax.broadcasted_iota(jnp.int32, (rows, rows), 1)
    same = lax.shift_right_logical(ri, shift) == lax.shift_right_logical(ci, shift)
    causal = jnp.logical_and(same, ri >= ci)
    bc = _dg_sel(causal.astype(BF16), gk)
    bl = _dg_sel(same.astype(BF16), gk)
    qt = q * jnp.exp(bc)
    kt = k * jnp.exp(-bc)
    ks = k * jnp.exp(bl - bc)
    ebl = jnp.exp(bl)

    ksl = [slice(h * GLA_DK, (h + 1) * GLA_DK) for h in heads]
    vsl = [slice(h * GLA_DV, (h + 1) * GLA_DV) for h in heads]
    chunks = [(i, slice(i * tt + j * c, i * tt + (j + 1) * c)) for i in seqs for j in range(nc)]
    att = [jnp.where(causal, _dg(qt[:, ksl[h]], kt[:, ksl[h]], _NT), 0.0) for h in heads]
    o_intra = [_dg(att[h], v[:, vsl[h]]) for h in heads]
    kv = [[_dg(v[rs, vsl[h]], ks[rs, ksl[h]], _TN) for _, rs in chunks] for h in heads]
    st = [[state_scr[i, h] for i in seqs] for h in heads]
    for ci_, (i, rs) in enumerate(chunks):
        for h in heads:
            o_scr[rs, vsl[h]] = o_intra[h][rs] + _dg(qt[rs, ksl[h]], st[h][i], _NT)
            st[h][i] = st[h][i] * ebl[rs.start:rs.start + 1, ksl[h]] + kv[h][ci_]
    for h in heads:
        for i in seqs:
            state_scr[i, h] = st[h][i]

    for h in heads:
        o_h = o_scr[:, vsl[h]]
        o_h = o_h * lax.rsqrt(jnp.mean(o_h * o_h, axis=-1, keepdims=True) + NORM_EPS) * gn_ref[...]
        o_scr[:, vsl[h]] = o_h * jax.nn.silu(gz[:, vsl[h]])
    o_ref[...] = o_scr[...].astype(o_ref.dtype).reshape(o_ref.shape)

    @pl.when(n == pl.num_programs(1) - 1)
    def _():
        eye_v = ident(GLA_DV)
        for i in seqs:
            for h in heads:
                sout_ref[i, h] = _dg_selr(state_scr[i, h], eye_v, _TN)


def gla_mix(pg, pxa, s0, wa2_pad, ba, g_norm):
    b, t, _ = pg.shape
    tt = min(GLA_TILE, t)
    c = min(GLA_CHUNK, t)
    nb = max(1, min(b, GLA_SMALL_ROWS // tt))
    assert t % tt == 0 and tt % c == 0 and b % nb == 0
    tok_shape, tok_block, tok_index = _token_blocks(b, t, nb, tt)
    o, s_out = pl.pallas_call(
        functools.partial(_gla_kernel, chunk=c),
        grid=(b // nb, t // tt),
        in_specs=[
            pl.BlockSpec(tok_block + (GLA_MAIN,), tok_index),
            pl.BlockSpec(tok_block + (LANES,), tok_index),
            pl.BlockSpec((nb, GLA_HEADS, GLA_DK, GLA_DV), lambda i, j: (i, 0, 0, 0)),
            _full(wa2_pad.shape), _full((1, GLA_K)), _full((1, GLA_DV)),
        ],
        out_specs=[
            pl.BlockSpec(tok_block + (GLA_V,), tok_index),
            pl.BlockSpec((nb, GLA_HEADS, GLA_DK, GLA_DV), lambda i, j: (i, 0, 0, 0)),
        ],
        out_shape=[jax.ShapeDtypeStruct(tok_shape + (GLA_V,), BF16),
                   jax.ShapeDtypeStruct((b, GLA_HEADS, GLA_DK, GLA_DV), F32)],
        scratch_shapes=[pltpu.VMEM((nb, GLA_HEADS, GLA_DV, GLA_DK), F32), pltpu.VMEM((nb * tt, GLA_V), F32)],
        compiler_params=pltpu.CompilerParams(dimension_semantics=("parallel", "arbitrary")),
        name="gla_mix",
    )(pg.reshape(tok_shape + (GLA_MAIN,)), pxa.reshape(tok_shape + (LANES,)), s0, wa2_pad,
      ba.reshape(1, GLA_K), g_norm.reshape(1, GLA_DV))
    return o.reshape(b, t, GLA_V), s_out


def _cmul(ar, ai, br, bi):
    return ar * br - ai * bi, ar * bi + ai * br


def _s5_prep_kernel(lre_ref, lim_ref, ldt_ref, cre_ref, cim_ref, btre_ref, btim_ref, d_ref,
                    tiles_ref, bsre_ref, bsim_ref, csre_ref, csim_ref, lamre_ref, lamim_ref, *, chunk):
    c = chunk
    p_ = S5_STATE
    gb = LANES // S5_GROUP
    tiles_ref[...] = jnp.zeros_like(tiles_ref)
    bsre_ref[...] = jnp.zeros_like(bsre_ref)
    bsim_ref[...] = jnp.zeros_like(bsim_ref)
    csre_ref[...] = jnp.zeros_like(csre_ref)
    csim_ref[...] = jnp.zeros_like(csim_ref)
    r16 = lax.broadcasted_iota(jnp.int32, (S5_GROUP, S5_GROUP), 0)
    c16 = lax.broadcasted_iota(jnp.int32, (S5_GROUP, S5_GROUP), 1)
    for gi in range(gb):
        lr = lre_ref[0, gi:gi + 1, :]
        li = lim_ref[0, gi:gi + 1, :]
        dt = jnp.exp(ldt_ref[0, gi:gi + 1, :])
        mag = jnp.exp(lr * dt)
        ang = li * dt
        abr, abi = mag * jnp.cos(ang), mag * jnp.sin(ang)
        den = lr * lr + li * li
        fr = ((abr - 1.0) * lr + abi * li) / den
        fi = (abi * lr - (abr - 1.0) * li) / den
        pows = [(jnp.ones_like(abr), jnp.zeros_like(abr))]
        for _ in range(c):
            pows.append(_cmul(pows[-1][0], pows[-1][1], abr, abi))
        cre, cim = cre_ref[0, gi], cim_ref[0, gi]
        btre, btim = btre_ref[0, gi], btim_ref[0, gi]
        q_re, q_im = [], []
        col = slice(gi * p_, (gi + 1) * p_)
        ch = slice(gi * S5_GROUP, (gi + 1) * S5_GROUP)
        for t in range(c):
            er, ei = _cmul(pows[t][0], pows[t][1], fr, fi)
            qr, qi = _cmul(cre, cim, er, ei)
            q_re.append(qr)
            q_im.append(qi)
            er, ei = _cmul(pows[c - 1 - t][0], pows[c - 1 - t][1], fr, fi)
            zr, zi = _cmul(btre, btim, er, ei)
            rows = slice(t * LANES + gi * S5_GROUP, t * LANES + (gi + 1) * S5_GROUP)
            bsre_ref[0, rows, col] = zr.astype(BF16)
            bsim_ref[0, rows, col] = zi.astype(BF16)
            kr, ki = pows[t + 1]
            csre_ref[0, rows, col] = (cre * kr - cim * ki).astype(BF16)
            csim_ref[0, rows, col] = (-cre * ki - cim * kr).astype(BF16)
        q_re = jnp.concatenate(q_re, axis=0)
        q_im = jnp.concatenate(q_im, axis=0)
        m = _dg3(btre, q_re, _NT) - _dg3(btim, q_im, _NT)
        for t in range(c):
            blk = m[:, t * S5_GROUP:(t + 1) * S5_GROUP]
            if t == 0:
                blk = blk + jnp.where(r16 == c16, d_ref[0, gi], 0.0)
            tiles_ref[0, t, ch, ch] = blk.astype(BF16)
        for t in range(c + 1):
            lamre_ref[0, t, :, col] = pows[t][0]
            lamim_ref[0, t, :, col] = pows[t][1]


def s5_prep(prm, chunk):
    g, p_ = S5_GROUPS, S5_STATE
    gb = LANES // S5_GROUP
    nblk = g // gb
    kc = chunk * LANES
    grp = lambda x: x.reshape((nblk, gb) + x.shape[1:])
    args = [grp(prm['lam_re']), grp(prm['lam_im']), grp(prm['log_dt'].reshape(g, 1)),
            grp(prm['c_re']), grp(prm['c_im']),
            grp(jnp.swapaxes(prm['b_re'], 1, 2)), grp(jnp.swapaxes(prm['b_im'], 1, 2)),
            grp(prm['d'].reshape(g, S5_GROUP, 1))]
    blk = lambda x: pl.BlockSpec((1,) + x.shape[1:], lambda i: (i,) + (0,) * (x.ndim - 1))
    out_shape = [jax.ShapeDtypeStruct((nblk, chunk, LANES, LANES), BF16)] + \
        [jax.ShapeDtypeStruct((nblk, kc, gb * p_), BF16)] * 4 + \
        [jax.ShapeDtypeStruct((nblk, chunk + 1, 1, gb * p_), F32)] * 2
    return pl.pallas_call(
        functools.partial(_s5_prep_kernel, chunk=chunk),
        grid=(nblk,),
        in_specs=[blk(a) for a in args],
        out_specs=[blk(o) for o in out_shape],
        out_shape=out_shape,
        compiler_params=pltpu.CompilerParams(dimension_semantics=("parallel",)),
        name="s5_prep",
    )(*args)


def _s5_main_kernel(u_ref, tiles_ref, bsre_ref, bsim_ref, csre_ref, csim_ref, lamre_ref, lamim_ref, x0re_ref,
                    x0im_ref, y_ref, xre_ref, xim_ref, uc_scr, tp_scr, zre_scr, zim_scr, sre_scr, sim_scr,
                    xr_scr, xi_scr, *, chunk):
    c = chunk
    nb, tt, _ = u_ref.shape
    nct = tt // c
    rows = nct * nb
    j = pl.program_id(1)

    @pl.when(j == 0)
    def _():
        xr_scr[...] = x0re_ref[...]
        xi_scr[...] = x0im_ref[...]
        for d in range(c // 2):
            tp_scr[d, :LANES, :LANES] = tiles_ref[0, 2 * d]
            tp_scr[d, :LANES, LANES:] = tiles_ref[0, 2 * d + 1]
            tp_scr[d, LANES:, LANES:] = tiles_ref[0, 2 * d]
            tp_scr[d, LANES:, :LANES] = tiles_ref[0, 2 * d - 1] if d > 0 else jnp.zeros((LANES, LANES), BF16)

    for t in range(c):
        if nct == 1:
            uc_scr[t] = u_ref[:, t, :]
        else:
            for b in range(nb):
                uc_scr[t, pl.ds(b, nct, stride=nb), :] = u_ref[b, pl.ds(t, nct, stride=c), :]
    u = jnp.concatenate([uc_scr[t].astype(BF16) for t in range(c)], axis=1)
    zre_scr[...] = jnp.dot(u, bsre_ref[0], preferred_element_type=F32)
    zim_scr[...] = jnp.dot(u, bsim_ref[0], preferred_element_type=F32)
    lr, li = lamre_ref[0, 0], lamim_ref[0, 0]

    def step(i, carry):
        xr, xi = carry
        rs = pl.ds(pl.multiple_of(i * nb, nb), nb)
        sre_scr[rs, :] = xr
        sim_scr[rs, :] = xi
        return (lr * xr - li * xi + zre_scr[rs, :], li * xr + lr * xi + zim_scr[rs, :])

    xr, xi = lax.fori_loop(0, nct, step, (xr_scr[...], xi_scr[...]))
    xr_scr[...] = xr
    xi_scr[...] = xi
    sre = sre_scr[...].astype(BF16)
    sim = sim_scr[...].astype(BF16)
    pair = lambda k: slice(2 * k * LANES, (2 * k + 2) * LANES)
    for k in range(c // 2):
        y = (lax.dot_general(sre, csre_ref[0, pair(k), :], _NT, preferred_element_type=F32)
             + lax.dot_general(sim, csim_ref[0, pair(k), :], _NT, preferred_element_type=F32))
        for kin in range(k + 1):
            y = y + jnp.dot(u[:, pair(kin)], tp_scr[k - kin], preferred_element_type=F32)
        y = jax.nn.gelu(y)
        uc_scr[2 * k] = y[:, :LANES]
        uc_scr[2 * k + 1] = y[:, LANES:]
    for t in range(c):
        if nct == 1:
            y_ref[:, t, :] = uc_scr[t]
        else:
            for b in range(nb):
                y_ref[b, pl.ds(t, nct, stride=c), :] = uc_scr[t, pl.ds(b, nct, stride=nb), :]

    @pl.when(j == pl.num_programs(1) - 1)
    def _():
        xre_ref[...] = xr
        xim_ref[...] = xi


def s5_mix(u, x0re, x0im, tables, chunk):
    b, t, e = u.shape
    tiles, bsre, bsim, csre, csim, lamre, lamim = tables
    nblk = e // LANES
    gp = (LANES // S5_GROUP) * S5_STATE
    nct = min(t // chunk, max(1, S5_ROWS // b))
    assert (t // chunk) % nct == 0
    tt = nct * chunk
    rows = nct * b
    kc = chunk * LANES
    last = tiles.shape[1] // chunk - 1
    assert tiles.shape[1] % chunk == 0
    bs_spec = pl.BlockSpec((1, kc, gp), lambda i, j: (i, last, 0))
    cs_spec = pl.BlockSpec((1, kc, gp), lambda i, j: (i, 0, 0))
    lam_spec = pl.BlockSpec((1, 1, 1, gp), lambda i, j: (i, chunk, 0, 0))
    y, xre, xim = pl.pallas_call(
        functools.partial(_s5_main_kernel, chunk=chunk),
        grid=(nblk, t // tt),
        in_specs=[pl.BlockSpec((b, tt, LANES), lambda i, j: (0, j, i)),
                  pl.BlockSpec((1, chunk, LANES, LANES), lambda i, j: (i, 0, 0, 0)),
                  bs_spec, bs_spec, cs_spec, cs_spec, lam_spec, lam_spec,
                  pl.BlockSpec((b, gp), lambda i, j: (0, i)), pl.BlockSpec((b, gp), lambda i, j: (0, i))],
        out_specs=[pl.BlockSpec((b, tt, LANES), lambda i, j: (0, j, i)),
                   pl.BlockSpec((b, gp), lambda i, j: (0, i)), pl.BlockSpec((b, gp), lambda i, j: (0, i))],
        out_shape=[jax.ShapeDtypeStruct((b, t, e), F32),
                   jax.ShapeDtypeStruct((b, S5_GROUPS * S5_STATE), F32),
                   jax.ShapeDtypeStruct((b, S5_GROUPS * S5_STATE), F32)],
        scratch_shapes=[pltpu.VMEM((chunk, rows, LANES), F32), pltpu.VMEM((chunk // 2, 2 * LANES, 2 * LANES), BF16)]
        + [pltpu.VMEM((rows, gp), F32)] * 4 + [pltpu.VMEM((b, gp), F32)] * 2,
        compiler_params=pltpu.CompilerParams(dimension_semantics=("parallel", "arbitrary"),
                                             vmem_limit_bytes=VMEM_LIMIT),
        name="s5_main",
    )(u, tiles, bsre, bsim, csre, csim, lamre, lamim, x0re.reshape(b, -1), x0im.reshape(b, -1))
    return y, xre.reshape(b, S5_GROUPS, S5_STATE), xim.reshape(b, S5_GROUPS, S5_STATE)


def _trunk(x, st_shift, st_wkv, st_gla, st_re, st_im, w):
    b, t, d = x.shape
    m = b * t
    x2 = x.reshape(m, d)

    p_r, p_g, p_xa = norm_proj(x2, w['norm_mix_pre'][0], [w['w_in_rwkv'], w['w_in_gla'], w['w_in_xa']],
                               [BF16, BF16, BF16])
    o_r, wkv = rwkv_mix(p_r.reshape(b, t, -1), st_shift[0], st_wkv[0], w['rwkv'])
    o_g, gla = gla_mix(p_g.reshape(b, t, -1), p_xa.reshape(b, t, -1), st_gla[0], w['gla_wa2'], w['gla_ba'],
                       w['gla_norm'])
    shift = p_r.reshape(b, t, -1)[:, -1].astype(F32)
    x2 = proj_post([o_r.reshape(m, -1), o_g.reshape(m, -1)], [w['w_out_rwkv'], w['w_out_gla']],
                   w['norm_mix_post'][0], x2)
    x2 = ffn(x2, w['norm_ffn_pre'][0], w['w_ff_up'][0], w['w_ff_down'][0], w['norm_ffn_post'][0])

    (u,) = norm_proj(x2, w['norm_mix_pre'][1], [w['s5_w_in']], [F32])
    chunk = min(S5_CHUNK, t)
    y, s_re, s_im = s5_mix(u.reshape(b, t, d), st_re[0], st_im[0], w['s5_tables'], chunk)
    x2 = proj_post([y.reshape(m, d)], [w['s5_w_out']], w['norm_mix_post'][1], x2, glu=True)
    x2 = ffn(x2, w['norm_ffn_pre'][1], w['w_ff_up'][1], w['w_ff_down'][1], w['norm_ffn_post'][1])
    return x2.reshape(b, t, d), shift[None], wkv[None], gla[None], s_re[None], s_im[None]


def kernel(x_prompt, x_sample, state_rwkv_shift, state_rwkv_wkv, state_gla, state_s5_re, state_s5_im,
           norm_mix_pre, norm_mix_post, norm_ffn_pre, norm_ffn_post,
           w_mix_in, w_mix_out, rwkv_mu, rwkv_w0, rwkv_w2, rwkv_a0, rwkv_a2, rwkv_g2,
           rwkv_k_k, rwkv_k_a, rwkv_r_k, rwkv_gn_g, rwkv_gn_b, gla_wa2, gla_ba, gla_norm,
           s5_w_in, s5_lam_re, s5_lam_im, s5_log_dt, s5_b_re, s5_b_im, s5_c_re, s5_c_im, s5_d, s5_w_out,
           w_ff_up, w_ff_down):
    assert norm_mix_pre.shape[0] == 2, "two layers: one RWKV-7/GLA layer, one S5 layer"
    bf = lambda a: a.astype(BF16)
    w_in = w_mix_in[0]
    gla0 = RWKV_PROJ
    xa0 = gla0 + 2 * GLA_K + GLA_V
    w_in_gla = jnp.concatenate([w_in[:, gla0:xa0], w_in[:, xa0 + GLA_LORA:]], axis=1)
    w_in_xa = jnp.pad(w_in[:, xa0:xa0 + GLA_LORA], ((0, 0), (0, LANES - GLA_LORA)))
    s5_prm = dict(lam_re=s5_lam_re[0], lam_im=s5_lam_im[0], log_dt=s5_log_dt[0], b_re=s5_b_re[0],
                  b_im=s5_b_im[0], c_re=s5_c_re[0], c_im=s5_c_im[0], d=s5_d[0])
    tables = s5_prep(s5_prm, min(S5_CHUNK, max(x_prompt.shape[1], x_sample.shape[1])))
    w = dict(
        norm_mix_pre=norm_mix_pre, norm_mix_post=norm_mix_post, norm_ffn_pre=norm_ffn_pre,
        norm_ffn_post=norm_ffn_post,
        w_in_rwkv=bf(w_in[:, :RWKV_PROJ]), w_in_gla=bf(w_in_gla), w_in_xa=bf(w_in_xa),
        w_out_rwkv=bf(w_mix_out[0, :RWKV_W]), w_out_gla=bf(w_mix_out[0, RWKV_W:]),
        rwkv=dict(mu=rwkv_mu[0], w0=rwkv_w0[0], w2=rwkv_w2[0], a0=rwkv_a0[0], a2=rwkv_a2[0], g2=rwkv_g2[0],
                  k_k=rwkv_k_k[0], k_a=rwkv_k_a[0], r_k=rwkv_r_k[0], gn_g=rwkv_gn_g[0], gn_b=rwkv_gn_b[0]),
        gla_wa2=bf(jnp.pad(gla_wa2[0], ((0, LANES - GLA_LORA), (0, 0)))), gla_ba=gla_ba[0], gla_norm=gla_norm[0],
        s5_w_in=bf(s5_w_in[0]), s5_w_out=bf(s5_w_out[0]), s5_tables=tables,
        w_ff_up=bf(w_ff_up), w_ff_down=bf(w_ff_down),
    )
    bp = x_prompt.shape[0]
    zeros = lambda s: jnp.zeros((s.shape[0], bp) + s.shape[2:], x_prompt.dtype)
    y_p, sh_p, wkv_p, gla_p, re_p, im_p = _trunk(x_prompt, zeros(state_rwkv_shift), zeros(state_rwkv_wkv),
                                                 zeros(state_gla), zeros(state_s5_re), zeros(state_s5_im), w)
    y_s, sh_s, wkv_s, gla_s, re_s, im_s = _trunk(x_sample, state_rwkv_shift, state_rwkv_wkv, state_gla,
                                                 state_s5_re, state_s5_im, w)
    return (y_p, y_s, sh_p, sh_s, wkv_p, wkv_s, gla_p, gla_s, re_p, re_s, im_p, im_s)
```

```python
import functools
import math

import jax
import jax.numpy as jnp
from jax import lax
from jax.experimental import pallas as pl
from jax.experimental.pallas import tpu as pltpu

F32 = jnp.float32
BF16 = jnp.bfloat16

D_MODEL = 1024
D_FF = 4096
NORM_EPS = 1e-6
RWKV_HEADS = 8
RWKV_HEAD = 64
RWKV_W = RWKV_HEADS * RWKV_HEAD
RWKV_PROJ = 1792
RWKV_CHUNK = 64
RWKV_ROWS = 64
RWKV_STEP_ROWS = 256
GN_EPS = 64e-5
_P_GRAM = 1
_P_INV = 1
_P_SOLVE = 1
_P_MIX = 1
_P_STATE = 1
_P_HEADSUM = 1
GLA_HEADS = 4
GLA_DK = 64
GLA_DV = 128
GLA_K = GLA_HEADS * GLA_DK
GLA_V = GLA_HEADS * GLA_DV
GLA_LORA = 16
GLA_TAU = 16.0
GLA_CHUNK = 32
GLA_MAIN = 2 * GLA_K + 2 * GLA_V
GLA_TILE = 256
GLA_TILE_SEQS = 1
GLA_SMALL_ROWS = 64
S5_GROUPS = 64
S5_GROUP = 16
S5_STATE = 64
S5_CHUNK = 16
S5_ROWS = 512
LANES = 128
VMEM_LIMIT = 48 * 1024 * 1024

_NN = (((1,), (0,)), ((), ()))
_NT = (((1,), (1,)), ((), ()))
_TN = (((0,), (0,)), ((), ()))


def _dg(a, b, dn=_NN):
    return lax.dot_general(a.astype(BF16), b.astype(BF16), dn, preferred_element_type=F32)


def _split2(x):
    hi = x.astype(BF16)
    lo = (x - hi.astype(F32)).astype(BF16)
    return hi, lo


def _split3(x):
    hi = x.astype(BF16)
    r1 = x - hi.astype(F32)
    mid = r1.astype(BF16)
    lo = (r1 - mid.astype(F32)).astype(BF16)
    return hi, mid, lo


def _dg3(a, b, dn=_NN):
    ah, al = _split2(a)
    bh, bl = _split2(b)
    d = lambda x, y: lax.dot_general(x, y, dn, preferred_element_type=F32)
    return d(ah, bh) + (d(ah, bl) + d(al, bh))


def _mp(a, b, dn, passes):
    return _dg(a, b, dn) if passes == 1 else _dg3(a, b, dn)


def _dg_sel(sel, x, dn=_NN, terms=3):
    d = lambda y: lax.dot_general(sel, y, dn, preferred_element_type=F32)
    if terms == 2:
        h, l = _split2(x)
        return d(h) + d(l)
    h, m, l = _split3(x)
    return d(h) + (d(m) + d(l))


def _dg_selr(x, sel, dn=_NN):
    d = lambda y: lax.dot_general(y, sel, dn, preferred_element_type=F32)
    h, m, l = _split3(x)
    return d(h) + (d(m) + d(l))


def _rms(x, g):
    return x * lax.rsqrt(jnp.mean(x * x, axis=-1, keepdims=True) + NORM_EPS) * g


def _row_tile(m, want):
    t = min(m, want)
    assert m % t == 0, (m, t)
    return t


def _full(shape):
    nd = len(shape)
    return pl.BlockSpec(shape, lambda *_: (0,) * nd)


def _token_blocks(b, t, nb, tt):
    if tt == t:
        return (b // nb, nb * t), (1, nb * t), (lambda i, j: (i, 0, 0))
    return (b, t), (nb, tt), (lambda i, j: (i, j, 0))


def _norm_proj_kernel(x_ref, g_ref, *refs):
    n = len(refs) // 2
    h = _rms(x_ref[...], g_ref[...]).astype(BF16)
    for w_ref, o_ref in zip(refs[:n], refs[n:]):
        o_ref[...] = jnp.dot(h, w_ref[...], preferred_element_type=F32).astype(o_ref.dtype)


def norm_proj(x, g, ws, out_dtypes, tm=512):
    m, d = x.shape
    tm = _row_tile(m, tm)
    return pl.pallas_call(
        _norm_proj_kernel,
        grid=(m // tm,),
        in_specs=[pl.BlockSpec((tm, d), lambda i: (i, 0)), _full((1, d))]
        + [_full(w.shape) for w in ws],
        out_specs=[pl.BlockSpec((tm, w.shape[1]), lambda i: (i, 0)) for w in ws],
        out_shape=[jax.ShapeDtypeStruct((m, w.shape[1]), dt) for w, dt in zip(ws, out_dtypes)],
        compiler_params=pltpu.CompilerParams(dimension_semantics=("parallel",), vmem_limit_bytes=VMEM_LIMIT),
        name="norm_proj",
    )(x, g.reshape(1, d), *ws)


def _proj_post_kernel(*refs, n_in, glu):
    a_refs = refs[:n_in]
    w_refs = refs[n_in:2 * n_in]
    g_ref, x_ref, o_ref = refs[2 * n_in:]
    m = None
    for a_ref, w_ref in zip(a_refs, w_refs):
        t = jnp.dot(a_ref[...].astype(BF16), w_ref[...], preferred_element_type=F32)
        m = t if m is None else m + t
    if glu:
        m = m[:, :D_MODEL] * jax.nn.sigmoid(m[:, D_MODEL:])
    o_ref[...] = x_ref[...] + _rms(m, g_ref[...])


def proj_post(a_list, w_list, g, x, glu=False, tm=512):
    m, d = x.shape
    tm = _row_tile(m, tm)
    n_in = len(a_list)
    return pl.pallas_call(
        functools.partial(_proj_post_kernel, n_in=n_in, glu=glu),
        grid=(m // tm,),
        in_specs=[pl.BlockSpec((tm, a.shape[1]), lambda i: (i, 0)) for a in a_list]
        + [_full(w.shape) for w in w_list]
        + [_full((1, d)), pl.BlockSpec((tm, d), lambda i: (i, 0))],
        out_specs=pl.BlockSpec((tm, d), lambda i: (i, 0)),
        out_shape=jax.ShapeDtypeStruct((m, d), F32),
        compiler_params=pltpu.CompilerParams(dimension_semantics=("parallel",), vmem_limit_bytes=VMEM_LIMIT),
        name="proj_post",
    )(*a_list, *w_list, g.reshape(1, d), x)


def _ffn_kernel(x_ref, gpre_ref, wup_ref, wdn_ref, gpost_ref, o_ref, h_scr, acc_scr):
    j = pl.program_id(1)

    @pl.when(j == 0)
    def _():
        h_scr[...] = _rms(x_ref[...], gpre_ref[...]).astype(BF16)
        acc_scr[...] = jnp.zeros_like(acc_scr)

    u = jnp.dot(h_scr[...], wup_ref[...], preferred_element_type=F32)
    u = jnp.square(jnp.maximum(u, 0.0)).astype(BF16)
    acc_scr[...] += jnp.dot(u, wdn_ref[...], preferred_element_type=F32)

    @pl.when(j == pl.num_programs(1) - 1)
    def _():
        o_ref[...] = x_ref[...] + _rms(acc_scr[...], gpost_ref[...])


def ffn(x, g_pre, w_up, w_down, g_post, tm=1024, tf=1024):
    m, d = x.shape
    f = w_up.shape[1]
    tm = _row_tile(m, tm)
    return pl.pallas_call(
        _ffn_kernel,
        grid=(m // tm, f // tf),
        in_specs=[
            pl.BlockSpec((tm, d), lambda i, j: (i, 0)),
            _full((1, d)),
            pl.BlockSpec((d, tf), lambda i, j: (0, j)),
            pl.BlockSpec((tf, d), lambda i, j: (j, 0)),
            _full((1, d)),
        ],
        out_specs=pl.BlockSpec((tm, d), lambda i, j: (i, 0)),
        out_shape=jax.ShapeDtypeStruct((m, d), F32),
        scratch_shapes=[pltpu.VMEM((tm, d), BF16), pltpu.VMEM((tm, d), F32)],
        compiler_params=pltpu.CompilerParams(dimension_semantics=("parallel", "arbitrary"),
                                             vmem_limit_bytes=VMEM_LIMIT),
        name="ffn",
    )(x, g_pre.reshape(1, d), w_up, w_down, g_post.reshape(1, d))


def _head_sum(x, ones_bd):
    d = lambda y: jnp.dot(y, ones_bd, preferred_element_type=F32)
    if _P_HEADSUM == 1:
        return d(x.astype(BF16))
    hi, lo = _split2(x)
    return d(hi) + d(lo)


def _rwkv_kernel(p_ref, sh0_ref, s0_ref, mu_ref, w0_ref, w2_ref, a0_ref, a2_ref, g2_ref, kk_ref, ka_ref,
                 rk_ref, gng_ref, gnb_ref, ones_ref, o_ref, sout_ref, state_scr, carry_scr, y_scr, *, chunk):
    c = chunk
    nb = s0_ref.shape[0]
    rows = nb * c
    shift = int(math.log2(c))
    n = pl.program_id(1)

    @pl.when(n == 0)
    def _():
        carry_scr[...] = sh0_ref[...]
        for i in range(nb):
            for h in range(RWKV_HEADS):
                state_scr[i, h // 2, :, (h % 2) * RWKV_HEAD:(h % 2 + 1) * RWKV_HEAD] = s0_ref[i, h]

    p3 = p_ref[...]
    p = p3.astype(F32).reshape(rows, RWKV_PROJ)
    t_idx = jnp.bitwise_and(lax.broadcasted_iota(jnp.int32, (rows, 1), 0), c - 1)
    carry = jnp.broadcast_to(carry_scr[...], (nb, c, RWKV_PROJ)).reshape(rows, RWKV_PROJ)
    prev = jnp.where(t_idx == 0, carry, pltpu.roll(p, 1, 0))
    if p3.shape[0] == nb:
        carry_scr[...] = p3[:, c - 1:c, :].astype(F32)
    z = p + (prev - p) * mu_ref[...]

    w_ = RWKV_W
    r = z[:, 0:w_]
    k = z[:, w_:2 * w_]
    v = z[:, 2 * w_:3 * w_]
    xw = z[:, 3 * w_:3 * w_ + 64]
    xa = z[:, 3 * w_ + 64:3 * w_ + 128]
    xg = z[:, 3 * w_ + 128:3 * w_ + 256]

    wlog = -jax.nn.softplus(-(w0_ref[...] + _dg(jnp.tanh(xw), w2_ref[...]))) - 0.5
    logd = -jnp.exp(wlog)
    a = jax.nn.sigmoid(a0_ref[...] + _dg(xa, a2_ref[...]))
    g = _dg(jax.nn.sigmoid(xg), g2_ref[...])

    ones_bd = ones_ref[...]
    kk = k * kk_ref[...]
    kk = kk * lax.rsqrt(jnp.maximum(_head_sum(kk * kk, ones_bd), 1e-24))
    k2 = k * (1.0 + (a - 1.0) * ka_ref[...])
    bb = kk * a

    def masks(m):
        ri = lax.broadcasted_iota(jnp.int32, (m, m), 0)
        ci = lax.broadcasted_iota(jnp.int32, (m, m), 1)
        same = lax.shift_right_logical(ri, shift) == lax.shift_right_logical(ci, shift)
        return same, jnp.logical_and(same, ri > ci), jnp.logical_and(same, ri >= ci), (ri == ci).astype(F32)

    same_all, _, lower_eq_all, _ = masks(rows)
    cum = _dg_sel(lower_eq_all.astype(BF16), logd, terms=2)
    tot = _dg_sel(same_all.astype(BF16), logd, terms=2)
    g_in = jnp.exp(cum)
    g_ex = jnp.exp(cum - logd)
    g_inv = jnp.exp(-cum)
    g_hat = jnp.exp(tot - cum)
    g_all = jnp.exp(tot)
    rt = r * g_in
    kt = kk * g_ex
    ktil = k2 * g_inv
    btil = bb * g_inv
    khat = k2 * g_hat
    bhat = bb * g_hat

    gr = RWKV_ROWS
    assert rows % gr == 0 and gr == RWKV_HEAD and gr % c == 0
    spg = gr // c
    hp = 2 * RWKV_HEAD
    ri = lax.broadcasted_iota(jnp.int32, (gr, hp), 0)
    si = jnp.bitwise_and(lax.broadcasted_iota(jnp.int32, (gr, hp), 1), RWKV_HEAD - 1)
    same = lax.shift_right_logical(ri, shift) == lax.shift_right_logical(si, shift)
    lower = jnp.logical_and(same, ri > si)
    lower_eq = jnp.logical_and(same, ri >= si)
    eye = (ri == si).astype(F32)
    bi = lax.broadcasted_iota(jnp.int32, (hp, hp), 0)
    bj = lax.broadcasted_iota(jnp.int32, (hp, hp), 1)
    on_diag = lax.shift_right_logical(bi, 6) == lax.shift_right_logical(bj, 6)
    first_head = lax.broadcasted_iota(jnp.int32, (RWKV_HEAD, hp), 1) < RWKV_HEAD

    def bd(x2):
        x2 = x2.astype(BF16)
        return jnp.where(on_diag, jnp.concatenate([x2, x2], axis=0), jnp.zeros((), BF16))

    probs = [(g_, q) for g_ in range(rows // gr) for q in range(RWKV_HEADS // 2)]
    np_ = len(probs)
    psl = lambda q: slice(q * hp, (q + 1) * hp)
    gsl = lambda g_: slice(g_ * gr, (g_ + 1) * gr)
    pick = lambda x, g_, q: x[gsl(g_), psl(q)]
    seq_rows = lambda g_, i: slice(g_ * gr + i * c, g_ * gr + (i + 1) * c)
    cat = lambda xs: jnp.concatenate(xs, axis=0) if len(xs) > 1 else xs[0]
    kr = [jnp.concatenate([pick(kt, *q), pick(rt, *q)], axis=0) for q in probs]
    gb = [_dg(kr[j], bd(pick(btil, *q)), _NT) for j, q in enumerate(probs)]
    gk = [_dg(kr[j], bd(pick(ktil, *q)), _NT) for j, q in enumerate(probs)]
    a_b = [jnp.where(lower, x[:gr], 0.0) for x in gb]
    a_rb = [jnp.where(lower_eq, x[gr:], 0.0) for x in gb]
    a_kk = [jnp.concatenate([jnp.where(lower, x[:gr], 0.0), jnp.where(lower_eq, x[gr:], 0.0)], axis=0)
            for x in gk]
    ks = [[_dg(jnp.concatenate([kt[seq_rows(g_, i), psl(q)], rt[seq_rows(g_, i), psl(q)]], axis=0),
               bd(state_scr[g_ * spg + i, q]), _NT) for i in range(spg)] for g_, q in probs]
    x0 = [cat([x[:c] for x in ks[j]]) for j in range(np_)]
    r0 = [cat([x[c:] for x in ks[j]]) for j in range(np_)]
    av = [_dg(a_kk[j], bd(pick(v, *q))) for j, q in enumerate(probs)]
    tinv = [eye - x for x in a_b]
    pw = [_dg(x, bd(x)) for x in a_b]
    span = 2
    while span < c:
        span *= 2
        if span < c:
            pr = [_dg(jnp.concatenate([t_, p_], axis=0), bd(p_)) for t_, p_ in zip(tinv, pw)]
            tinv = [t_ + x[:gr] for t_, x in zip(tinv, pr)]
            pw = [x[gr:] for x in pr]
        else:
            tinv = [t_ + _dg(t_, bd(p_)) for t_, p_ in zip(tinv, pw)]
    u = [_dg(tinv[j], bd(x0[j] + av[j][:gr])) for j in range(np_)]
    for j, (g_, q) in enumerate(probs):
        y_scr[gsl(g_), psl(q)] = r0[j] + av[j][gr:] - _dg(a_rb[j], bd(u[j]))
    for j, (g_, q) in enumerate(probs):
        for i in range(spg):
            rs = seq_rows(g_, i)
            vu = jnp.concatenate([v[rs, psl(q)], -u[j][i * c:(i + 1) * c]], axis=0)
            kb = jnp.concatenate([khat[rs, psl(q)], bhat[rs, psl(q)]], axis=0)
            full = _dg(vu, kb, _TN)
            upd = jnp.where(first_head, full[:RWKV_HEAD], full[RWKV_HEAD:])
            s_i = g_ * spg + i
            state_scr[s_i, q] = state_scr[s_i, q] * g_all[rs.start:rs.start + 1, psl(q)] + upd

    y = y_scr[...]
    inv_n = 1.0 / RWKV_HEAD
    mean = _head_sum(y, ones_bd) * inv_n
    yc = y - mean
    var = _head_sum(yc * yc, ones_bd) * inv_n
    yn = yc * lax.rsqrt(var + GN_EPS) * gng_ref[...] + gnb_ref[...]
    bonus = _head_sum(r * k2 * rk_ref[...], ones_bd) * v
    o_ref[...] = ((yn + bonus) * g).astype(o_ref.dtype).reshape(o_ref.shape)

    @pl.when(n == pl.num_programs(1) - 1)
    def _():
        for i in range(nb):
            for h in range(RWKV_HEADS):
                sout_ref[i, h] = state_scr[i, h // 2, :, (h % 2) * RWKV_HEAD:(h % 2 + 1) * RWKV_HEAD]


def rwkv_mix(p, shift0, s0, prm):
    b, t, _ = p.shape
    c = min(RWKV_CHUNK, t)
    assert t % c == 0
    nb = min(b, RWKV_STEP_ROWS // c)
    assert b % nb == 0
    w_ = RWKV_W
    row = lambda x: x.reshape(1, -1)
    ones_bd = jnp.kron(jnp.eye(RWKV_HEADS, dtype=F32), jnp.ones((RWKV_HEAD, RWKV_HEAD), F32)).astype(BF16)
    consts = [row(prm['mu']), row(prm['w0']), prm['w2'].astype(BF16), row(prm['a0']), prm['a2'].astype(BF16),
              prm['g2'].astype(BF16), row(prm['k_k']), row(prm['k_a']), row(prm['r_k']), row(prm['gn_g']),
              row(prm['gn_b']), ones_bd]
    tok_shape, tok_block, tok_index = _token_blocks(b, t, nb, c)
    o, s_out = pl.pallas_call(
        functools.partial(_rwkv_kernel, chunk=c),
        grid=(b // nb, t // c),
        in_specs=[
            pl.BlockSpec(tok_block + (RWKV_PROJ,), tok_index),
            pl.BlockSpec((nb, 1, RWKV_PROJ), lambda i, j: (i, 0, 0)),
            pl.BlockSpec((nb, RWKV_HEADS, RWKV_HEAD, RWKV_HEAD), lambda i, j: (i, 0, 0, 0)),
        ] + [_full(x.shape) for x in consts],
        out_specs=[
            pl.BlockSpec(tok_block + (w_,), tok_index),
            pl.BlockSpec((nb, RWKV_HEADS, RWKV_HEAD, RWKV_HEAD), lambda i, j: (i, 0, 0, 0)),
        ],
        out_shape=[jax.ShapeDtypeStruct(tok_shape + (w_,), BF16),
                   jax.ShapeDtypeStruct((b, RWKV_HEADS, RWKV_HEAD, RWKV_HEAD), F32)],
        scratch_shapes=[pltpu.VMEM((nb, RWKV_HEADS // 2, RWKV_HEAD, 2 * RWKV_HEAD), F32),
                        pltpu.VMEM((nb, 1, RWKV_PROJ), F32),
                        pltpu.VMEM((nb * c, w_), F32)],
        compiler_params=pltpu.CompilerParams(dimension_semantics=("parallel", "arbitrary")),
        name="rwkv_mix",
    )(p.reshape(tok_shape + (RWKV_PROJ,)), shift0.reshape(b, 1, RWKV_PROJ), s0, *consts)
    return o.reshape(b, t, w_), s_out


def _gla_kernel(pg_ref, xa_ref, s0_ref, wa2_ref, ba_ref, gn_ref, o_ref, sout_ref, state_scr, o_scr, *, chunk):
    c = chunk
    nb = s0_ref.shape[0]
    rows = pg_ref.shape[0] * pg_ref.shape[1]
    tt = rows // nb
    nc = tt // c
    n = pl.program_id(1)
    heads = range(GLA_HEADS)
    seqs = range(nb)
    ident = lambda m: (lax.broadcasted_iota(jnp.int32, (m, m), 0)
                       == lax.broadcasted_iota(jnp.int32, (m, m), 1)).astype(BF16)

    @pl.when(n == 0)
    def _():
        eye_k = ident(GLA_DK)
        for i in seqs:
            for h in heads:
                state_scr[i, h] = _dg_selr(s0_ref[i, h], eye_k, _TN)

    pg = pg_ref[...].astype(F32).reshape(rows, GLA_MAIN)
    q = pg[:, 0:GLA_K] * (GLA_DK ** -0.5)
    k = pg[:, GLA_K:2 * GLA_K]
    v = pg[:, 2 * GLA_K:2 * GLA_K + GLA_V]
    gz = pg[:, 2 * GLA_K + GLA_V:]
    xa = xa_ref[...].astype(F32).reshape(rows, LANES)
    gk = jax.nn.log_sigmoid(_dg(xa, wa2_ref[...]) + ba_ref[...]) * (1.0 / GLA_TAU)

    shift = int(math.log2(c))
    ri = lax.broadcasted_iota(jnp.int32, (rows, rows), 0)
    ci = lax.broadcasted_iota(jnp.int32, (rows, rows), 1)
    same = lax.shift_right_logical(ri, shift) == lax.shift_right_logical(ci, shift)
    causal = jnp.logical_and(same, ri >= ci)
    bc = _dg_sel(causal.astype(BF16), gk)
    bl = _dg_sel(same.astype(BF16), gk)
    qt = q * jnp.exp(bc)
    kt = k * jnp.exp(-bc)
    ks = k * jnp.exp(bl - bc)
    ebl = jnp.exp(bl)

    ksl = [slice(h * GLA_DK, (h + 1) * GLA_DK) for h in heads]
    vsl = [slice(h * GLA_DV, (h + 1) * GLA_DV) for h in heads]
    chunks = [(i, slice(i * tt + j * c, i * tt + (j + 1) * c)) for i in seqs for j in range(nc)]
    att = [jnp.where(causal, _dg(qt[:, ksl[h]], kt[:, ksl[h]], _NT), 0.0) for h in heads]
    o_intra = [_dg(att[h], v[:, vsl[h]]) for h in heads]
    kv = [[_dg(v[rs, vsl[h]], ks[rs, ksl[h]], _TN) for _, rs in chunks] for h in heads]
    st = [[state_scr[i, h] for i in seqs] for h in heads]
    for ci_, (i, rs) in enumerate(chunks):
        for h in heads:
            o_scr[rs, vsl[h]] = o_intra[h][rs] + _dg(qt[rs, ksl[h]], st[h][i], _NT)
            st[h][i] = st[h][i] * ebl[rs.start:rs.start + 1, ksl[h]] + kv[h][ci_]
    for h in heads:
        for i in seqs:
            state_scr[i, h] = st[h][i]

    for h in heads:
        o_h = o_scr[:, vsl[h]]
        o_h = o_h * lax.rsqrt(jnp.mean(o_h * o_h, axis=-1, keepdims=True) + NORM_EPS) * gn_ref[...]
        o_scr[:, vsl[h]] = o_h * jax.nn.silu(gz[:, vsl[h]])
    o_ref[...] = o_scr[...].astype(o_ref.dtype).reshape(o_ref.shape)

    @pl.when(n == pl.num_programs(1) - 1)
    def _():
        eye_v = ident(GLA_DV)
        for i in seqs:
            for h in heads:
                sout_ref[i, h] = _dg_selr(state_scr[i, h], eye_v, _TN)


def gla_mix(pg, pxa, s0, wa2_pad, ba, g_norm):
    b, t, _ = pg.shape
    tt = min(GLA_TILE, t)
    c = min(GLA_CHUNK, t)
    nb = min(b, max(GLA_TILE_SEQS, GLA_SMALL_ROWS // tt))
    assert t % tt == 0 and tt % c == 0 and b % nb == 0
    tok_shape, tok_block, tok_index = _token_blocks(b, t, nb, tt)
    o, s_out = pl.pallas_call(
        functools.partial(_gla_kernel, chunk=c),
        grid=(b // nb, t // tt),
        in_specs=[
            pl.BlockSpec(tok_block + (GLA_MAIN,), tok_index),
            pl.BlockSpec(tok_block + (LANES,), tok_index),
            pl.BlockSpec((nb, GLA_HEADS, GLA_DK, GLA_DV), lambda i, j: (i, 0, 0, 0)),
            _full(wa2_pad.shape), _full((1, GLA_K)), _full((1, GLA_DV)),
        ],
        out_specs=[
            pl.BlockSpec(tok_block + (GLA_V,), tok_index),
            pl.BlockSpec((nb, GLA_HEADS, GLA_DK, GLA_DV), lambda i, j: (i, 0, 0, 0)),
        ],
        out_shape=[jax.ShapeDtypeStruct(tok_shape + (GLA_V,), BF16),
                   jax.ShapeDtypeStruct((b, GLA_HEADS, GLA_DK, GLA_DV), F32)],
        scratch_shapes=[pltpu.VMEM((nb, GLA_HEADS, GLA_DV, GLA_DK), F32), pltpu.VMEM((nb * tt, GLA_V), F32)],
        compiler_params=pltpu.CompilerParams(dimension_semantics=("parallel", "arbitrary")),
        name="gla_mix",
    )(pg.reshape(tok_shape + (GLA_MAIN,)), pxa.reshape(tok_shape + (LANES,)), s0, wa2_pad,
      ba.reshape(1, GLA_K), g_norm.reshape(1, GLA_DV))
    return o.reshape(b, t, GLA_V), s_out


def _cmul(ar, ai, br, bi):
    return ar * br - ai * bi, ar * bi + ai * br


def _s5_prep_kernel(lre_ref, lim_ref, ldt_ref, cre_ref, cim_ref, btre_ref, btim_ref, d_ref,
                    tiles_ref, bsre_ref, bsim_ref, csre_ref, csim_ref, lamre_ref, lamim_ref, *, chunk):
    c = chunk
    p_ = S5_STATE
    gb = LANES // S5_GROUP
    tiles_ref[...] = jnp.zeros_like(tiles_ref)
    bsre_ref[...] = jnp.zeros_like(bsre_ref)
    bsim_ref[...] = jnp.zeros_like(bsim_ref)
    csre_ref[...] = jnp.zeros_like(csre_ref)
    csim_ref[...] = jnp.zeros_like(csim_ref)
    r16 = lax.broadcasted_iota(jnp.int32, (S5_GROUP, S5_GROUP), 0)
    c16 = lax.broadcasted_iota(jnp.int32, (S5_GROUP, S5_GROUP), 1)
    for gi in range(gb):
        lr = lre_ref[0, gi:gi + 1, :]
        li = lim_ref[0, gi:gi + 1, :]
        dt = jnp.exp(ldt_ref[0, gi:gi + 1, :])
        mag = jnp.exp(lr * dt)
        ang = li * dt
        abr, abi = mag * jnp.cos(ang), mag * jnp.sin(ang)
        den = lr * lr + li * li
        fr = ((abr - 1.0) * lr + abi * li) / den
        fi = (abi * lr - (abr - 1.0) * li) / den
        pows = [(jnp.ones_like(abr), jnp.zeros_like(abr))]
        for _ in range(c):
            pows.append(_cmul(pows[-1][0], pows[-1][1], abr, abi))
        cre, cim = cre_ref[0, gi], cim_ref[0, gi]
        btre, btim = btre_ref[0, gi], btim_ref[0, gi]
        q_re, q_im = [], []
        col = slice(gi * p_, (gi + 1) * p_)
        ch = slice(gi * S5_GROUP, (gi + 1) * S5_GROUP)
        for t in range(c):
            er, ei = _cmul(pows[t][0], pows[t][1], fr, fi)
            qr, qi = _cmul(cre, cim, er, ei)
            q_re.append(qr)
            q_im.append(qi)
            er, ei = _cmul(pows[c - 1 - t][0], pows[c - 1 - t][1], fr, fi)
            zr, zi = _cmul(btre, btim, er, ei)
            rows = slice(t * LANES + gi * S5_GROUP, t * LANES + (gi + 1) * S5_GROUP)
            bsre_ref[0, rows, col] = zr.astype(BF16)
            bsim_ref[0, rows, col] = zi.astype(BF16)
            kr, ki = pows[t + 1]
            csre_ref[0, rows, col] = (cre * kr - cim * ki).astype(BF16)
            csim_ref[0, rows, col] = (-cre * ki - cim * kr).astype(BF16)
        q_re = jnp.concatenate(q_re, axis=0)
        q_im = jnp.concatenate(q_im, axis=0)
        m = _dg3(btre, q_re, _NT) - _dg3(btim, q_im, _NT)
        for t in range(c):
            blk = m[:, t * S5_GROUP:(t + 1) * S5_GROUP]
            if t == 0:
                blk = blk + jnp.where(r16 == c16, d_ref[0, gi], 0.0)
            tiles_ref[0, t, ch, ch] = blk.astype(BF16)
        for t in range(c + 1):
            lamre_ref[0, t, :, col] = pows[t][0]
            lamim_ref[0, t, :, col] = pows[t][1]


def s5_prep(prm, chunk):
    g, p_ = S5_GROUPS, S5_STATE
    gb = LANES // S5_GROUP
    nblk = g // gb
    kc = chunk * LANES
    grp = lambda x: x.reshape((nblk, gb) + x.shape[1:])
    args = [grp(prm['lam_re']), grp(prm['lam_im']), grp(prm['log_dt'].reshape(g, 1)),
            grp(prm['c_re']), grp(prm['c_im']),
            grp(jnp.swapaxes(prm['b_re'], 1, 2)), grp(jnp.swapaxes(prm['b_im'], 1, 2)),
            grp(prm['d'].reshape(g, S5_GROUP, 1))]
    blk = lambda x: pl.BlockSpec((1,) + x.shape[1:], lambda i: (i,) + (0,) * (x.ndim - 1))
    out_shape = [jax.ShapeDtypeStruct((nblk, chunk, LANES, LANES), BF16)] + \
        [jax.ShapeDtypeStruct((nblk, kc, gb * p_), BF16)] * 4 + \
        [jax.ShapeDtypeStruct((nblk, chunk + 1, 1, gb * p_), F32)] * 2
    return pl.pallas_call(
        functools.partial(_s5_prep_kernel, chunk=chunk),
        grid=(nblk,),
        in_specs=[blk(a) for a in args],
        out_specs=[blk(o) for o in out_shape],
        out_shape=out_shape,
        compiler_params=pltpu.CompilerParams(dimension_semantics=("parallel",)),
        name="s5_prep",
    )(*args)


def _s5_main_kernel(u_ref, tiles_ref, bsre_ref, bsim_ref, csre_ref, csim_ref, lamre_ref, lamim_ref, x0re_ref,
                    x0im_ref, y_ref, xre_ref, xim_ref, uc_scr, tp_scr, zre_scr, zim_scr, sre_scr, sim_scr,
                    xr_scr, xi_scr, *, chunk):
    c = chunk
    nb, tt, _ = u_ref.shape
    nct = tt // c
    rows = nct * nb
    j = pl.program_id(1)

    @pl.when(j == 0)
    def _():
        xr_scr[...] = x0re_ref[...]
        xi_scr[...] = x0im_ref[...]
        for d in range(c // 2):
            tp_scr[d, :LANES, :LANES] = tiles_ref[0, 2 * d]
            tp_scr[d, :LANES, LANES:] = tiles_ref[0, 2 * d + 1]
            tp_scr[d, LANES:, LANES:] = tiles_ref[0, 2 * d]
            tp_scr[d, LANES:, :LANES] = tiles_ref[0, 2 * d - 1] if d > 0 else jnp.zeros((LANES, LANES), BF16)

    for t in range(c):
        if nct == 1:
            uc_scr[t] = u_ref[:, t, :]
        else:
            for b in range(nb):
                uc_scr[t, pl.ds(b, nct, stride=nb), :] = u_ref[b, pl.ds(t, nct, stride=c), :]
    u = jnp.concatenate([uc_scr[t].astype(BF16) for t in range(c)], axis=1)
    zre_scr[...] = jnp.dot(u, bsre_ref[0], preferred_element_type=F32)
    zim_scr[...] = jnp.dot(u, bsim_ref[0], preferred_element_type=F32)
    lr, li = lamre_ref[0, 0], lamim_ref[0, 0]

    def step(i, carry):
        xr, xi = carry
        rs = pl.ds(pl.multiple_of(i * nb, nb), nb)
        sre_scr[rs, :] = xr
        sim_scr[rs, :] = xi
        return (lr * xr - li * xi + zre_scr[rs, :], li * xr + lr * xi + zim_scr[rs, :])

    xr, xi = lax.fori_loop(0, nct, step, (xr_scr[...], xi_scr[...]))
    xr_scr[...] = xr
    xi_scr[...] = xi
    sre = sre_scr[...].astype(BF16)
    sim = sim_scr[...].astype(BF16)
    pair = lambda k: slice(2 * k * LANES, (2 * k + 2) * LANES)
    for k in range(c // 2):
        y = (lax.dot_general(sre, csre_ref[0, pair(k), :], _NT, preferred_element_type=F32)
             + lax.dot_general(sim, csim_ref[0, pair(k), :], _NT, preferred_element_type=F32))
        for kin in range(k + 1):
            y = y + jnp.dot(u[:, pair(kin)], tp_scr[k - kin], preferred_element_type=F32)
        y = jax.nn.gelu(y)
        uc_scr[2 * k] = y[:, :LANES]
        uc_scr[2 * k + 1] = y[:, LANES:]
    for t in range(c):
        if nct == 1:
            y_ref[:, t, :] = uc_scr[t]
        else:
            for b in range(nb):
                y_ref[b, pl.ds(t, nct, stride=c), :] = uc_scr[t, pl.ds(b, nct, stride=nb), :]

    @pl.when(j == pl.num_programs(1) - 1)
    def _():
        xre_ref[...] = xr
        xim_ref[...] = xi


def s5_mix(u, x0re, x0im, tables, chunk):
    b, t, e = u.shape
    tiles, bsre, bsim, csre, csim, lamre, lamim = tables
    nblk = e // LANES
    gp = (LANES // S5_GROUP) * S5_STATE
    nct = min(t // chunk, max(1, S5_ROWS // b))
    assert (t // chunk) % nct == 0
    tt = nct * chunk
    rows = nct * b
    kc = chunk * LANES
    last = tiles.shape[1] // chunk - 1
    assert tiles.shape[1] % chunk == 0
    bs_spec = pl.BlockSpec((1, kc, gp), lambda i, j: (i, last, 0))
    cs_spec = pl.BlockSpec((1, kc, gp), lambda i, j: (i, 0, 0))
    lam_spec = pl.BlockSpec((1, 1, 1, gp), lambda i, j: (i, chunk, 0, 0))
    y, xre, xim = pl.pallas_call(
        functools.partial(_s5_main_kernel, chunk=chunk),
        grid=(nblk, t // tt),
        in_specs=[pl.BlockSpec((b, tt, LANES), lambda i, j: (0, j, i)),
                  pl.BlockSpec((1, chunk, LANES, LANES), lambda i, j: (i, 0, 0, 0)),
                  bs_spec, bs_spec, cs_spec, cs_spec, lam_spec, lam_spec,
                  pl.BlockSpec((b, gp), lambda i, j: (0, i)), pl.BlockSpec((b, gp), lambda i, j: (0, i))],
        out_specs=[pl.BlockSpec((b, tt, LANES), lambda i, j: (0, j, i)),
                   pl.BlockSpec((b, gp), lambda i, j: (0, i)), pl.BlockSpec((b, gp), lambda i, j: (0, i))],
        out_shape=[jax.ShapeDtypeStruct((b, t, e), F32),
                   jax.ShapeDtypeStruct((b, S5_GROUPS * S5_STATE), F32),
                   jax.ShapeDtypeStruct((b, S5_GROUPS * S5_STATE), F32)],
        scratch_shapes=[pltpu.VMEM((chunk, rows, LANES), F32), pltpu.VMEM((chunk // 2, 2 * LANES, 2 * LANES), BF16)]
        + [pltpu.VMEM((rows, gp), F32)] * 4 + [pltpu.VMEM((b, gp), F32)] * 2,
        compiler_params=pltpu.CompilerParams(dimension_semantics=("parallel", "arbitrary"),
                                             vmem_limit_bytes=VMEM_LIMIT),
        name="s5_main",
    )(u, tiles, bsre, bsim, csre, csim, lamre, lamim, x0re.reshape(b, -1), x0im.reshape(b, -1))
    return y, xre.reshape(b, S5_GROUPS, S5_STATE), xim.reshape(b, S5_GROUPS, S5_STATE)


def _trunk(x, st_shift, st_wkv, st_gla, st_re, st_im, w):
    b, t, d = x.shape
    m = b * t
    x2 = x.reshape(m, d)

    p_r, p_g, p_xa = norm_proj(x2, w['norm_mix_pre'][0], [w['w_in_rwkv'], w['w_in_gla'], w['w_in_xa']],
                               [BF16, BF16, BF16])
    o_r, wkv = rwkv_mix(p_r.reshape(b, t, -1), st_shift[0], st_wkv[0], w['rwkv'])
    o_g, gla = gla_mix(p_g.reshape(b, t, -1), p_xa.reshape(b, t, -1), st_gla[0], w['gla_wa2'], w['gla_ba'],
                       w['gla_norm'])
    shift = p_r.reshape(b, t, -1)[:, -1].astype(F32)
    x2 = proj_post([o_r.reshape(m, -1), o_g.reshape(m, -1)], [w['w_out_rwkv'], w['w_out_gla']],
                   w['norm_mix_post'][0], x2)
    x2 = ffn(x2, w['norm_ffn_pre'][0], w['w_ff_up'][0], w['w_ff_down'][0], w['norm_ffn_post'][0])

    (u,) = norm_proj(x2, w['norm_mix_pre'][1], [w['s5_w_in']], [F32])
    chunk = min(S5_CHUNK, t)
    y, s_re, s_im = s5_mix(u.reshape(b, t, d), st_re[0], st_im[0], w['s5_tables'], chunk)
    x2 = proj_post([y.reshape(m, d)], [w['s5_w_out']], w['norm_mix_post'][1], x2, glu=True)
    x2 = ffn(x2, w['norm_ffn_pre'][1], w['w_ff_up'][1], w['w_ff_down'][1], w['norm_ffn_post'][1])
    return x2.reshape(b, t, d), shift[None], wkv[None], gla[None], s_re[None], s_im[None]


def kernel(x_prompt, x_sample, state_rwkv_shift, state_rwkv_wkv, state_gla, state_s5_re, state_s5_im,
           norm_mix_pre, norm_mix_post, norm_ffn_pre, norm_ffn_post,
           w_mix_in, w_mix_out, rwkv_mu, rwkv_w0, rwkv_w2, rwkv_a0, rwkv_a2, rwkv_g2,
           rwkv_k_k, rwkv_k_a, rwkv_r_k, rwkv_gn_g, rwkv_gn_b, gla_wa2, gla_ba, gla_norm,
           s5_w_in, s5_lam_re, s5_lam_im, s5_log_dt, s5_b_re, s5_b_im, s5_c_re, s5_c_im, s5_d, s5_w_out,
           w_ff_up, w_ff_down):
    assert norm_mix_pre.shape[0] == 2, "two layers: one RWKV-7/GLA layer, one S5 layer"
    bf = lambda a: a.astype(BF16)
    w_in = w_mix_in[0]
    gla0 = RWKV_PROJ
    xa0 = gla0 + 2 * GLA_K + GLA_V
    w_in_gla = jnp.concatenate([w_in[:, gla0:xa0], w_in[:, xa0 + GLA_LORA:]], axis=1)
    w_in_xa = jnp.pad(w_in[:, xa0:xa0 + GLA_LORA], ((0, 0), (0, LANES - GLA_LORA)))
    s5_prm = dict(lam_re=s5_lam_re[0], lam_im=s5_lam_im[0], log_dt=s5_log_dt[0], b_re=s5_b_re[0],
                  b_im=s5_b_im[0], c_re=s5_c_re[0], c_im=s5_c_im[0], d=s5_d[0])
    tables = s5_prep(s5_prm, min(S5_CHUNK, max(x_prompt.shape[1], x_sample.shape[1])))
    w = dict(
        norm_mix_pre=norm_mix_pre, norm_mix_post=norm_mix_post, norm_ffn_pre=norm_ffn_pre,
        norm_ffn_post=norm_ffn_post,
        w_in_rwkv=bf(w_in[:, :RWKV_PROJ]), w_in_gla=bf(w_in_gla), w_in_xa=bf(w_in_xa),
        w_out_rwkv=bf(w_mix_out[0, :RWKV_W]), w_out_gla=bf(w_mix_out[0, RWKV_W:]),
        rwkv=dict(mu=rwkv_mu[0], w0=rwkv_w0[0], w2=rwkv_w2[0], a0=rwkv_a0[0], a2=rwkv_a2[0], g2=rwkv_g2[0],
                  k_k=rwkv_k_k[0], k_a=rwkv_k_a[0], r_k=rwkv_r_k[0], gn_g=rwkv_gn_g[0], gn_b=rwkv_gn_b[0]),
        gla_wa2=bf(jnp.pad(gla_wa2[0], ((0, LANES - GLA_LORA), (0, 0)))), gla_ba=gla_ba[0], gla_norm=gla_norm[0],
        s5_w_in=bf(s5_w_in[0]), s5_w_out=bf(s5_w_out[0]), s5_tables=tables,
        w_ff_up=bf(w_ff_up), w_ff_down=bf(w_ff_down),
    )
    bp = x_prompt.shape[0]
    zeros = lambda s: jnp.zeros((s.shape[0], bp) + s.shape[2:], x_prompt.dtype)
    y_p, sh_p, wkv_p, gla_p, re_p, im_p = _trunk(x_prompt, zeros(state_rwkv_shift), zeros(state_rwkv_wkv),
                                                 zeros(state_gla), zeros(state_s5_re), zeros(state_s5_im), w)
    y_s, sh_s, wkv_s, gla_s, re_s, im_s = _trunk(x_sample, state_rwkv_shift, state_rwkv_wkv, state_gla,
                                                 state_s5_re, state_s5_im, w)
    return (y_p, y_s, sh_p, sh_s, wkv_p, wkv_s, gla_p, gla_s, re_p, re_s, im_p, im_s)
```

```python
import functools
import math

import jax
import jax.numpy as jnp
from jax import lax
from jax.experimental import pallas as pl
from jax.experimental.pallas import tpu as pltpu

F32 = jnp.float32
BF16 = jnp.bfloat16

D_MODEL = 1024
D_FF = 4096
NORM_EPS = 1e-6
RWKV_HEADS = 8
RWKV_HEAD = 64
RWKV_W = RWKV_HEADS * RWKV_HEAD
RWKV_PROJ = 1792
RWKV_CHUNK = 64
RWKV_ROWS = 64
RWKV_STEP_ROWS = 256
GN_EPS = 64e-5
_P_GRAM = 1
_P_INV = 1
_P_SOLVE = 1
_P_MIX = 1
_P_STATE = 1
_P_HEADSUM = 1
GLA_HEADS = 4
GLA_DK = 64
GLA_DV = 128
GLA_K = GLA_HEADS * GLA_DK
GLA_V = GLA_HEADS * GLA_DV
GLA_LORA = 16
GLA_TAU = 16.0
GLA_CHUNK = 32
GLA_MAIN = 2 * GLA_K + 2 * GLA_V
GLA_TILE = 256
GLA_TILE_SEQS = 1
GLA_SMALL_ROWS = 64
S5_GROUPS = 64
S5_GROUP = 16
S5_STATE = 64
S5_CHUNK = 16
S5_ROWS = 512
LANES = 128
VMEM_LIMIT = 48 * 1024 * 1024

_NN = (((1,), (0,)), ((), ()))
_NT = (((1,), (1,)), ((), ()))
_TN = (((0,), (0,)), ((), ()))


def _dg(a, b, dn=_NN):
    return lax.dot_general(a.astype(BF16), b.astype(BF16), dn, preferred_element_type=F32)


def _split2(x):
    hi = x.astype(BF16)
    lo = (x - hi.astype(F32)).astype(BF16)
    return hi, lo


def _split3(x):
    hi = x.astype(BF16)
    r1 = x - hi.astype(F32)
    mid = r1.astype(BF16)
    lo = (r1 - mid.astype(F32)).astype(BF16)
    return hi, mid, lo


def _dg3(a, b, dn=_NN):
    ah, al = _split2(a)
    bh, bl = _split2(b)
    d = lambda x, y: lax.dot_general(x, y, dn, preferred_element_type=F32)
    return d(ah, bh) + (d(ah, bl) + d(al, bh))


def _mp(a, b, dn, passes):
    return _dg(a, b, dn) if passes == 1 else _dg3(a, b, dn)


def _dg_sel(sel, x, dn=_NN, terms=3):
    d = lambda y: lax.dot_general(sel, y, dn, preferred_element_type=F32)
    if terms == 2:
        h, l = _split2(x)
        return d(h) + d(l)
    h, m, l = _split3(x)
    return d(h) + (d(m) + d(l))


def _dg_selr(x, sel, dn=_NN):
    d = lambda y: lax.dot_general(y, sel, dn, preferred_element_type=F32)
    h, m, l = _split3(x)
    return d(h) + (d(m) + d(l))


def _rms(x, g):
    return x * lax.rsqrt(jnp.mean(x * x, axis=-1, keepdims=True) + NORM_EPS) * g


def _row_tile(m, want):
    t = min(m, want)
    assert m % t == 0, (m, t)
    return t


def _full(shape):
    nd = len(shape)
    return pl.BlockSpec(shape, lambda *_: (0,) * nd)


def _token_blocks(b, t, nb, tt):
    if tt == t:
        return (b // nb, nb * t), (1, nb * t), (lambda i, j: (i, 0, 0))
    return (b, t), (nb, tt), (lambda i, j: (i, j, 0))


def _norm_proj_kernel(x_ref, g_ref, *refs):
    n = len(refs) // 2
    h = _rms(x_ref[...], g_ref[...]).astype(BF16)
    for w_ref, o_ref in zip(refs[:n], refs[n:]):
        o_ref[...] = jnp.dot(h, w_ref[...], preferred_element_type=F32).astype(o_ref.dtype)


def norm_proj(x, g, ws, out_dtypes, tm=512):
    m, d = x.shape
    tm = _row_tile(m, tm)
    return pl.pallas_call(
        _norm_proj_kernel,
        grid=(m // tm,),
        in_specs=[pl.BlockSpec((tm, d), lambda i: (i, 0)), _full((1, d))]
        + [_full(w.shape) for w in ws],
        out_specs=[pl.BlockSpec((tm, w.shape[1]), lambda i: (i, 0)) for w in ws],
        out_shape=[jax.ShapeDtypeStruct((m, w.shape[1]), dt) for w, dt in zip(ws, out_dtypes)],
        compiler_params=pltpu.CompilerParams(dimension_semantics=("parallel",), vmem_limit_bytes=VMEM_LIMIT),
        name="norm_proj",
    )(x, g.reshape(1, d), *ws)


def _proj_post_kernel(*refs, n_in, glu):
    a_refs = refs[:n_in]
    w_refs = refs[n_in:2 * n_in]
    g_ref, x_ref, o_ref = refs[2 * n_in:]
    m = None
    for a_ref, w_ref in zip(a_refs, w_refs):
        t = jnp.dot(a_ref[...].astype(BF16), w_ref[...], preferred_element_type=F32)
        m = t if m is None else m + t
    if glu:
        m = m[:, :D_MODEL] * jax.nn.sigmoid(m[:, D_MODEL:])
    o_ref[...] = x_ref[...] + _rms(m, g_ref[...])


def proj_post(a_list, w_list, g, x, glu=False, tm=512):
    m, d = x.shape
    tm = _row_tile(m, tm)
    n_in = len(a_list)
    return pl.pallas_call(
        functools.partial(_proj_post_kernel, n_in=n_in, glu=glu),
        grid=(m // tm,),
        in_specs=[pl.BlockSpec((tm, a.shape[1]), lambda i: (i, 0)) for a in a_list]
        + [_full(w.shape) for w in w_list]
        + [_full((1, d)), pl.BlockSpec((tm, d), lambda i: (i, 0))],
        out_specs=pl.BlockSpec((tm, d), lambda i: (i, 0)),
        out_shape=jax.ShapeDtypeStruct((m, d), F32),
        compiler_params=pltpu.CompilerParams(dimension_semantics=("parallel",), vmem_limit_bytes=VMEM_LIMIT),
        name="proj_post",
    )(*a_list, *w_list, g.reshape(1, d), x)


def _ffn_kernel(x_ref, gpre_ref, wup_ref, wdn_ref, gpost_ref, o_ref, h_scr, acc_scr):
    j = pl.program_id(1)

    @pl.when(j == 0)
    def _():
        h_scr[...] = _rms(x_ref[...], gpre_ref[...]).astype(BF16)
        acc_scr[...] = jnp.zeros_like(acc_scr)

    u = jnp.dot(h_scr[...], wup_ref[...], preferred_element_type=F32)
    u = jnp.square(jnp.maximum(u, 0.0)).astype(BF16)
    acc_scr[...] += jnp.dot(u, wdn_ref[...], preferred_element_type=F32)

    @pl.when(j == pl.num_programs(1) - 1)
    def _():
        o_ref[...] = x_ref[...] + _rms(acc_scr[...], gpost_ref[...])


def ffn(x, g_pre, w_up, w_down, g_post, tm=1024, tf=1024):
    m, d = x.shape
    f = w_up.shape[1]
    tm = _row_tile(m, tm)
    return pl.pallas_call(
        _ffn_kernel,
        grid=(m // tm, f // tf),
        in_specs=[
            pl.BlockSpec((tm, d), lambda i, j: (i, 0)),
            _full((1, d)),
            pl.BlockSpec((d, tf), lambda i, j: (0, j)),
            pl.BlockSpec((tf, d), lambda i, j: (j, 0)),
            _full((1, d)),
        ],
        out_specs=pl.BlockSpec((tm, d), lambda i, j: (i, 0)),
        out_shape=jax.ShapeDtypeStruct((m, d), F32),
        scratch_shapes=[pltpu.VMEM((tm, d), BF16), pltpu.VMEM((tm, d), F32)],
        compiler_params=pltpu.CompilerParams(dimension_semantics=("parallel", "arbitrary"),
                                             vmem_limit_bytes=VMEM_LIMIT),
        name="ffn",
    )(x, g_pre.reshape(1, d), w_up, w_down, g_post.reshape(1, d))


def _head_sum(x, ones_bd):
    d = lambda y: jnp.dot(y, ones_bd, preferred_element_type=F32)
    if _P_HEADSUM == 1:
        return d(x.astype(BF16))
    hi, lo = _split2(x)
    return d(hi) + d(lo)


def _rwkv_kernel(p_ref, sh0_ref, s0_ref, mu_ref, w0_ref, w2_ref, a0_ref, a2_ref, g2_ref, kk_ref, ka_ref,
                 rk_ref, gng_ref, gnb_ref, ones_ref, o_ref, sout_ref, state_scr, carry_scr, y_scr, *, chunk):
    c = chunk
    nb = s0_ref.shape[0]
    rows = nb * c
    shift = int(math.log2(c))
    n = pl.program_id(1)

    @pl.when(n == 0)
    def _():
        carry_scr[...] = sh0_ref[...]
        for i in range(nb):
            for h in range(RWKV_HEADS):
                state_scr[i, h // 2, :, (h % 2) * RWKV_HEAD:(h % 2 + 1) * RWKV_HEAD] = s0_ref[i, h]

    p3 = p_ref[...]
    p = p3.astype(F32).reshape(rows, RWKV_PROJ)
    t_idx = jnp.bitwise_and(lax.broadcasted_iota(jnp.int32, (rows, 1), 0), c - 1)
    carry = jnp.broadcast_to(carry_scr[...], (nb, c, RWKV_PROJ)).reshape(rows, RWKV_PROJ)
    prev = jnp.where(t_idx == 0, carry, pltpu.roll(p, 1, 0))
    if p3.shape[0] == nb:
        carry_scr[...] = p3[:, c - 1:c, :].astype(F32)
    z = p + (prev - p) * mu_ref[...]

    w_ = RWKV_W
    r = z[:, 0:w_]
    k = z[:, w_:2 * w_]
    v = z[:, 2 * w_:3 * w_]
    xw = z[:, 3 * w_:3 * w_ + 64]
    xa = z[:, 3 * w_ + 64:3 * w_ + 128]
    xg = z[:, 3 * w_ + 128:3 * w_ + 256]

    wlog = -jax.nn.softplus(-(w0_ref[...] + _dg(jnp.tanh(xw), w2_ref[...]))) - 0.5
    logd = -jnp.exp(wlog)
    a = jax.nn.sigmoid(a0_ref[...] + _dg(xa, a2_ref[...]))
    g = _dg(jax.nn.sigmoid(xg), g2_ref[...])

    ones_bd = ones_ref[...]
    kk = k * kk_ref[...]
    kk = kk * lax.rsqrt(jnp.maximum(_head_sum(kk * kk, ones_bd), 1e-24))
    k2 = k * (1.0 + (a - 1.0) * ka_ref[...])
    bb = kk * a

    def masks(m):
        ri = lax.broadcasted_iota(jnp.int32, (m, m), 0)
        ci = lax.broadcasted_iota(jnp.int32, (m, m), 1)
        same = lax.shift_right_logical(ri, shift) == lax.shift_right_logical(ci, shift)
        return same, jnp.logical_and(same, ri > ci), jnp.logical_and(same, ri >= ci), (ri == ci).astype(F32)

    same_all, _, lower_eq_all, _ = masks(rows)
    cum = _dg_sel(lower_eq_all.astype(BF16), logd, terms=2)
    tot = _dg_sel(same_all.astype(BF16), logd, terms=2)
    g_in = jnp.exp(cum)
    g_ex = jnp.exp(cum - logd)
    g_inv = jnp.exp(-cum)
    g_hat = jnp.exp(tot - cum)
    g_all = jnp.exp(tot)
    rt = r * g_in
    kt = kk * g_ex
    ktil = k2 * g_inv
    btil = bb * g_inv
    khat = k2 * g_hat
    bhat = bb * g_hat

    gr = RWKV_ROWS
    assert rows % gr == 0 and gr == RWKV_HEAD and gr % c == 0
    spg = gr // c
    hp = 2 * RWKV_HEAD
    ri = lax.broadcasted_iota(jnp.int32, (gr, hp), 0)
    si = jnp.bitwise_and(lax.broadcasted_iota(jnp.int32, (gr, hp), 1), RWKV_HEAD - 1)
    same = lax.shift_right_logical(ri, shift) == lax.shift_right_logical(si, shift)
    lower = jnp.logical_and(same, ri > si)
    lower_eq = jnp.logical_and(same, ri >= si)
    eye = (ri == si).astype(F32)
    bi = lax.broadcasted_iota(jnp.int32, (hp, hp), 0)
    bj = lax.broadcasted_iota(jnp.int32, (hp, hp), 1)
    on_diag = lax.shift_right_logical(bi, 6) == lax.shift_right_logical(bj, 6)
    first_head = lax.broadcasted_iota(jnp.int32, (RWKV_HEAD, hp), 1) < RWKV_HEAD

    def bd(x2):
        x2 = x2.astype(BF16)
        return jnp.where(on_diag, jnp.concatenate([x2, x2], axis=0), jnp.zeros((), BF16))

    probs = [(g_, q) for g_ in range(rows // gr) for q in range(RWKV_HEADS // 2)]
    np_ = len(probs)
    psl = lambda q: slice(q * hp, (q + 1) * hp)
    gsl = lambda g_: slice(g_ * gr, (g_ + 1) * gr)
    pick = lambda x, g_, q: x[gsl(g_), psl(q)]
    seq_rows = lambda g_, i: slice(g_ * gr + i * c, g_ * gr + (i + 1) * c)
    cat = lambda xs: jnp.concatenate(xs, axis=0) if len(xs) > 1 else xs[0]
    kr = [jnp.concatenate([pick(kt, *q), pick(rt, *q)], axis=0) for q in probs]
    gb = [_dg(kr[j], bd(pick(btil, *q)), _NT) for j, q in enumerate(probs)]
    gk = [_dg(kr[j], bd(pick(ktil, *q)), _NT) for j, q in enumerate(probs)]
    a_b = [jnp.where(lower, x[:gr], 0.0) for x in gb]
    a_rb = [jnp.where(lower_eq, x[gr:], 0.0) for x in gb]
    a_kk = [jnp.concatenate([jnp.where(lower, x[:gr], 0.0), jnp.where(lower_eq, x[gr:], 0.0)], axis=0)
            for x in gk]
    ks = [[_dg(jnp.concatenate([kt[seq_rows(g_, i), psl(q)], rt[seq_rows(g_, i), psl(q)]], axis=0),
               bd(state_scr[g_ * spg + i, q]), _NT) for i in range(spg)] for g_, q in probs]
    x0 = [cat([x[:c] for x in ks[j]]) for j in range(np_)]
    r0 = [cat([x[c:] for x in ks[j]]) for j in range(np_)]
    av = [_dg(a_kk[j], bd(pick(v, *q))) for j, q in enumerate(probs)]
    tinv = [eye - x for x in a_b]
    pw = [_dg(x, bd(x)) for x in a_b]
    span = 2
    while span < c:
        span *= 2
        if span < c:
            pr = [_dg(jnp.concatenate([t_, p_], axis=0), bd(p_)) for t_, p_ in zip(tinv, pw)]
            tinv = [t_ + x[:gr] for t_, x in zip(tinv, pr)]
            pw = [x[gr:] for x in pr]
        else:
            tinv = [t_ + _dg(t_, bd(p_)) for t_, p_ in zip(tinv, pw)]
    u = [_dg(tinv[j], bd(x0[j] + av[j][:gr])) for j in range(np_)]
    for j, (g_, q) in enumerate(probs):
        y_scr[gsl(g_), psl(q)] = r0[j] + av[j][gr:] - _dg(a_rb[j], bd(u[j]))
    for j, (g_, q) in enumerate(probs):
        for i in range(spg):
            rs = seq_rows(g_, i)
            vu = jnp.concatenate([v[rs, psl(q)], -u[j][i * c:(i + 1) * c]], axis=0)
            kb = jnp.concatenate([khat[rs, psl(q)], bhat[rs, psl(q)]], axis=0)
            full = _dg(vu, kb, _TN)
            upd = jnp.where(first_head, full[:RWKV_HEAD], full[RWKV_HEAD:])
            s_i = g_ * spg + i
            state_scr[s_i, q] = state_scr[s_i, q] * g_all[rs.start:rs.start + 1, psl(q)] + upd

    y = y_scr[...]
    inv_n = 1.0 / RWKV_HEAD
    mean = _head_sum(y, ones_bd) * inv_n
    yc = y - mean
    var = _head_sum(yc * yc, ones_bd) * inv_n
    yn = yc * lax.rsqrt(var + GN_EPS) * gng_ref[...] + gnb_ref[...]
    bonus = _head_sum(r * k2 * rk_ref[...], ones_bd) * v
    o_ref[...] = ((yn + bonus) * g).astype(o_ref.dtype).reshape(o_ref.shape)

    @pl.when(n == pl.num_programs(1) - 1)
    def _():
        for i in range(nb):
            for h in range(RWKV_HEADS):
                sout_ref[i, h] = state_scr[i, h // 2, :, (h % 2) * RWKV_HEAD:(h % 2 + 1) * RWKV_HEAD]


def rwkv_mix(p, shift0, s0, prm, layer=0):
    b, t, _ = p.shape
    c = min(RWKV_CHUNK, t)
    assert t % c == 0
    nb = min(b, RWKV_STEP_ROWS // c)
    assert b % nb == 0
    w_ = RWKV_W
    row = lambda x: x.reshape(1, -1)
    ones_bd = jnp.kron(jnp.eye(RWKV_HEADS, dtype=F32), jnp.ones((RWKV_HEAD, RWKV_HEAD), F32)).astype(BF16)
    consts = [row(prm['mu']), row(prm['w0']), prm['w2'].astype(BF16), row(prm['a0']), prm['a2'].astype(BF16),
              prm['g2'].astype(BF16), row(prm['k_k']), row(prm['k_a']), row(prm['r_k']), row(prm['gn_g']),
              row(prm['gn_b']), ones_bd]
    tok_shape, tok_block, tok_index = _token_blocks(b, t, nb, c)
    o, s_out = pl.pallas_call(
        functools.partial(_rwkv_kernel, chunk=c),
        grid=(b // nb, t // c),
        in_specs=[
            pl.BlockSpec(tok_block + (RWKV_PROJ,), tok_index),
            pl.BlockSpec((nb, 1, RWKV_PROJ), lambda i, j: (i, 0, 0)),
            pl.BlockSpec((None, nb, RWKV_HEADS, RWKV_HEAD, RWKV_HEAD), lambda i, j: (layer, i, 0, 0, 0)),
        ] + [_full(x.shape) for x in consts],
        out_specs=[
            pl.BlockSpec(tok_block + (w_,), tok_index),
            pl.BlockSpec((nb, RWKV_HEADS, RWKV_HEAD, RWKV_HEAD), lambda i, j: (i, 0, 0, 0)),
        ],
        out_shape=[jax.ShapeDtypeStruct(tok_shape + (w_,), BF16),
                   jax.ShapeDtypeStruct((b, RWKV_HEADS, RWKV_HEAD, RWKV_HEAD), F32)],
        scratch_shapes=[pltpu.VMEM((nb, RWKV_HEADS // 2, RWKV_HEAD, 2 * RWKV_HEAD), F32),
                        pltpu.VMEM((nb, 1, RWKV_PROJ), F32),
                        pltpu.VMEM((nb * c, w_), F32)],
        compiler_params=pltpu.CompilerParams(dimension_semantics=("parallel", "arbitrary")),
        name="rwkv_mix",
    )(p.reshape(tok_shape + (RWKV_PROJ,)), shift0.reshape(b, 1, RWKV_PROJ), s0, *consts)
    return o.reshape(b, t, w_), s_out


def _gla_kernel(pg_ref, xa_ref, s0_ref, wa2_ref, ba_ref, gn_ref, o_ref, sout_ref, state_scr, o_scr, *, chunk):
    c = chunk
    nb = s0_ref.shape[0]
    rows = pg_ref.shape[0] * pg_ref.shape[1]
    tt = rows // nb
    nc = tt // c
    n = pl.program_id(1)
    heads = range(GLA_HEADS)
    seqs = range(nb)
    ident = lambda m: (lax.broadcasted_iota(jnp.int32, (m, m), 0)
                       == lax.broadcasted_iota(jnp.int32, (m, m), 1)).astype(BF16)

    @pl.when(n == 0)
    def _():
        eye_k = ident(GLA_DK)
        for i in seqs:
            for h in heads:
                state_scr[i, h] = _dg_selr(s0_ref[i, h], eye_k, _TN)

    pg = pg_ref[...].astype(F32).reshape(rows, GLA_MAIN)
    q = pg[:, 0:GLA_K] * (GLA_DK ** -0.5)
    k = pg[:, GLA_K:2 * GLA_K]
    v = pg[:, 2 * GLA_K:2 * GLA_K + GLA_V]
    gz = pg[:, 2 * GLA_K + GLA_V:]
    xa = xa_ref[...].astype(F32).reshape(rows, LANES)
    gk = jax.nn.log_sigmoid(_dg(xa, wa2_ref[...]) + ba_ref[...]) * (1.0 / GLA_TAU)

    shift = int(math.log2(c))
    ri = lax.broadcasted_iota(jnp.int32, (rows, rows), 0)
    ci = lax.broadcasted_iota(jnp.int32, (rows, rows), 1)
    same = lax.shift_right_logical(ri, shift) == lax.shift_right_logical(ci, shift)
    causal = jnp.logical_and(same, ri >= ci)
    bc = _dg_sel(causal.astype(BF16), gk, terms=2)
    ends = [bc[(j + 1) * c - 1:(j + 1) * c, :] for j in range(rows // c)]
    bl = jnp.concatenate([jnp.broadcast_to(e, (c, GLA_K)) for e in ends], axis=0)
    qt = q * jnp.exp(bc)
    kt = k * jnp.exp(-bc)
    ks = k * jnp.exp(bl - bc)

    ksl = [slice(h * GLA_DK, (h + 1) * GLA_DK) for h in heads]
    vsl = [slice(h * GLA_DV, (h + 1) * GLA_DV) for h in heads]
    chunks = [(i, slice(i * tt + j * c, i * tt + (j + 1) * c)) for i in seqs for j in range(nc)]
    att = [jnp.where(causal, _dg(qt[:, ksl[h]], kt[:, ksl[h]], _NT), 0.0) for h in heads]
    o_intra = [_dg(att[h], v[:, vsl[h]]) for h in heads]
    kv = [[_dg(v[rs, vsl[h]], ks[rs, ksl[h]], _TN) for _, rs in chunks] for h in heads]
    st = [[state_scr[i, h] for i in seqs] for h in heads]
    for ci_, (i, rs) in enumerate(chunks):
        for h in heads:
            o_scr[rs, vsl[h]] = o_intra[h][rs] + _dg(qt[rs, ksl[h]], st[h][i], _NT)
            st[h][i] = st[h][i] * jnp.exp(ends[ci_][:, ksl[h]]) + kv[h][ci_]
    for h in heads:
        for i in seqs:
            state_scr[i, h] = st[h][i]

    for h in heads:
        o_h = o_scr[:, vsl[h]]
        o_h = o_h * lax.rsqrt(jnp.mean(o_h * o_h, axis=-1, keepdims=True) + NORM_EPS) * gn_ref[...]
        o_scr[:, vsl[h]] = o_h * jax.nn.silu(gz[:, vsl[h]])
    o_ref[...] = o_scr[...].astype(o_ref.dtype).reshape(o_ref.shape)

    @pl.when(n == pl.num_programs(1) - 1)
    def _():
        eye_v = ident(GLA_DV)
        for i in seqs:
            for h in heads:
                sout_ref[i, h] = _dg_selr(state_scr[i, h], eye_v, _TN)


def gla_mix(pg, pxa, s0, wa2_pad, ba, g_norm):
    b, t, _ = pg.shape
    tt = min(GLA_TILE, t)
    c = min(GLA_CHUNK, t)
    nb = min(b, max(GLA_TILE_SEQS, GLA_SMALL_ROWS // tt))
    assert t % tt == 0 and tt % c == 0 and b % nb == 0
    tok_shape, tok_block, tok_index = _token_blocks(b, t, nb, tt)
    o, s_out = pl.pallas_call(
        functools.partial(_gla_kernel, chunk=c),
        grid=(b // nb, t // tt),
        in_specs=[
            pl.BlockSpec(tok_block + (GLA_MAIN,), tok_index),
            pl.BlockSpec(tok_block + (LANES,), tok_index),
            pl.BlockSpec((nb, GLA_HEADS, GLA_DK, GLA_DV), lambda i, j: (i, 0, 0, 0)),
            _full(wa2_pad.shape), _full((1, GLA_K)), _full((1, GLA_DV)),
        ],
        out_specs=[
            pl.BlockSpec(tok_block + (GLA_V,), tok_index),
            pl.BlockSpec((nb, GLA_HEADS, GLA_DK, GLA_DV), lambda i, j: (i, 0, 0, 0)),
        ],
        out_shape=[jax.ShapeDtypeStruct(tok_shape + (GLA_V,), BF16),
                   jax.ShapeDtypeStruct((b, GLA_HEADS, GLA_DK, GLA_DV), F32)],
        scratch_shapes=[pltpu.VMEM((nb, GLA_HEADS, GLA_DV, GLA_DK), F32), pltpu.VMEM((nb * tt, GLA_V), F32)],
        compiler_params=pltpu.CompilerParams(dimension_semantics=("parallel", "arbitrary")),
        name="gla_mix",
    )(pg.reshape(tok_shape + (GLA_MAIN,)), pxa.reshape(tok_shape + (LANES,)), s0, wa2_pad,
      ba.reshape(1, GLA_K), g_norm.reshape(1, GLA_DV))
    return o.reshape(b, t, GLA_V), s_out


def _cmul(ar, ai, br, bi):
    return ar * br - ai * bi, ar * bi + ai * br


def _s5_prep_kernel(lre_ref, lim_ref, ldt_ref, cre_ref, cim_ref, btre_ref, btim_ref, d_ref,
                    tiles_ref, bsre_ref, bsim_ref, csre_ref, csim_ref, lamre_ref, lamim_ref, *, chunk):
    c = chunk
    p_ = S5_STATE
    gb = LANES // S5_GROUP
    tiles_ref[...] = jnp.zeros_like(tiles_ref)
    bsre_ref[...] = jnp.zeros_like(bsre_ref)
    bsim_ref[...] = jnp.zeros_like(bsim_ref)
    csre_ref[...] = jnp.zeros_like(csre_ref)
    csim_ref[...] = jnp.zeros_like(csim_ref)
    r16 = lax.broadcasted_iota(jnp.int32, (S5_GROUP, S5_GROUP), 0)
    c16 = lax.broadcasted_iota(jnp.int32, (S5_GROUP, S5_GROUP), 1)
    for gi in range(gb):
        lr = lre_ref[0, gi:gi + 1, :]
        li = lim_ref[0, gi:gi + 1, :]
        dt = jnp.exp(ldt_ref[0, gi:gi + 1, :])
        mag = jnp.exp(lr * dt)
        ang = li * dt
        abr, abi = mag * jnp.cos(ang), mag * jnp.sin(ang)
        den = lr * lr + li * li
        fr = ((abr - 1.0) * lr + abi * li) / den
        fi = (abi * lr - (abr - 1.0) * li) / den
        pows = [(jnp.ones_like(abr), jnp.zeros_like(abr))]
        for _ in range(c):
            pows.append(_cmul(pows[-1][0], pows[-1][1], abr, abi))
        cre, cim = cre_ref[0, gi], cim_ref[0, gi]
        btre, btim = btre_ref[0, gi], btim_ref[0, gi]
        q_re, q_im = [], []
        col = slice(gi * p_, (gi + 1) * p_)
        ch = slice(gi * S5_GROUP, (gi + 1) * S5_GROUP)
        for t in range(c):
            er, ei = _cmul(pows[t][0], pows[t][1], fr, fi)
            qr, qi = _cmul(cre, cim, er, ei)
            q_re.append(qr)
            q_im.append(qi)
            er, ei = _cmul(pows[c - 1 - t][0], pows[c - 1 - t][1], fr, fi)
            zr, zi = _cmul(btre, btim, er, ei)
            rows = slice(t * LANES + gi * S5_GROUP, t * LANES + (gi + 1) * S5_GROUP)
            bsre_ref[0, rows, col] = zr.astype(BF16)
            bsim_ref[0, rows, col] = zi.astype(BF16)
            kr, ki = pows[t + 1]
            csre_ref[0, rows, col] = (cre * kr - cim * ki).astype(BF16)
            csim_ref[0, rows, col] = (-cre * ki - cim * kr).astype(BF16)
        q_re = jnp.concatenate(q_re, axis=0)
        q_im = jnp.concatenate(q_im, axis=0)
        m = _dg3(btre, q_re, _NT) - _dg3(btim, q_im, _NT)
        for t in range(c):
            blk = m[:, t * S5_GROUP:(t + 1) * S5_GROUP]
            if t == 0:
                blk = blk + jnp.where(r16 == c16, d_ref[0, gi], 0.0)
            tiles_ref[0, t, ch, ch] = blk.astype(BF16)
        for t in range(c + 1):
            lamre_ref[0, t, :, col] = pows[t][0]
            lamim_ref[0, t, :, col] = pows[t][1]


def s5_prep(prm, chunk):
    g, p_ = S5_GROUPS, S5_STATE
    gb = LANES // S5_GROUP
    nblk = g // gb
    kc = chunk * LANES
    grp = lambda x: x.reshape((nblk, gb) + x.shape[1:])
    args = [grp(prm['lam_re']), grp(prm['lam_im']), grp(prm['log_dt'].reshape(g, 1)),
            grp(prm['c_re']), grp(prm['c_im']),
            grp(jnp.swapaxes(prm['b_re'], 1, 2)), grp(jnp.swapaxes(prm['b_im'], 1, 2)),
            grp(prm['d'].reshape(g, S5_GROUP, 1))]
    blk = lambda x: pl.BlockSpec((1,) + x.shape[1:], lambda i: (i,) + (0,) * (x.ndim - 1))
    out_shape = [jax.ShapeDtypeStruct((nblk, chunk, LANES, LANES), BF16)] + \
        [jax.ShapeDtypeStruct((nblk, kc, gb * p_), BF16)] * 4 + \
        [jax.ShapeDtypeStruct((nblk, chunk + 1, 1, gb * p_), F32)] * 2
    return pl.pallas_call(
        functools.partial(_s5_prep_kernel, chunk=chunk),
        grid=(nblk,),
        in_specs=[blk(a) for a in args],
        out_specs=[blk(o) for o in out_shape],
        out_shape=out_shape,
        compiler_params=pltpu.CompilerParams(dimension_semantics=("parallel",)),
        name="s5_prep",
    )(*args)


def _s5_main_kernel(u_ref, tiles_ref, bsre_ref, bsim_ref, csre_ref, csim_ref, lamre_ref, lamim_ref, x0re_ref,
                    x0im_ref, y_ref, xre_ref, xim_ref, uc_scr, tp_scr, zre_scr, zim_scr, sre_scr, sim_scr,
                    xr_scr, xi_scr, *, chunk):
    c = chunk
    nb, tt, _ = u_ref.shape
    nct = tt // c
    rows = nct * nb
    j = pl.program_id(1)

    @pl.when(j == 0)
    def _():
        xr_scr[...] = x0re_ref[...]
        xi_scr[...] = x0im_ref[...]
        for d in range(c // 2):
            tp_scr[d, :LANES, :LANES] = tiles_ref[0, 2 * d]
            tp_scr[d, :LANES, LANES:] = tiles_ref[0, 2 * d + 1]
            tp_scr[d, LANES:, LANES:] = tiles_ref[0, 2 * d]
            tp_scr[d, LANES:, :LANES] = tiles_ref[0, 2 * d - 1] if d > 0 else jnp.zeros((LANES, LANES), BF16)

    for t in range(c):
        if nct == 1:
            uc_scr[t] = u_ref[:, t, :]
        else:
            for b in range(nb):
                uc_scr[t, pl.ds(b, nct, stride=nb), :] = u_ref[b, pl.ds(t, nct, stride=c), :]
    u = jnp.concatenate([uc_scr[t].astype(BF16) for t in range(c)], axis=1)
    zre_scr[...] = jnp.dot(u, bsre_ref[0], preferred_element_type=F32)
    zim_scr[...] = jnp.dot(u, bsim_ref[0], preferred_element_type=F32)
    lr, li = lamre_ref[0, 0], lamim_ref[0, 0]

    pair = lambda k: slice(2 * k * LANES, (2 * k + 2) * LANES)
    for k in range(c // 2):
        y = jnp.dot(u[:, pair(0)], tp_scr[k], preferred_element_type=F32)
        for kin in range(1, k + 1):
            y = y + jnp.dot(u[:, pair(kin)], tp_scr[k - kin], preferred_element_type=F32)
        uc_scr[2 * k] = y[:, :LANES]
        uc_scr[2 * k + 1] = y[:, LANES:]

    xr, xi = xr_scr[...], xi_scr[...]
    for i in range(nct):
        rs = slice(i * nb, (i + 1) * nb)
        sre_scr[rs, :] = xr
        sim_scr[rs, :] = xi
        xr, xi = lr * xr - li * xi + zre_scr[rs, :], li * xr + lr * xi + zim_scr[rs, :]
    xr_scr[...] = xr
    xi_scr[...] = xi
    sre = sre_scr[...].astype(BF16)
    sim = sim_scr[...].astype(BF16)
    for k in range(c // 2):
        y = (lax.dot_general(sre, csre_ref[0, pair(k), :], _NT, preferred_element_type=F32)
             + lax.dot_general(sim, csim_ref[0, pair(k), :], _NT, preferred_element_type=F32))
        uc_scr[2 * k] = jax.nn.gelu(uc_scr[2 * k] + y[:, :LANES])
        uc_scr[2 * k + 1] = jax.nn.gelu(uc_scr[2 * k + 1] + y[:, LANES:])
    for t in range(c):
        if nct == 1:
            y_ref[:, t, :] = uc_scr[t]
        else:
            for b in range(nb):
                y_ref[b, pl.ds(t, nct, stride=c), :] = uc_scr[t, pl.ds(b, nct, stride=nb), :]

    @pl.when(j == pl.num_programs(1) - 1)
    def _():
        xre_ref[...] = xr
        xim_ref[...] = xi


def s5_mix(u, x0re, x0im, tables, chunk):
    b, t, e = u.shape
    tiles, bsre, bsim, csre, csim, lamre, lamim = tables
    nblk = e // LANES
    gp = (LANES // S5_GROUP) * S5_STATE
    nct = min(t // chunk, max(1, S5_ROWS // b))
    assert (t // chunk) % nct == 0
    tt = nct * chunk
    rows = nct * b
    kc = chunk * LANES
    last = tiles.shape[1] // chunk - 1
    assert tiles.shape[1] % chunk == 0
    bs_spec = pl.BlockSpec((1, kc, gp), lambda i, j: (i, last, 0))
    cs_spec = pl.BlockSpec((1, kc, gp), lambda i, j: (i, 0, 0))
    lam_spec = pl.BlockSpec((1, 1, 1, gp), lambda i, j: (i, chunk, 0, 0))
    y, xre, xim = pl.pallas_call(
        functools.partial(_s5_main_kernel, chunk=chunk),
        grid=(nblk, t // tt),
        in_specs=[pl.BlockSpec((b, tt, LANES), lambda i, j: (0, j, i)),
                  pl.BlockSpec((1, chunk, LANES, LANES), lambda i, j: (i, 0, 0, 0)),
                  bs_spec, bs_spec, cs_spec, cs_spec, lam_spec, lam_spec,
                  pl.BlockSpec((b, gp), lambda i, j: (0, i)), pl.BlockSpec((b, gp), lambda i, j: (0, i))],
        out_specs=[pl.BlockSpec((b, tt, LANES), lambda i, j: (0, j, i)),
                   pl.BlockSpec((b, gp), lambda i, j: (0, i)), pl.BlockSpec((b, gp), lambda i, j: (0, i))],
        out_shape=[jax.ShapeDtypeStruct((b, t, e), F32),
                   jax.ShapeDtypeStruct((b, S5_GROUPS * S5_STATE), F32),
                   jax.ShapeDtypeStruct((b, S5_GROUPS * S5_STATE), F32)],
        scratch_shapes=[pltpu.VMEM((chunk, rows, LANES), F32), pltpu.VMEM((chunk // 2, 2 * LANES, 2 * LANES), BF16)]
        + [pltpu.VMEM((rows, gp), F32)] * 4 + [pltpu.VMEM((b, gp), F32)] * 2,
        compiler_params=pltpu.CompilerParams(dimension_semantics=("parallel", "arbitrary"),
                                             vmem_limit_bytes=VMEM_LIMIT),
        name="s5_main",
    )(u, tiles, bsre, bsim, csre, csim, lamre, lamim, x0re.reshape(b, -1), x0im.reshape(b, -1))
    return y, xre.reshape(b, S5_GROUPS, S5_STATE), xim.reshape(b, S5_GROUPS, S5_STATE)


def _trunk(x, st_shift, st_wkv, st_gla, st_re, st_im, w):
    b, t, d = x.shape
    m = b * t
    x2 = x.reshape(m, d)

    p_r, p_g, p_xa = norm_proj(x2, w['norm_mix_pre'][0], [w['w_in_rwkv'], w['w_in_gla'], w['w_in_xa']],
                               [BF16, BF16, BF16])
    o_r, wkv = rwkv_mix(p_r.reshape(b, t, -1), st_shift[0], st_wkv, w['rwkv'])
    o_g, gla = gla_mix(p_g.reshape(b, t, -1), p_xa.reshape(b, t, -1), st_gla[0], w['gla_wa2'], w['gla_ba'],
                       w['gla_norm'])
    shift = p_r.reshape(b, t, -1)[:, -1].astype(F32)
    x2 = proj_post([o_r.reshape(m, -1), o_g.reshape(m, -1)], [w['w_out_rwkv'], w['w_out_gla']],
                   w['norm_mix_post'][0], x2)
    x2 = ffn(x2, w['norm_ffn_pre'][0], w['w_ff_up'][0], w['w_ff_down'][0], w['norm_ffn_post'][0])

    (u,) = norm_proj(x2, w['norm_mix_pre'][1], [w['s5_w_in']], [F32])
    chunk = min(S5_CHUNK, t)
    y, s_re, s_im = s5_mix(u.reshape(b, t, d), st_re[0], st_im[0], w['s5_tables'], chunk)
    x2 = proj_post([y.reshape(m, d)], [w['s5_w_out']], w['norm_mix_post'][1], x2, glu=True)
    x2 = ffn(x2, w['norm_ffn_pre'][1], w['w_ff_up'][1], w['w_ff_down'][1], w['norm_ffn_post'][1])
    return x2.reshape(b, t, d), shift[None], wkv[None], gla[None], s_re[None], s_im[None]


def kernel(x_prompt, x_sample, state_rwkv_shift, state_rwkv_wkv, state_gla, state_s5_re, state_s5_im,
           norm_mix_pre, norm_mix_post, norm_ffn_pre, norm_ffn_post,
           w_mix_in, w_mix_out, rwkv_mu, rwkv_w0, rwkv_w2, rwkv_a0, rwkv_a2, rwkv_g2,
           rwkv_k_k, rwkv_k_a, rwkv_r_k, rwkv_gn_g, rwkv_gn_b, gla_wa2, gla_ba, gla_norm,
           s5_w_in, s5_lam_re, s5_lam_im, s5_log_dt, s5_b_re, s5_b_im, s5_c_re, s5_c_im, s5_d, s5_w_out,
           w_ff_up, w_ff_down):
    assert norm_mix_pre.shape[0] == 2, "two layers: one RWKV-7/GLA layer, one S5 layer"
    bf = lambda a: a.astype(BF16)
    w_in = w_mix_in[0]
    gla0 = RWKV_PROJ
    xa0 = gla0 + 2 * GLA_K + GLA_V
    w_in_gla = jnp.concatenate([w_in[:, gla0:xa0], w_in[:, xa0 + GLA_LORA:]], axis=1)
    w_in_xa = jnp.pad(w_in[:, xa0:xa0 + GLA_LORA], ((0, 0), (0, LANES - GLA_LORA)))
    s5_prm = dict(lam_re=s5_lam_re[0], lam_im=s5_lam_im[0], log_dt=s5_log_dt[0], b_re=s5_b_re[0],
                  b_im=s5_b_im[0], c_re=s5_c_re[0], c_im=s5_c_im[0], d=s5_d[0])
    tables = s5_prep(s5_prm, min(S5_CHUNK, max(x_prompt.shape[1], x_sample.shape[1])))
    w = dict(
        norm_mix_pre=norm_mix_pre, norm_mix_post=norm_mix_post, norm_ffn_pre=norm_ffn_pre,
        norm_ffn_post=norm_ffn_post,
        w_in_rwkv=bf(w_in[:, :RWKV_PROJ]), w_in_gla=bf(w_in_gla), w_in_xa=bf(w_in_xa),
        w_out_rwkv=bf(w_mix_out[0, :RWKV_W]), w_out_gla=bf(w_mix_out[0, RWKV_W:]),
        rwkv=dict(mu=rwkv_mu[0], w0=rwkv_w0[0], w2=rwkv_w2[0], a0=rwkv_a0[0], a2=rwkv_a2[0], g2=rwkv_g2[0],
                  k_k=rwkv_k_k[0], k_a=rwkv_k_a[0], r_k=rwkv_r_k[0], gn_g=rwkv_gn_g[0], gn_b=rwkv_gn_b[0]),
        gla_wa2=bf(jnp.pad(gla_wa2[0], ((0, LANES - GLA_LORA), (0, 0)))), gla_ba=gla_ba[0], gla_norm=gla_norm[0],
        s5_w_in=bf(s5_w_in[0]), s5_w_out=bf(s5_w_out[0]), s5_tables=tables,
        w_ff_up=[bf(w_ff_up[l]) for l in range(2)], w_ff_down=[bf(w_ff_down[l]) for l in range(2)],
    )
    bp = x_prompt.shape[0]
    zeros = lambda s: jnp.zeros((s.shape[0], bp) + s.shape[2:], x_prompt.dtype)
    y_p, sh_p, wkv_p, gla_p, re_p, im_p = _trunk(x_prompt, zeros(state_rwkv_shift), zeros(state_rwkv_wkv),
                                                 zeros(state_gla), zeros(state_s5_re), zeros(state_s5_im), w)
    y_s, sh_s, wkv_s, gla_s, re_s, im_s = _trunk(x_sample, state_rwkv_shift, state_rwkv_wkv, state_gla,
                                                 state_s5_re, state_s5_im, w)
    return (y_p, y_s, sh_p, sh_s, wkv_p, wkv_s, gla_p, gla_s, re_p, re_s, im_p, im_s)
```

```python
import functools
import math

import jax
import jax.numpy as jnp
from jax import lax
from jax.experimental import pallas as pl
from jax.experimental.pallas import tpu as pltpu

F32 = jnp.float32
BF16 = jnp.bfloat16

D_MODEL = 1024
D_FF = 4096
NORM_EPS = 1e-6
RWKV_HEADS = 8
RWKV_HEAD = 64
RWKV_W = RWKV_HEADS * RWKV_HEAD
RWKV_PROJ = 1792
RWKV_CHUNK = 64
RWKV_ROWS = 64
RWKV_STEP_ROWS = 256
GN_EPS = 64e-5
_P_GRAM = 1
_P_INV = 1
_P_SOLVE = 1
_P_MIX = 1
_P_STATE = 1
_P_HEADSUM = 1
GLA_HEADS = 4
GLA_DK = 64
GLA_DV = 128
GLA_K = GLA_HEADS * GLA_DK
GLA_V = GLA_HEADS * GLA_DV
GLA_LORA = 16
GLA_TAU = 16.0
GLA_CHUNK = 32
GLA_MAIN = 2 * GLA_K + 2 * GLA_V
GLA_TILE = 256
GLA_TILE_SEQS = 1
GLA_SMALL_ROWS = 64
S5_GROUPS = 64
S5_GROUP = 16
S5_STATE = 64
S5_CHUNK = 16
S5_ROWS = 512
LANES = 128
DENSE_HALVES = 2
VMEM_LIMIT = 48 * 1024 * 1024
FFN_VMEM_LIMIT = 56 * 1024 * 1024

_NN = (((1,), (0,)), ((), ()))
_NT = (((1,), (1,)), ((), ()))
_TN = (((0,), (0,)), ((), ()))


def _dg(a, b, dn=_NN):
    return lax.dot_general(a.astype(BF16), b.astype(BF16), dn, preferred_element_type=F32)


def _split2(x):
    hi = x.astype(BF16)
    lo = (x - hi.astype(F32)).astype(BF16)
    return hi, lo


def _split3(x):
    hi = x.astype(BF16)
    r1 = x - hi.astype(F32)
    mid = r1.astype(BF16)
    lo = (r1 - mid.astype(F32)).astype(BF16)
    return hi, mid, lo


def _dg3(a, b, dn=_NN):
    ah, al = _split2(a)
    bh, bl = _split2(b)
    d = lambda x, y: lax.dot_general(x, y, dn, preferred_element_type=F32)
    return d(ah, bh) + (d(ah, bl) + d(al, bh))


def _mp(a, b, dn, passes):
    return _dg(a, b, dn) if passes == 1 else _dg3(a, b, dn)


def _dg_sel(sel, x, dn=_NN, terms=3):
    d = lambda y: lax.dot_general(sel, y, dn, preferred_element_type=F32)
    if terms == 2:
        h, l = _split2(x)
        return d(h) + d(l)
    h, m, l = _split3(x)
    return d(h) + (d(m) + d(l))


def _dg_selr(x, sel, dn=_NN):
    d = lambda y: lax.dot_general(y, sel, dn, preferred_element_type=F32)
    h, m, l = _split3(x)
    return d(h) + (d(m) + d(l))


def _rms(x, g):
    return x * lax.rsqrt(jnp.mean(x * x, axis=-1, keepdims=True) + NORM_EPS) * g


def _row_tile(m, want):
    t = min(m, want)
    assert m % t == 0, (m, t)
    return t


def _full(shape):
    nd = len(shape)
    return pl.BlockSpec(shape, lambda *_: (0,) * nd)


def _token_blocks(b, t, nb, tt):
    if tt == t:
        return (b // nb, nb * t), (1, nb * t), (lambda i, j: (i, 0, 0))
    return (b, t), (nb, tt), (lambda i, j: (i, j, 0))


def _halves(tm):
    hm = tm // DENSE_HALVES
    return [slice(i * hm, (i + 1) * hm) for i in range(DENSE_HALVES)]


def _norm_proj_kernel(x_ref, g_ref, *refs):
    n = len(refs) // 2
    for rs in _halves(x_ref.shape[0]):
        h = _rms(x_ref[rs, :], g_ref[...]).astype(BF16)
        for w_ref, o_ref in zip(refs[:n], refs[n:]):
            o_ref[rs, :] = jnp.dot(h, w_ref[...], preferred_element_type=F32).astype(o_ref.dtype)


def norm_proj(x, g, ws, out_dtypes, tm=1024):
    m, d = x.shape
    tm = _row_tile(m, tm)
    once = dict(pipeline_mode=pl.Buffered(1))
    return pl.pallas_call(
        _norm_proj_kernel,
        grid=(m // tm,),
        in_specs=[pl.BlockSpec((tm, d), lambda i: (i, 0)), _full((1, d))]
        + [pl.BlockSpec(w.shape, lambda i: (0, 0), **once) for w in ws],
        out_specs=[pl.BlockSpec((tm, w.shape[1]), lambda i: (i, 0)) for w in ws],
        out_shape=[jax.ShapeDtypeStruct((m, w.shape[1]), dt) for w, dt in zip(ws, out_dtypes)],
        compiler_params=pltpu.CompilerParams(dimension_semantics=("parallel",), vmem_limit_bytes=VMEM_LIMIT),
        name="norm_proj",
    )(x, g.reshape(1, d), *ws)


def _proj_post_kernel(*refs, n_in, glu):
    a_refs = refs[:n_in]
    w_refs = refs[n_in:2 * n_in]
    g_ref, x_ref, o_ref = refs[2 * n_in:]
    for rs in _halves(x_ref.shape[0]):
        m = None
        for a_ref, w_ref in zip(a_refs, w_refs):
            t = jnp.dot(a_ref[rs, :].astype(BF16), w_ref[...], preferred_element_type=F32)
            m = t if m is None else m + t
        if glu:
            m = m[:, :D_MODEL] * jax.nn.sigmoid(m[:, D_MODEL:])
        o_ref[rs, :] = x_ref[rs, :] + _rms(m, g_ref[...])


def proj_post(a_list, w_list, g, x, glu=False, tm=1024):
    m, d = x.shape
    tm = _row_tile(m, tm)
    n_in = len(a_list)
    once = dict(pipeline_mode=pl.Buffered(1))
    return pl.pallas_call(
        functools.partial(_proj_post_kernel, n_in=n_in, glu=glu),
        grid=(m // tm,),
        in_specs=[pl.BlockSpec((tm, a.shape[1]), lambda i: (i, 0)) for a in a_list]
        + [pl.BlockSpec(w.shape, lambda i: (0, 0), **once) for w in w_list]
        + [_full((1, d)), pl.BlockSpec((tm, d), lambda i: (i, 0))],
        out_specs=pl.BlockSpec((tm, d), lambda i: (i, 0)),
        out_shape=jax.ShapeDtypeStruct((m, d), F32),
        compiler_params=pltpu.CompilerParams(dimension_semantics=("parallel",), vmem_limit_bytes=VMEM_LIMIT),
        name="proj_post",
    )(*a_list, *w_list, g.reshape(1, d), x)


def _ffn_kernel(x_ref, gpre_ref, wup_ref, wdn_ref, gpost_ref, o_ref, *, tf):
    f = wup_ref.shape[1]
    for rs in _halves(x_ref.shape[0]):
        x = x_ref[rs, :]
        h = _rms(x, gpre_ref[...]).astype(BF16)
        acc = None
        for j in range(f // tf):
            u = jnp.dot(h, wup_ref[:, j * tf:(j + 1) * tf], preferred_element_type=F32)
            u = jnp.square(jnp.maximum(u, 0.0)).astype(BF16)
            t = jnp.dot(u, wdn_ref[j * tf:(j + 1) * tf, :], preferred_element_type=F32)
            acc = t if acc is None else acc + t
        o_ref[rs, :] = x + _rms(acc, gpost_ref[...])


def ffn(x, g_pre, w_up, w_down, g_post, tm=1024, tf=512):
    m, d = x.shape
    f = w_up.shape[1]
    tm = _row_tile(m, tm)
    once = dict(pipeline_mode=pl.Buffered(1))
    return pl.pallas_call(
        functools.partial(_ffn_kernel, tf=tf),
        grid=(m // tm,),
        in_specs=[
            pl.BlockSpec((tm, d), lambda i: (i, 0)),
            _full((1, d)),
            pl.BlockSpec((d, f), lambda i: (0, 0), **once),
            pl.BlockSpec((f, d), lambda i: (0, 0), **once),
            _full((1, d)),
        ],
        out_specs=pl.BlockSpec((tm, d), lambda i: (i, 0)),
        out_shape=jax.ShapeDtypeStruct((m, d), F32),
        compiler_params=pltpu.CompilerParams(dimension_semantics=("parallel",), vmem_limit_bytes=FFN_VMEM_LIMIT),
        name="ffn",
    )(x, g_pre.reshape(1, d), w_up, w_down, g_post.reshape(1, d))


def _head_sum(x, ones_bd):
    d = lambda y: jnp.dot(y, ones_bd, preferred_element_type=F32)
    if _P_HEADSUM == 1:
        return d(x.astype(BF16))
    hi, lo = _split2(x)
    return d(hi) + d(lo)


def _rwkv_kernel(p_ref, sh0_ref, s0_ref, mu_ref, w0_ref, w2_ref, a0_ref, a2_ref, g2_ref, kk_ref, ka_ref,
                 rk_ref, gng_ref, gnb_ref, ones_ref, o_ref, sout_ref, state_scr, carry_scr, y_scr, *, chunk):
    c = chunk
    nb = s0_ref.shape[0]
    rows = nb * c
    shift = int(math.log2(c))
    n = pl.program_id(1)

    @pl.when(n == 0)
    def _():
        carry_scr[...] = sh0_ref[...]
        for i in range(nb):
            for h in range(RWKV_HEADS):
                state_scr[i, h // 2, :, (h % 2) * RWKV_HEAD:(h % 2 + 1) * RWKV_HEAD] = s0_ref[i, h]

    p3 = p_ref[...]
    p = p3.astype(F32).reshape(rows, RWKV_PROJ)
    t_idx = jnp.bitwise_and(lax.broadcasted_iota(jnp.int32, (rows, 1), 0), c - 1)
    carry = jnp.broadcast_to(carry_scr[...], (nb, c, RWKV_PROJ)).reshape(rows, RWKV_PROJ)
    prev = jnp.where(t_idx == 0, carry, pltpu.roll(p, 1, 0))
    if p3.shape[0] == nb:
        carry_scr[...] = p3[:, c - 1:c, :].astype(F32)
    z = p + (prev - p) * mu_ref[...]

    w_ = RWKV_W
    r = z[:, 0:w_]
    k = z[:, w_:2 * w_]
    v = z[:, 2 * w_:3 * w_]
    xw = z[:, 3 * w_:3 * w_ + 64]
    xa = z[:, 3 * w_ + 64:3 * w_ + 128]
    xg = z[:, 3 * w_ + 128:3 * w_ + 256]

    wlog = -jax.nn.softplus(-(w0_ref[...] + _dg(jnp.tanh(xw), w2_ref[...]))) - 0.5
    logd = -jnp.exp(wlog)
    a = jax.nn.sigmoid(a0_ref[...] + _dg(xa, a2_ref[...]))
    g = _dg(jax.nn.sigmoid(xg), g2_ref[...])

    ones_bd = ones_ref[...]
    kk = k * kk_ref[...]
    kk = kk * lax.rsqrt(jnp.maximum(_head_sum(kk * kk, ones_bd), 1e-24))
    k2 = k * (1.0 + (a - 1.0) * ka_ref[...])
    bb = kk * a

    def masks(m):
        ri = lax.broadcasted_iota(jnp.int32, (m, m), 0)
        ci = lax.broadcasted_iota(jnp.int32, (m, m), 1)
        same = lax.shift_right_logical(ri, shift) == lax.shift_right_logical(ci, shift)
        return same, jnp.logical_and(same, ri > ci), jnp.logical_and(same, ri >= ci), (ri == ci).astype(F32)

    same_all, _, lower_eq_all, _ = masks(rows)
    cum = _dg_sel(lower_eq_all.astype(BF16), logd, terms=2)
    tot = _dg_sel(same_all.astype(BF16), logd, terms=2)
    g_in = jnp.exp(cum)
    g_ex = jnp.exp(cum - logd)
    g_inv = jnp.exp(-cum)
    g_hat = jnp.exp(tot - cum)
    g_all = jnp.exp(tot)
    rt = r * g_in
    kt = kk * g_ex
    ktil = k2 * g_inv
    btil = bb * g_inv
    khat = k2 * g_hat
    bhat = bb * g_hat

    gr = RWKV_ROWS
    assert rows % gr == 0 and gr == RWKV_HEAD and gr % c == 0
    spg = gr // c
    hp = 2 * RWKV_HEAD
    ri = lax.broadcasted_iota(jnp.int32, (gr, hp), 0)
    si = jnp.bitwise_and(lax.broadcasted_iota(jnp.int32, (gr, hp), 1), RWKV_HEAD - 1)
    same = lax.shift_right_logical(ri, shift) == lax.shift_right_logical(si, shift)
    lower = jnp.logical_and(same, ri > si)
    lower_eq = jnp.logical_and(same, ri >= si)
    eye = (ri == si).astype(F32)
    bi = lax.broadcasted_iota(jnp.int32, (hp, hp), 0)
    bj = lax.broadcasted_iota(jnp.int32, (hp, hp), 1)
    on_diag = lax.shift_right_logical(bi, 6) == lax.shift_right_logical(bj, 6)
    first_head = lax.broadcasted_iota(jnp.int32, (RWKV_HEAD, hp), 1) < RWKV_HEAD

    def bd(x2):
        x2 = x2.astype(BF16)
        return jnp.where(on_diag, jnp.concatenate([x2, x2], axis=0), jnp.zeros((), BF16))

    probs = [(g_, q) for g_ in range(rows // gr) for q in range(RWKV_HEADS // 2)]
    np_ = len(probs)
    psl = lambda q: slice(q * hp, (q + 1) * hp)
    gsl = lambda g_: slice(g_ * gr, (g_ + 1) * gr)
    pick = lambda x, g_, q: x[gsl(g_), psl(q)]
    seq_rows = lambda g_, i: slice(g_ * gr + i * c, g_ * gr + (i + 1) * c)
    cat = lambda xs: jnp.concatenate(xs, axis=0) if len(xs) > 1 else xs[0]
    kr = [jnp.concatenate([pick(kt, *q), pick(rt, *q)], axis=0) for q in probs]
    gb = [_dg(kr[j], bd(pick(btil, *q)), _NT) for j, q in enumerate(probs)]
    gk = [_dg(kr[j], bd(pick(ktil, *q)), _NT) for j, q in enumerate(probs)]
    a_b = [jnp.where(lower, x[:gr], 0.0) for x in gb]
    a_rb = [jnp.where(lower_eq, x[gr:], 0.0) for x in gb]
    a_kk = [jnp.concatenate([jnp.where(lower, x[:gr], 0.0), jnp.where(lower_eq, x[gr:], 0.0)], axis=0)
            for x in gk]
    ks = [[_dg(jnp.concatenate([kt[seq_rows(g_, i), psl(q)], rt[seq_rows(g_, i), psl(q)]], axis=0),
               bd(state_scr[g_ * spg + i, q]), _NT) for i in range(spg)] for g_, q in probs]
    x0 = [cat([x[:c] for x in ks[j]]) for j in range(np_)]
    r0 = [cat([x[c:] for x in ks[j]]) for j in range(np_)]
    av = [_dg(a_kk[j], bd(pick(v, *q))) for j, q in enumerate(probs)]
    tinv = [eye - x for x in a_b]
    pw = [_dg(x, bd(x)) for x in a_b]
    span = 2
    while span < c:
        span *= 2
        if span < c:
            pr = [_dg(jnp.concatenate([t_, p_], axis=0), bd(p_)) for t_, p_ in zip(tinv, pw)]
            tinv = [t_ + x[:gr] for t_, x in zip(tinv, pr)]
            pw = [x[gr:] for x in pr]
        else:
            tinv = [t_ + _dg(t_, bd(p_)) for t_, p_ in zip(tinv, pw)]
    u = [_dg(tinv[j], bd(x0[j] + av[j][:gr])) for j in range(np_)]
    for j, (g_, q) in enumerate(probs):
        y_scr[gsl(g_), psl(q)] = r0[j] + av[j][gr:] - _dg(a_rb[j], bd(u[j]))
    for j, (g_, q) in enumerate(probs):
        for i in range(spg):
            rs = seq_rows(g_, i)
            vu = jnp.concatenate([v[rs, psl(q)], -u[j][i * c:(i + 1) * c]], axis=0)
            kb = jnp.concatenate([khat[rs, psl(q)], bhat[rs, psl(q)]], axis=0)
            full = _dg(vu, kb, _TN)
            upd = jnp.where(first_head, full[:RWKV_HEAD], full[RWKV_HEAD:])
            s_i = g_ * spg + i
            state_scr[s_i, q] = state_scr[s_i, q] * g_all[rs.start:rs.start + 1, psl(q)] + upd

    y = y_scr[...]
    inv_n = 1.0 / RWKV_HEAD
    mean = _head_sum(y, ones_bd) * inv_n
    yc = y - mean
    var = _head_sum(yc * yc, ones_bd) * inv_n
    yn = yc * lax.rsqrt(var + GN_EPS) * gng_ref[...] + gnb_ref[...]
    bonus = _head_sum(r * k2 * rk_ref[...], ones_bd) * v
    o_ref[...] = ((yn + bonus) * g).astype(o_ref.dtype).reshape(o_ref.shape)

    @pl.when(n == pl.num_programs(1) - 1)
    def _():
        for i in range(nb):
            for h in range(RWKV_HEADS):
                sout_ref[i, h] = state_scr[i, h // 2, :, (h % 2) * RWKV_HEAD:(h % 2 + 1) * RWKV_HEAD]


def rwkv_mix(p, shift0, s0, prm, layer=0):
    b, t, _ = p.shape
    c = min(RWKV_CHUNK, t)
    assert t % c == 0
    nb = min(b, RWKV_STEP_ROWS // c)
    assert b % nb == 0
    w_ = RWKV_W
    row = lambda x: x.reshape(1, -1)
    ones_bd = jnp.kron(jnp.eye(RWKV_HEADS, dtype=F32), jnp.ones((RWKV_HEAD, RWKV_HEAD), F32)).astype(BF16)
    consts = [row(prm['mu']), row(prm['w0']), prm['w2'].astype(BF16), row(prm['a0']), prm['a2'].astype(BF16),
              prm['g2'].astype(BF16), row(prm['k_k']), row(prm['k_a']), row(prm['r_k']), row(prm['gn_g']),
              row(prm['gn_b']), ones_bd]
    tok_shape, tok_block, tok_index = _token_blocks(b, t, nb, c)
    o, s_out = pl.pallas_call(
        functools.partial(_rwkv_kernel, chunk=c),
        grid=(b // nb, t // c),
        in_specs=[
            pl.BlockSpec(tok_block + (RWKV_PROJ,), tok_index),
            pl.BlockSpec((nb, 1, RWKV_PROJ), lambda i, j: (i, 0, 0)),
            pl.BlockSpec((None, nb, RWKV_HEADS, RWKV_HEAD, RWKV_HEAD), lambda i, j: (layer, i, 0, 0, 0)),
        ] + [_full(x.shape) for x in consts],
        out_specs=[
            pl.BlockSpec(tok_block + (w_,), tok_index),
            pl.BlockSpec((nb, RWKV_HEADS, RWKV_HEAD, RWKV_HEAD), lambda i, j: (i, 0, 0, 0)),
        ],
        out_shape=[jax.ShapeDtypeStruct(tok_shape + (w_,), BF16),
                   jax.ShapeDtypeStruct((b, RWKV_HEADS, RWKV_HEAD, RWKV_HEAD), F32)],
        scratch_shapes=[pltpu.VMEM((nb, RWKV_HEADS // 2, RWKV_HEAD, 2 * RWKV_HEAD), F32),
                        pltpu.VMEM((nb, 1, RWKV_PROJ), F32),
                        pltpu.VMEM((nb * c, w_), F32)],
        compiler_params=pltpu.CompilerParams(dimension_semantics=("parallel", "arbitrary")),
        name="rwkv_mix",
    )(p.reshape(tok_shape + (RWKV_PROJ,)), shift0.reshape(b, 1, RWKV_PROJ), s0, *consts)
    return o.reshape(b, t, w_), s_out


def _gla_kernel(pg_ref, xa_ref, s0_ref, wa2_ref, ba_ref, gn_ref, o_ref, sout_ref, state_scr, o_scr, *, chunk):
    c = chunk
    nb = s0_ref.shape[0]
    rows = pg_ref.shape[0] * pg_ref.shape[1]
    tt = rows // nb
    nc = tt // c
    n = pl.program_id(1)
    heads = range(GLA_HEADS)
    seqs = range(nb)
    ident = lambda m: (lax.broadcasted_iota(jnp.int32, (m, m), 0)
                       == lax.broadcasted_iota(jnp.int32, (m, m), 1)).astype(BF16)

    @pl.when(n == 0)
    def _():
        eye_k = ident(GLA_DK)
        for i in seqs:
            for h in heads:
                state_scr[i, h] = _dg_selr(s0_ref[i, h], eye_k, _TN)

    pg = pg_ref[...].astype(F32).reshape(rows, GLA_MAIN)
    q = pg[:, 0:GLA_K] * (GLA_DK ** -0.5)
    k = pg[:, GLA_K:2 * GLA_K]
    v = pg[:, 2 * GLA_K:2 * GLA_K + GLA_V]
    gz = pg[:, 2 * GLA_K + GLA_V:]
    xa = xa_ref[...].astype(F32).reshape(rows, LANES)
    gk = jax.nn.log_sigmoid(_dg(xa, wa2_ref[...]) + ba_ref[...]) * (1.0 / GLA_TAU)

    shift = int(math.log2(c))
    ri = lax.broadcasted_iota(jnp.int32, (rows, rows), 0)
    ci = lax.broadcasted_iota(jnp.int32, (rows, rows), 1)
    same = lax.shift_right_logical(ri, shift) == lax.shift_right_logical(ci, shift)
    causal = jnp.logical_and(same, ri >= ci)
    bc = _dg_sel(causal.astype(BF16), gk, terms=2)
    ends = [bc[(j + 1) * c - 1:(j + 1) * c, :] for j in range(rows // c)]
    bl = jnp.concatenate([jnp.broadcast_to(e, (c, GLA_K)) for e in ends], axis=0)
    qt = q * jnp.exp(bc)
    kt = k * jnp.exp(-bc)
    ks = k * jnp.exp(bl - bc)

    ksl = [slice(h * GLA_DK, (h + 1) * GLA_DK) for h in heads]
    vsl = [slice(h * GLA_DV, (h + 1) * GLA_DV) for h in heads]
    chunks = [(i, slice(i * tt + j * c, i * tt + (j + 1) * c)) for i in seqs for j in range(nc)]
    att = [jnp.where(causal, _dg(qt[:, ksl[h]], kt[:, ksl[h]], _NT), 0.0) for h in heads]
    o_intra = [_dg(att[h], v[:, vsl[h]]) for h in heads]
    kv = [[_dg(v[rs, vsl[h]], ks[rs, ksl[h]], _TN) for _, rs in chunks] for h in heads]
    st = [[state_scr[i, h] for i in seqs] for h in heads]
    for ci_, (i, rs) in enumerate(chunks):
        for h in heads:
            o_scr[rs, vsl[h]] = o_intra[h][rs] + _dg(qt[rs, ksl[h]], st[h][i], _NT)
            st[h][i] = st[h][i] * jnp.exp(ends[ci_][:, ksl[h]]) + kv[h][ci_]
    for h in heads:
        for i in seqs:
            state_scr[i, h] = st[h][i]

    for h in heads:
        o_h = o_scr[:, vsl[h]]
        o_h = o_h * lax.rsqrt(jnp.mean(o_h * o_h, axis=-1, keepdims=True) + NORM_EPS) * gn_ref[...]
        o_scr[:, vsl[h]] = o_h * jax.nn.silu(gz[:, vsl[h]])
    o_ref[...] = o_scr[...].astype(o_ref.dtype).reshape(o_ref.shape)

    @pl.when(n == pl.num_programs(1) - 1)
    def _():
        eye_v = ident(GLA_DV)
        for i in seqs:
            for h in heads:
                sout_ref[i, h] = _dg_selr(state_scr[i, h], eye_v, _TN)


def gla_mix(pg, pxa, s0, wa2_pad, ba, g_norm):
    b, t, _ = pg.shape
    tt = min(GLA_TILE, t)
    c = min(GLA_CHUNK, t)
    nb = min(b, max(GLA_TILE_SEQS, GLA_SMALL_ROWS // tt))
    assert t % tt == 0 and tt % c == 0 and b % nb == 0
    tok_shape, tok_block, tok_index = _token_blocks(b, t, nb, tt)
    o, s_out = pl.pallas_call(
        functools.partial(_gla_kernel, chunk=c),
        grid=(b // nb, t // tt),
        in_specs=[
            pl.BlockSpec(tok_block + (GLA_MAIN,), tok_index),
            pl.BlockSpec(tok_block + (LANES,), tok_index),
            pl.BlockSpec((nb, GLA_HEADS, GLA_DK, GLA_DV), lambda i, j: (i, 0, 0, 0)),
            _full(wa2_pad.shape), _full((1, GLA_K)), _full((1, GLA_DV)),
        ],
        out_specs=[
            pl.BlockSpec(tok_block + (GLA_V,), tok_index),
            pl.BlockSpec((nb, GLA_HEADS, GLA_DK, GLA_DV), lambda i, j: (i, 0, 0, 0)),
        ],
        out_shape=[jax.ShapeDtypeStruct(tok_shape + (GLA_V,), BF16),
                   jax.ShapeDtypeStruct((b, GLA_HEADS, GLA_DK, GLA_DV), F32)],
        scratch_shapes=[pltpu.VMEM((nb, GLA_HEADS, GLA_DV, GLA_DK), F32), pltpu.VMEM((nb * tt, GLA_V), F32)],
        compiler_params=pltpu.CompilerParams(dimension_semantics=("parallel", "arbitrary")),
        name="gla_mix",
    )(pg.reshape(tok_shape + (GLA_MAIN,)), pxa.reshape(tok_shape + (LANES,)), s0, wa2_pad,
      ba.reshape(1, GLA_K), g_norm.reshape(1, GLA_DV))
    return o.reshape(b, t, GLA_V), s_out


def _cmul(ar, ai, br, bi):
    return ar * br - ai * bi, ar * bi + ai * br


def _s5_prep_kernel(lre_ref, lim_ref, ldt_ref, cre_ref, cim_ref, btre_ref, btim_ref, d_ref,
                    tiles_ref, bsre_ref, bsim_ref, csre_ref, csim_ref, lamre_ref, lamim_ref, *, chunk):
    c = chunk
    p_ = S5_STATE
    gb = LANES // S5_GROUP
    tiles_ref[...] = jnp.zeros_like(tiles_ref)
    bsre_ref[...] = jnp.zeros_like(bsre_ref)
    bsim_ref[...] = jnp.zeros_like(bsim_ref)
    csre_ref[...] = jnp.zeros_like(csre_ref)
    csim_ref[...] = jnp.zeros_like(csim_ref)
    r16 = lax.broadcasted_iota(jnp.int32, (S5_GROUP, S5_GROUP), 0)
    c16 = lax.broadcasted_iota(jnp.int32, (S5_GROUP, S5_GROUP), 1)
    for gi in range(gb):
        lr = lre_ref[0, gi:gi + 1, :]
        li = lim_ref[0, gi:gi + 1, :]
        dt = jnp.exp(ldt_ref[0, gi:gi + 1, :])
        mag = jnp.exp(lr * dt)
        ang = li * dt
        abr, abi = mag * jnp.cos(ang), mag * jnp.sin(ang)
        den = lr * lr + li * li
        fr = ((abr - 1.0) * lr + abi * li) / den
        fi = (abi * lr - (abr - 1.0) * li) / den
        pows = [(jnp.ones_like(abr), jnp.zeros_like(abr))]
        for _ in range(c):
            pows.append(_cmul(pows[-1][0], pows[-1][1], abr, abi))
        cre, cim = cre_ref[0, gi], cim_ref[0, gi]
        btre, btim = btre_ref[0, gi], btim_ref[0, gi]
        q_re, q_im = [], []
        col = slice(gi * p_, (gi + 1) * p_)
        ch = slice(gi * S5_GROUP, (gi + 1) * S5_GROUP)
        for t in range(c):
            er, ei = _cmul(pows[t][0], pows[t][1], fr, fi)
            qr, qi = _cmul(cre, cim, er, ei)
            q_re.append(qr)
            q_im.append(qi)
            er, ei = _cmul(pows[c - 1 - t][0], pows[c - 1 - t][1], fr, fi)
            zr, zi = _cmul(btre, btim, er, ei)
            rows = slice(t * LANES + gi * S5_GROUP, t * LANES + (gi + 1) * S5_GROUP)
            bsre_ref[0, rows, col] = zr.astype(BF16)
            bsim_ref[0, rows, col] = zi.astype(BF16)
            kr, ki = pows[t + 1]
            csre_ref[0, rows, col] = (cre * kr - cim * ki).astype(BF16)
            csim_ref[0, rows, col] = (-cre * ki - cim * kr).astype(BF16)
        q_re = jnp.concatenate(q_re, axis=0)
        q_im = jnp.concatenate(q_im, axis=0)
        m = _dg3(btre, q_re, _NT) - _dg3(btim, q_im, _NT)
        for t in range(c):
            blk = m[:, t * S5_GROUP:(t + 1) * S5_GROUP]
            if t == 0:
                blk = blk + jnp.where(r16 == c16, d_ref[0, gi], 0.0)
            tiles_ref[0, t, ch, ch] = blk.astype(BF16)
        for t in range(c + 1):
            lamre_ref[0, t, :, col] = pows[t][0]
            lamim_ref[0, t, :, col] = pows[t][1]


def s5_prep(prm, chunk):
    g, p_ = S5_GROUPS, S5_STATE
    gb = LANES // S5_GROUP
    nblk = g // gb
    kc = chunk * LANES
    grp = lambda x: x.reshape((nblk, gb) + x.shape[1:])
    args = [grp(prm['lam_re']), grp(prm['lam_im']), grp(prm['log_dt'].reshape(g, 1)),
            grp(prm['c_re']), grp(prm['c_im']),
            grp(jnp.swapaxes(prm['b_re'], 1, 2)), grp(jnp.swapaxes(prm['b_im'], 1, 2)),
            grp(prm['d'].reshape(g, S5_GROUP, 1))]
    blk = lambda x: pl.BlockSpec((1,) + x.shape[1:], lambda i: (i,) + (0,) * (x.ndim - 1))
    out_shape = [jax.ShapeDtypeStruct((nblk, chunk, LANES, LANES), BF16)] + \
        [jax.ShapeDtypeStruct((nblk, kc, gb * p_), BF16)] * 4 + \
        [jax.ShapeDtypeStruct((nblk, chunk + 1, 1, gb * p_), F32)] * 2
    return pl.pallas_call(
        functools.partial(_s5_prep_kernel, chunk=chunk),
        grid=(nblk,),
        in_specs=[blk(a) for a in args],
        out_specs=[blk(o) for o in out_shape],
        out_shape=out_shape,
        compiler_params=pltpu.CompilerParams(dimension_semantics=("parallel",)),
        name="s5_prep",
    )(*args)


def _s5_main_kernel(u_ref, tiles_ref, bsre_ref, bsim_ref, csre_ref, csim_ref, lamre_ref, lamim_ref, x0re_ref,
                    x0im_ref, y_ref, xre_ref, xim_ref, uc_scr, tp_scr, zre_scr, zim_scr, sre_scr, sim_scr,
                    xr_scr, xi_scr, *, chunk):
    c = chunk
    nb, tt, _ = u_ref.shape
    nct = tt // c
    rows = nct * nb
    j = pl.program_id(1)

    @pl.when(j == 0)
    def _():
        xr_scr[...] = x0re_ref[...]
        xi_scr[...] = x0im_ref[...]
        for d in range(c // 2):
            tp_scr[d, :LANES, :LANES] = tiles_ref[0, 2 * d]
            tp_scr[d, :LANES, LANES:] = tiles_ref[0, 2 * d + 1]
            tp_scr[d, LANES:, LANES:] = tiles_ref[0, 2 * d]
            tp_scr[d, LANES:, :LANES] = tiles_ref[0, 2 * d - 1] if d > 0 else jnp.zeros((LANES, LANES), BF16)

    for t in range(c):
        if nct == 1:
            uc_scr[t] = u_ref[:, t, :]
        else:
            for b in range(nb):
                uc_scr[t, pl.ds(b, nct, stride=nb), :] = u_ref[b, pl.ds(t, nct, stride=c), :]
    u = jnp.concatenate([uc_scr[t].astype(BF16) for t in range(c)], axis=1)
    zre_scr[...] = jnp.dot(u, bsre_ref[0], preferred_element_type=F32)
    zim_scr[...] = jnp.dot(u, bsim_ref[0], preferred_element_type=F32)
    lr, li = lamre_ref[0, 0], lamim_ref[0, 0]

    pair = lambda k: slice(2 * k * LANES, (2 * k + 2) * LANES)
    for k in range(c // 2):
        y = jnp.dot(u[:, pair(0)], tp_scr[k], preferred_element_type=F32)
        for kin in range(1, k + 1):
            y = y + jnp.dot(u[:, pair(kin)], tp_scr[k - kin], preferred_element_type=F32)
        uc_scr[2 * k] = y[:, :LANES]
        uc_scr[2 * k + 1] = y[:, LANES:]

    xr, xi = xr_scr[...], xi_scr[...]
    for i in range(nct):
        rs = slice(i * nb, (i + 1) * nb)
        sre_scr[rs, :] = xr
        sim_scr[rs, :] = xi
        xr, xi = lr * xr - li * xi + zre_scr[rs, :], li * xr + lr * xi + zim_scr[rs, :]
    xr_scr[...] = xr
    xi_scr[...] = xi
    sre = sre_scr[...].astype(BF16)
    sim = sim_scr[...].astype(BF16)
    for k in range(c // 2):
        y = (lax.dot_general(sre, csre_ref[0, pair(k), :], _NT, preferred_element_type=F32)
             + lax.dot_general(sim, csim_ref[0, pair(k), :], _NT, preferred_element_type=F32))
        uc_scr[2 * k] = jax.nn.gelu(uc_scr[2 * k] + y[:, :LANES])
        uc_scr[2 * k + 1] = jax.nn.gelu(uc_scr[2 * k + 1] + y[:, LANES:])
    for t in range(c):
        if nct == 1:
            y_ref[:, t, :] = uc_scr[t]
        else:
            for b in range(nb):
                y_ref[b, pl.ds(t, nct, stride=c), :] = uc_scr[t, pl.ds(b, nct, stride=nb), :]

    @pl.when(j == pl.num_programs(1) - 1)
    def _():
        xre_ref[...] = xr
        xim_ref[...] = xi


def s5_mix(u, x0re, x0im, tables, chunk):
    b, t, e = u.shape
    tiles, bsre, bsim, csre, csim, lamre, lamim = tables
    nblk = e // LANES
    gp = (LANES // S5_GROUP) * S5_STATE
    nct = min(t // chunk, max(1, S5_ROWS // b))
    assert (t // chunk) % nct == 0
    tt = nct * chunk
    rows = nct * b
    kc = chunk * LANES
    last = tiles.shape[1] // chunk - 1
    assert tiles.shape[1] % chunk == 0
    bs_spec = pl.BlockSpec((1, kc, gp), lambda i, j: (i, last, 0))
    cs_spec = pl.BlockSpec((1, kc, gp), lambda i, j: (i, 0, 0))
    lam_spec = pl.BlockSpec((1, 1, 1, gp), lambda i, j: (i, chunk, 0, 0))
    y, xre, xim = pl.pallas_call(
        functools.partial(_s5_main_kernel, chunk=chunk),
        grid=(nblk, t // tt),
        in_specs=[pl.BlockSpec((b, tt, LANES), lambda i, j: (0, j, i)),
                  pl.BlockSpec((1, chunk, LANES, LANES), lambda i, j: (i, 0, 0, 0)),
                  bs_spec, bs_spec, cs_spec, cs_spec, lam_spec, lam_spec,
                  pl.BlockSpec((b, gp), lambda i, j: (0, i)), pl.BlockSpec((b, gp), lambda i, j: (0, i))],
        out_specs=[pl.BlockSpec((b, tt, LANES), lambda i, j: (0, j, i)),
                   pl.BlockSpec((b, gp), lambda i, j: (0, i)), pl.BlockSpec((b, gp), lambda i, j: (0, i))],
        out_shape=[jax.ShapeDtypeStruct((b, t, e), F32),
                   jax.ShapeDtypeStruct((b, S5_GROUPS * S5_STATE), F32),
                   jax.ShapeDtypeStruct((b, S5_GROUPS * S5_STATE), F32)],
        scratch_shapes=[pltpu.VMEM((chunk, rows, LANES), F32), pltpu.VMEM((chunk // 2, 2 * LANES, 2 * LANES), BF16)]
        + [pltpu.VMEM((rows, gp), F32)] * 4 + [pltpu.VMEM((b, gp), F32)] * 2,
        compiler_params=pltpu.CompilerParams(dimension_semantics=("parallel", "arbitrary"),
                                             vmem_limit_bytes=VMEM_LIMIT),
        name="s5_main",
    )(u, tiles, bsre, bsim, csre, csim, lamre, lamim, x0re.reshape(b, -1), x0im.reshape(b, -1))
    return y, xre.reshape(b, S5_GROUPS, S5_STATE), xim.reshape(b, S5_GROUPS, S5_STATE)


def _trunk(x, st_shift, st_wkv, st_gla, st_re, st_im, w):
    b, t, d = x.shape
    m = b * t
    x2 = x.reshape(m, d)

    p_r, p_g, p_xa = norm_proj(x2, w['norm_mix_pre'][0], [w['w_in_rwkv'], w['w_in_gla'], w['w_in_xa']],
                               [BF16, BF16, BF16])
    o_r, wkv = rwkv_mix(p_r.reshape(b, t, -1), st_shift[0], st_wkv, w['rwkv'])
    o_g, gla = gla_mix(p_g.reshape(b, t, -1), p_xa.reshape(b, t, -1), st_gla[0], w['gla_wa2'], w['gla_ba'],
                       w['gla_norm'])
    shift = p_r.reshape(b, t, -1)[:, -1].astype(F32)
    x2 = proj_post([o_r.reshape(m, -1), o_g.reshape(m, -1)], [w['w_out_rwkv'], w['w_out_gla']],
                   w['norm_mix_post'][0], x2)
    x2 = ffn(x2, w['norm_ffn_pre'][0], w['w_ff_up'][0], w['w_ff_down'][0], w['norm_ffn_post'][0])

    (u,) = norm_proj(x2, w['norm_mix_pre'][1], [w['s5_w_in']], [F32])
    chunk = min(S5_CHUNK, t)
    y, s_re, s_im = s5_mix(u.reshape(b, t, d), st_re[0], st_im[0], w['s5_tables'], chunk)
    x2 = proj_post([y.reshape(m, d)], [w['s5_w_out']], w['norm_mix_post'][1], x2, glu=True)
    x2 = ffn(x2, w['norm_ffn_pre'][1], w['w_ff_up'][1], w['w_ff_down'][1], w['norm_ffn_post'][1])
    return x2.reshape(b, t, d), shift[None], wkv[None], gla[None], s_re[None], s_im[None]


def kernel(x_prompt, x_sample, state_rwkv_shift, state_rwkv_wkv, state_gla, state_s5_re, state_s5_im,
           norm_mix_pre, norm_mix_post, norm_ffn_pre, norm_ffn_post,
           w_mix_in, w_mix_out, rwkv_mu, rwkv_w0, rwkv_w2, rwkv_a0, rwkv_a2, rwkv_g2,
           rwkv_k_k, rwkv_k_a, rwkv_r_k, rwkv_gn_g, rwkv_gn_b, gla_wa2, gla_ba, gla_norm,
           s5_w_in, s5_lam_re, s5_lam_im, s5_log_dt, s5_b_re, s5_b_im, s5_c_re, s5_c_im, s5_d, s5_w_out,
           w_ff_up, w_ff_down):
    assert norm_mix_pre.shape[0] == 2, "two layers: one RWKV-7/GLA layer, one S5 layer"
    bf = lambda a: a.astype(BF16)
    w_in = w_mix_in[0]
    gla0 = RWKV_PROJ
    xa0 = gla0 + 2 * GLA_K + GLA_V
    w_in_gla = jnp.concatenate([w_in[:, gla0:xa0], w_in[:, xa0 + GLA_LORA:]], axis=1)
    w_in_xa = jnp.pad(w_in[:, xa0:xa0 + GLA_LORA], ((0, 0), (0, LANES - GLA_LORA)))
    s5_prm = dict(lam_re=s5_lam_re[0], lam_im=s5_lam_im[0], log_dt=s5_log_dt[0], b_re=s5_b_re[0],
                  b_im=s5_b_im[0], c_re=s5_c_re[0], c_im=s5_c_im[0], d=s5_d[0])
    tables = s5_prep(s5_prm, min(S5_CHUNK, max(x_prompt.shape[1], x_sample.shape[1])))
    w = dict(
        norm_mix_pre=norm_mix_pre, norm_mix_post=norm_mix_post, norm_ffn_pre=norm_ffn_pre,
        norm_ffn_post=norm_ffn_post,
        w_in_rwkv=bf(w_in[:, :RWKV_PROJ]), w_in_gla=bf(w_in_gla), w_in_xa=bf(w_in_xa),
        w_out_rwkv=bf(w_mix_out[0, :RWKV_W]), w_out_gla=bf(w_mix_out[0, RWKV_W:]),
        rwkv=dict(mu=rwkv_mu[0], w0=rwkv_w0[0], w2=rwkv_w2[0], a0=rwkv_a0[0], a2=rwkv_a2[0], g2=rwkv_g2[0],
                  k_k=rwkv_k_k[0], k_a=rwkv_k_a[0], r_k=rwkv_r_k[0], gn_g=rwkv_gn_g[0], gn_b=rwkv_gn_b[0]),
        gla_wa2=bf(jnp.pad(gla_wa2[0], ((0, LANES - GLA_LORA), (0, 0)))), gla_ba=gla_ba[0], gla_norm=gla_norm[0],
        s5_w_in=bf(s5_w_in[0]), s5_w_out=bf(s5_w_out[0]), s5_tables=tables,
        w_ff_up=[bf(w_ff_up[l]) for l in range(2)], w_ff_down=[bf(w_ff_down[l]) for l in range(2)],
    )
    bp = x_prompt.shape[0]
    zeros = lambda s: jnp.zeros((s.shape[0], bp) + s.shape[2:], x_prompt.dtype)
    y_p, sh_p, wkv_p, gla_p, re_p, im_p = _trunk(x_prompt, zeros(state_rwkv_shift), zeros(state_rwkv_wkv),
                                                 zeros(state_gla), zeros(state_s5_re), zeros(state_s5_im), w)
    y_s, sh_s, wkv_s, gla_s, re_s, im_s = _trunk(x_sample, state_rwkv_shift, state_rwkv_wkv, state_gla,
                                                 state_s5_re, state_s5_im, w)
    return (y_p, y_s, sh_p, sh_s, wkv_p, wkv_s, gla_p, gla_s, re_p, re_s, im_p, im_s)
```

```python
import functools
import math

import jax
import jax.numpy as jnp
from jax import lax
from jax.experimental import pallas as pl
from jax.experimental.pallas import tpu as pltpu

F32 = jnp.float32
BF16 = jnp.bfloat16

D_MODEL = 1024
NORM_EPS = 1e-6
RWKV_HEADS = 8
RWKV_HEAD = 64
RWKV_W = RWKV_HEADS * RWKV_HEAD
RWKV_PROJ = 1792
RWKV_CHUNK = 64
RWKV_ROWS = 64
RWKV_STEP_ROWS = 256
GN_EPS = 64e-5
GLA_HEADS = 4
GLA_DK = 64
GLA_DV = 128
GLA_K = GLA_HEADS * GLA_DK
GLA_V = GLA_HEADS * GLA_DV
GLA_LORA = 16
GLA_TAU = 16.0
GLA_CHUNK = 32
GLA_MAIN = 2 * GLA_K + 2 * GLA_V
GLA_TILE = 256
GLA_SMALL_ROWS = 64
S5_GROUPS = 64
S5_GROUP = 16
S5_STATE = 64
S5_CHUNK = 16
S5_ROWS = 512
S5_PIECES = 2
LANES = 128
DENSE_HALVES = 2
VMEM_LIMIT = 48 * 1024 * 1024
FFN_VMEM_LIMIT = 56 * 1024 * 1024

_NN = (((1,), (0,)), ((), ()))
_NT = (((1,), (1,)), ((), ()))
_TN = (((0,), (0,)), ((), ()))


def _dg(a, b, dn=_NN):
    return lax.dot_general(a.astype(BF16), b.astype(BF16), dn, preferred_element_type=F32)


def _split2(x):
    hi = x.astype(BF16)
    lo = (x - hi.astype(F32)).astype(BF16)
    return hi, lo


def _split3(x):
    hi = x.astype(BF16)
    r1 = x - hi.astype(F32)
    mid = r1.astype(BF16)
    lo = (r1 - mid.astype(F32)).astype(BF16)
    return hi, mid, lo


def _dg3(a, b, dn=_NN):
    ah, al = _split2(a)
    bh, bl = _split2(b)
    d = lambda x, y: lax.dot_general(x, y, dn, preferred_element_type=F32)
    return d(ah, bh) + (d(ah, bl) + d(al, bh))


def _dg_sel(sel, x, dn=_NN, terms=3):
    d = lambda y: lax.dot_general(sel, y, dn, preferred_element_type=F32)
    if terms == 2:
        h, l = _split2(x)
        return d(h) + d(l)
    h, m, l = _split3(x)
    return d(h) + (d(m) + d(l))


def _dg_selr(x, sel, dn=_NN):
    d = lambda y: lax.dot_general(y, sel, dn, preferred_element_type=F32)
    h, m, l = _split3(x)
    return d(h) + (d(m) + d(l))


def _rms(x, g):
    return x * lax.rsqrt(jnp.mean(x * x, axis=-1, keepdims=True) + NORM_EPS) * g


def _row_tile(m, want):
    t = min(m, want)
    assert m % t == 0, (m, t)
    return t


def _full(shape):
    nd = len(shape)
    return pl.BlockSpec(shape, lambda *_: (0,) * nd)


def _token_blocks(b, t, nb, tt):
    if tt == t:
        return (b // nb, nb * t), (1, nb * t), (lambda i, j: (i, 0, 0))
    return (b, t), (nb, tt), (lambda i, j: (i, j, 0))


def _halves(tm):
    hm = tm // DENSE_HALVES
    return [slice(i * hm, (i + 1) * hm) for i in range(DENSE_HALVES)]


def _norm_proj_kernel(x_ref, g_ref, *refs):
    n = len(refs) // 2
    for rs in _halves(x_ref.shape[0]):
        h = _rms(x_ref[rs, :], g_ref[...]).astype(BF16)
        for w_ref, o_ref in zip(refs[:n], refs[n:]):
            o_ref[rs, :] = jnp.dot(h, w_ref[...], preferred_element_type=F32).astype(o_ref.dtype)


def norm_proj(x, g, ws, out_dtypes, tm=1024):
    m, d = x.shape
    tm = _row_tile(m, tm)
    once = dict(pipeline_mode=pl.Buffered(1))
    return pl.pallas_call(
        _norm_proj_kernel,
        grid=(m // tm,),
        in_specs=[pl.BlockSpec((tm, d), lambda i: (i, 0)), _full((1, d))]
        + [pl.BlockSpec(w.shape, lambda i: (0, 0), **once) for w in ws],
        out_specs=[pl.BlockSpec((tm, w.shape[1]), lambda i: (i, 0)) for w in ws],
        out_shape=[jax.ShapeDtypeStruct((m, w.shape[1]), dt) for w, dt in zip(ws, out_dtypes)],
        compiler_params=pltpu.CompilerParams(dimension_semantics=("parallel",), vmem_limit_bytes=VMEM_LIMIT),
        name="norm_proj",
    )(x, g.reshape(1, d), *ws)


def _proj_post_kernel(*refs, n_in, glu):
    a_refs = refs[:n_in]
    w_refs = refs[n_in:2 * n_in]
    g_ref, x_ref, o_ref = refs[2 * n_in:]
    for rs in _halves(x_ref.shape[0]):
        m = None
        for a_ref, w_ref in zip(a_refs, w_refs):
            t = jnp.dot(a_ref[rs, :].astype(BF16), w_ref[...], preferred_element_type=F32)
            m = t if m is None else m + t
        if glu:
            m = m[:, :D_MODEL] * jax.nn.sigmoid(m[:, D_MODEL:])
        o_ref[rs, :] = x_ref[rs, :] + _rms(m, g_ref[...])


def proj_post(a_list, w_list, g, x, glu=False, tm=1024):
    m, d = x.shape
    tm = _row_tile(m, tm)
    n_in = len(a_list)
    once = dict(pipeline_mode=pl.Buffered(1))
    return pl.pallas_call(
        functools.partial(_proj_post_kernel, n_in=n_in, glu=glu),
        grid=(m // tm,),
        in_specs=[pl.BlockSpec((tm, a.shape[1]), lambda i: (i, 0)) for a in a_list]
        + [pl.BlockSpec(w.shape, lambda i: (0, 0), **once) for w in w_list]
        + [_full((1, d)), pl.BlockSpec((tm, d), lambda i: (i, 0))],
        out_specs=pl.BlockSpec((tm, d), lambda i: (i, 0)),
        out_shape=jax.ShapeDtypeStruct((m, d), F32),
        compiler_params=pltpu.CompilerParams(dimension_semantics=("parallel",), vmem_limit_bytes=VMEM_LIMIT),
        name="proj_post",
    )(*a_list, *w_list, g.reshape(1, d), x)


def _ffn_kernel(x_ref, gpre_ref, wup_ref, wdn_ref, gpost_ref, o_ref, *, tf):
    f = wup_ref.shape[1]
    for rs in _halves(x_ref.shape[0]):
        x = x_ref[rs, :]
        h = _rms(x, gpre_ref[...]).astype(BF16)
        acc = None
        for j in range(f // tf):
            u = jnp.dot(h, wup_ref[:, j * tf:(j + 1) * tf], preferred_element_type=F32)
            u = jnp.square(jnp.maximum(u, 0.0)).astype(BF16)
            t = jnp.dot(u, wdn_ref[j * tf:(j + 1) * tf, :], preferred_element_type=F32)
            acc = t if acc is None else acc + t
        o_ref[rs, :] = x + _rms(acc, gpost_ref[...])


def ffn(x, g_pre, w_up, w_down, g_post, tm=1024, tf=512):
    m, d = x.shape
    f = w_up.shape[1]
    tm = _row_tile(m, tm)
    once = dict(pipeline_mode=pl.Buffered(1))
    return pl.pallas_call(
        functools.partial(_ffn_kernel, tf=tf),
        grid=(m // tm,),
        in_specs=[
            pl.BlockSpec((tm, d), lambda i: (i, 0)),
            _full((1, d)),
            pl.BlockSpec((d, f), lambda i: (0, 0), **once),
            pl.BlockSpec((f, d), lambda i: (0, 0), **once),
            _full((1, d)),
        ],
        out_specs=pl.BlockSpec((tm, d), lambda i: (i, 0)),
        out_shape=jax.ShapeDtypeStruct((m, d), F32),
        compiler_params=pltpu.CompilerParams(dimension_semantics=("parallel",), vmem_limit_bytes=FFN_VMEM_LIMIT),
        name="ffn",
    )(x, g_pre.reshape(1, d), w_up, w_down, g_post.reshape(1, d))


def _head_sum(x, ones_bd):
    return jnp.dot(x.astype(BF16), ones_bd, preferred_element_type=F32)


def _rwkv_kernel(p_ref, sh0_ref, s0_ref, mu_ref, w0_ref, w2_ref, a0_ref, a2_ref, g2_ref, kk_ref, ka_ref,
                 rk_ref, gng_ref, gnb_ref, ones_ref, o_ref, sout_ref, state_scr, carry_scr, y_scr, *, chunk):
    c = chunk
    nb = s0_ref.shape[0]
    rows = nb * c
    shift = int(math.log2(c))
    n = pl.program_id(1)

    @pl.when(n == 0)
    def _():
        carry_scr[...] = sh0_ref[...]
        for i in range(nb):
            for h in range(RWKV_HEADS):
                state_scr[i, h // 2, :, (h % 2) * RWKV_HEAD:(h % 2 + 1) * RWKV_HEAD] = s0_ref[i, h]

    p3 = p_ref[...]
    p = p3.astype(F32).reshape(rows, RWKV_PROJ)
    t_idx = jnp.bitwise_and(lax.broadcasted_iota(jnp.int32, (rows, 1), 0), c - 1)
    carry = jnp.broadcast_to(carry_scr[...], (nb, c, RWKV_PROJ)).reshape(rows, RWKV_PROJ)
    prev = jnp.where(t_idx == 0, carry, pltpu.roll(p, 1, 0))
    if p3.shape[0] == nb:
        carry_scr[...] = p3[:, c - 1:c, :].astype(F32)
    z = p + (prev - p) * mu_ref[...]

    w_ = RWKV_W
    r = z[:, 0:w_]
    k = z[:, w_:2 * w_]
    v = z[:, 2 * w_:3 * w_]
    lw, la = w2_ref.shape[0], a2_ref.shape[0]
    xw = z[:, 3 * w_:3 * w_ + lw]
    xa = z[:, 3 * w_ + lw:3 * w_ + lw + la]
    xg = z[:, 3 * w_ + lw + la:]

    wlog = -jax.nn.softplus(-(w0_ref[...] + _dg(jnp.tanh(xw), w2_ref[...]))) - 0.5
    logd = -jnp.exp(wlog)
    a = jax.nn.sigmoid(a0_ref[...] + _dg(xa, a2_ref[...]))
    g = _dg(jax.nn.sigmoid(xg), g2_ref[...])

    ones_bd = ones_ref[...]
    kk = k * kk_ref[...]
    kk = kk * lax.rsqrt(jnp.maximum(_head_sum(kk * kk, ones_bd), 1e-24))
    k2 = k * (1.0 + (a - 1.0) * ka_ref[...])
    bb = kk * a

    ra = lax.broadcasted_iota(jnp.int32, (rows, rows), 0)
    ca = lax.broadcasted_iota(jnp.int32, (rows, rows), 1)
    same_all = lax.shift_right_logical(ra, shift) == lax.shift_right_logical(ca, shift)
    lower_eq_all = jnp.logical_and(same_all, ra >= ca)
    cum = _dg_sel(lower_eq_all.astype(BF16), logd, terms=2)
    tot = _dg_sel(same_all.astype(BF16), logd, terms=2)
    g_in = jnp.exp(cum)
    g_ex = jnp.exp(cum - logd)
    g_inv = jnp.exp(-cum)
    g_hat = jnp.exp(tot - cum)
    g_all = jnp.exp(tot)
    rt = r * g_in
    kt = kk * g_ex
    ktil = k2 * g_inv
    btil = bb * g_inv
    khat = k2 * g_hat
    bhat = bb * g_hat

    gr = RWKV_ROWS
    assert rows % gr == 0 and gr == RWKV_HEAD and gr % c == 0
    spg = gr // c
    hp = 2 * RWKV_HEAD
    ri = lax.broadcasted_iota(jnp.int32, (gr, hp), 0)
    si = jnp.bitwise_and(lax.broadcasted_iota(jnp.int32, (gr, hp), 1), RWKV_HEAD - 1)
    same = lax.shift_right_logical(ri, shift) == lax.shift_right_logical(si, shift)
    lower = jnp.logical_and(same, ri > si)
    lower_eq = jnp.logical_and(same, ri >= si)
    eye = (ri == si).astype(F32)
    bi = lax.broadcasted_iota(jnp.int32, (hp, hp), 0)
    bj = lax.broadcasted_iota(jnp.int32, (hp, hp), 1)
    head_bits = int(math.log2(RWKV_HEAD))
    on_diag = lax.shift_right_logical(bi, head_bits) == lax.shift_right_logical(bj, head_bits)
    first_head = lax.broadcasted_iota(jnp.int32, (RWKV_HEAD, hp), 1) < RWKV_HEAD

    def bd(x2):
        x2 = x2.astype(BF16)
        return jnp.where(on_diag, jnp.concatenate([x2, x2], axis=0), jnp.zeros((), BF16))

    probs = [(g_, q) for g_ in range(rows // gr) for q in range(RWKV_HEADS // 2)]
    np_ = len(probs)
    psl = lambda q: slice(q * hp, (q + 1) * hp)
    gsl = lambda g_: slice(g_ * gr, (g_ + 1) * gr)
    pick = lambda x, g_, q: x[gsl(g_), psl(q)]
    seq_rows = lambda g_, i: slice(g_ * gr + i * c, g_ * gr + (i + 1) * c)
    cat = lambda xs: jnp.concatenate(xs, axis=0) if len(xs) > 1 else xs[0]
    kr = [jnp.concatenate([pick(kt, *q), pick(rt, *q)], axis=0) for q in probs]
    gb = [_dg(kr[j], bd(pick(btil, *q)), _NT) for j, q in enumerate(probs)]
    gk = [_dg(kr[j], bd(pick(ktil, *q)), _NT) for j, q in enumerate(probs)]
    a_b = [jnp.where(lower, x[:gr], 0.0) for x in gb]
    a_rb = [jnp.where(lower_eq, x[gr:], 0.0) for x in gb]
    a_kk = [jnp.concatenate([jnp.where(lower, x[:gr], 0.0), jnp.where(lower_eq, x[gr:], 0.0)], axis=0)
            for x in gk]
    ks = [[_dg(jnp.concatenate([kt[seq_rows(g_, i), psl(q)], rt[seq_rows(g_, i), psl(q)]], axis=0),
               bd(state_scr[g_ * spg + i, q]), _NT) for i in range(spg)] for g_, q in probs]
    x0 = [cat([x[:c] for x in ks[j]]) for j in range(np_)]
    r0 = [cat([x[c:] for x in ks[j]]) for j in range(np_)]
    av = [_dg(a_kk[j], bd(pick(v, *q))) for j, q in enumerate(probs)]
    tinv = [eye - x for x in a_b]
    pw = [_dg(x, bd(x)) for x in a_b]
    span = 2
    while span < c:
        span *= 2
        if span < c:
            pr = [_dg(jnp.concatenate([t_, p_], axis=0), bd(p_)) for t_, p_ in zip(tinv, pw)]
            tinv = [t_ + x[:gr] for t_, x in zip(tinv, pr)]
            pw = [x[gr:] for x in pr]
        else:
            tinv = [t_ + _dg(t_, bd(p_)) for t_, p_ in zip(tinv, pw)]
    u = [_dg(tinv[j], bd(x0[j] + av[j][:gr])) for j in range(np_)]
    for j, (g_, q) in enumerate(probs):
        y_scr[gsl(g_), psl(q)] = r0[j] + av[j][gr:] - _dg(a_rb[j], bd(u[j]))
    for j, (g_, q) in enumerate(probs):
        for i in range(spg):
            rs = seq_rows(g_, i)
            vu = jnp.concatenate([v[rs, psl(q)], -u[j][i * c:(i + 1) * c]], axis=0)
            kb = jnp.concatenate([khat[rs, psl(q)], bhat[rs, psl(q)]], axis=0)
            full = _dg(vu, kb, _TN)
            upd = jnp.where(first_head, full[:RWKV_HEAD], full[RWKV_HEAD:])
            s_i = g_ * spg + i
            state_scr[s_i, q] = state_scr[s_i, q] * g_all[rs.start:rs.start + 1, psl(q)] + upd

    y = y_scr[...]
    inv_n = 1.0 / RWKV_HEAD
    mean = _head_sum(y, ones_bd) * inv_n
    yc = y - mean
    var = _head_sum(yc * yc, ones_bd) * inv_n
    yn = yc * lax.rsqrt(var + GN_EPS) * gng_ref[...] + gnb_ref[...]
    bonus = _head_sum(r * k2 * rk_ref[...], ones_bd) * v
    o_ref[...] = ((yn + bonus) * g).astype(o_ref.dtype).reshape(o_ref.shape)

    @pl.when(n == pl.num_programs(1) - 1)
    def _():
        for i in range(nb):
            for h in range(RWKV_HEADS):
                sout_ref[i, h] = state_scr[i, h // 2, :, (h % 2) * RWKV_HEAD:(h % 2 + 1) * RWKV_HEAD]


def rwkv_mix(p, shift0, s0, prm, layer=0):
    b, t, _ = p.shape
    c = min(RWKV_CHUNK, t)
    assert t % c == 0
    nb = min(b, RWKV_STEP_ROWS // c)
    assert b % nb == 0
    w_ = RWKV_W
    row = lambda x: x.reshape(1, -1)
    ones_bd = jnp.kron(jnp.eye(RWKV_HEADS, dtype=F32), jnp.ones((RWKV_HEAD, RWKV_HEAD), F32)).astype(BF16)
    consts = [row(prm['mu']), row(prm['w0']), prm['w2'].astype(BF16), row(prm['a0']), prm['a2'].astype(BF16),
              prm['g2'].astype(BF16), row(prm['k_k']), row(prm['k_a']), row(prm['r_k']), row(prm['gn_g']),
              row(prm['gn_b']), ones_bd]
    tok_shape, tok_block, tok_index = _token_blocks(b, t, nb, c)
    o, s_out = pl.pallas_call(
        functools.partial(_rwkv_kernel, chunk=c),
        grid=(b // nb, t // c),
        in_specs=[
            pl.BlockSpec(tok_block + (RWKV_PROJ,), tok_index),
            pl.BlockSpec((nb, 1, RWKV_PROJ), lambda i, j: (i, 0, 0)),
            pl.BlockSpec((None, nb, RWKV_HEADS, RWKV_HEAD, RWKV_HEAD), lambda i, j: (layer, i, 0, 0, 0)),
        ] + [_full(x.shape) for x in consts],
        out_specs=[
            pl.BlockSpec(tok_block + (w_,), tok_index),
            pl.BlockSpec((nb, RWKV_HEADS, RWKV_HEAD, RWKV_HEAD), lambda i, j: (i, 0, 0, 0)),
        ],
        out_shape=[jax.ShapeDtypeStruct(tok_shape + (w_,), BF16),
                   jax.ShapeDtypeStruct((b, RWKV_HEADS, RWKV_HEAD, RWKV_HEAD), F32)],
        scratch_shapes=[pltpu.VMEM((nb, RWKV_HEADS // 2, RWKV_HEAD, 2 * RWKV_HEAD), F32),
                        pltpu.VMEM((nb, 1, RWKV_PROJ), F32),
                        pltpu.VMEM((nb * c, w_), F32)],
        compiler_params=pltpu.CompilerParams(dimension_semantics=("parallel", "arbitrary")),
        name="rwkv_mix",
    )(p.reshape(tok_shape + (RWKV_PROJ,)), shift0.reshape(b, 1, RWKV_PROJ), s0, *consts)
    return o.reshape(b, t, w_), s_out


def _gla_kernel(pg_ref, xa_ref, s0_ref, wa2_ref, ba_ref, gn_ref, o_ref, sout_ref, state_scr, o_scr, *, chunk):
    c = chunk
    nb = s0_ref.shape[0]
    rows = pg_ref.shape[0] * pg_ref.shape[1]
    tt = rows // nb
    nc = tt // c
    n = pl.program_id(1)
    heads = range(GLA_HEADS)
    seqs = range(nb)
    ident = lambda m: (lax.broadcasted_iota(jnp.int32, (m, m), 0)
                       == lax.broadcasted_iota(jnp.int32, (m, m), 1)).astype(BF16)

    @pl.when(n == 0)
    def _():
        eye_k = ident(GLA_DK)
        for i in seqs:
            for h in heads:
                state_scr[i, h] = _dg_selr(s0_ref[i, h], eye_k, _TN)

    pg = pg_ref[...].astype(F32).reshape(rows, GLA_MAIN)
    q = pg[:, 0:GLA_K] * (GLA_DK ** -0.5)
    k = pg[:, GLA_K:2 * GLA_K]
    v = pg[:, 2 * GLA_K:2 * GLA_K + GLA_V]
    gz = pg[:, 2 * GLA_K + GLA_V:]
    xa = xa_ref[...].astype(F32).reshape(rows, LANES)
    gk = jax.nn.log_sigmoid(_dg(xa, wa2_ref[...]) + ba_ref[...]) * (1.0 / GLA_TAU)

    shift = int(math.log2(c))
    ri = lax.broadcasted_iota(jnp.int32, (rows, rows), 0)
    ci = lax.broadcasted_iota(jnp.int32, (rows, rows), 1)
    same = lax.shift_right_logical(ri, shift) == lax.shift_right_logical(ci, shift)
    causal = jnp.logical_and(same, ri >= ci)
    bc = _dg_sel(causal.astype(BF16), gk, terms=2)
    ends = [bc[(j + 1) * c - 1:(j + 1) * c, :] for j in range(rows // c)]
    bl = jnp.concatenate([jnp.broadcast_to(e, (c, GLA_K)) for e in ends], axis=0)
    qt = q * jnp.exp(bc)
    kt = k * jnp.exp(-bc)
    ks = k * jnp.exp(bl - bc)

    ksl = [slice(h * GLA_DK, (h + 1) * GLA_DK) for h in heads]
    vsl = [slice(h * GLA_DV, (h + 1) * GLA_DV) for h in heads]
    chunks = [(i, slice(i * tt + j * c, i * tt + (j + 1) * c)) for i in seqs for j in range(nc)]
    att = [jnp.where(causal, _dg(qt[:, ksl[h]], kt[:, ksl[h]], _NT), 0.0) for h in heads]
    o_intra = [_dg(att[h], v[:, vsl[h]]) for h in heads]
    kv = [[_dg(v[rs, vsl[h]], ks[rs, ksl[h]], _TN) for _, rs in chunks] for h in heads]
    st = [[state_scr[i, h] for i in seqs] for h in heads]
    for ci_, (i, rs) in enumerate(chunks):
        for h in heads:
            o_scr[rs, vsl[h]] = o_intra[h][rs] + _dg(qt[rs, ksl[h]], st[h][i], _NT)
            st[h][i] = st[h][i] * jnp.exp(ends[ci_][:, ksl[h]]) + kv[h][ci_]
    for h in heads:
        for i in seqs:
            state_scr[i, h] = st[h][i]

    for h in heads:
        o_h = o_scr[:, vsl[h]]
        o_h = o_h * lax.rsqrt(jnp.mean(o_h * o_h, axis=-1, keepdims=True) + NORM_EPS) * gn_ref[...]
        o_scr[:, vsl[h]] = o_h * jax.nn.silu(gz[:, vsl[h]])
    o_ref[...] = o_scr[...].astype(o_ref.dtype).reshape(o_ref.shape)

    @pl.when(n == pl.num_programs(1) - 1)
    def _():
        eye_v = ident(GLA_DV)
        for i in seqs:
            for h in heads:
                sout_ref[i, h] = _dg_selr(state_scr[i, h], eye_v, _TN)


def gla_mix(pg, pxa, s0, wa2_pad, ba, g_norm):
    b, t, _ = pg.shape
    tt = min(GLA_TILE, t)
    c = min(GLA_CHUNK, t)
    nb = min(b, max(1, GLA_SMALL_ROWS // tt))
    assert t % tt == 0 and tt % c == 0 and b % nb == 0
    tok_shape, tok_block, tok_index = _token_blocks(b, t, nb, tt)
    o, s_out = pl.pallas_call(
        functools.partial(_gla_kernel, chunk=c),
        grid=(b // nb, t // tt),
        in_specs=[
            pl.BlockSpec(tok_block + (GLA_MAIN,), tok_index),
            pl.BlockSpec(tok_block + (LANES,), tok_index),
            pl.BlockSpec((nb, GLA_HEADS, GLA_DK, GLA_DV), lambda i, j: (i, 0, 0, 0)),
            _full(wa2_pad.shape), _full((1, GLA_K)), _full((1, GLA_DV)),
        ],
        out_specs=[
            pl.BlockSpec(tok_block + (GLA_V,), tok_index),
            pl.BlockSpec((nb, GLA_HEADS, GLA_DK, GLA_DV), lambda i, j: (i, 0, 0, 0)),
        ],
        out_shape=[jax.ShapeDtypeStruct(tok_shape + (GLA_V,), BF16),
                   jax.ShapeDtypeStruct((b, GLA_HEADS, GLA_DK, GLA_DV), F32)],
        scratch_shapes=[pltpu.VMEM((nb, GLA_HEADS, GLA_DV, GLA_DK), F32), pltpu.VMEM((nb * tt, GLA_V), F32)],
        compiler_params=pltpu.CompilerParams(dimension_semantics=("parallel", "arbitrary")),
        name="gla_mix",
    )(pg.reshape(tok_shape + (GLA_MAIN,)), pxa.reshape(tok_shape + (LANES,)), s0, wa2_pad,
      ba.reshape(1, GLA_K), g_norm.reshape(1, GLA_DV))
    return o.reshape(b, t, GLA_V), s_out


def _cmul(ar, ai, br, bi):
    return ar * br - ai * bi, ar * bi + ai * br


def _s5_prep_kernel(lre_ref, lim_ref, ldt_ref, cre_ref, cim_ref, btre_ref, btim_ref, d_ref,
                    tiles_ref, bsre_ref, bsim_ref, csre_ref, csim_ref, lamre_ref, lamim_ref, *, chunk):
    c = chunk
    p_ = S5_STATE
    gb = LANES // S5_GROUP
    tiles_ref[...] = jnp.zeros_like(tiles_ref)
    bsre_ref[...] = jnp.zeros_like(bsre_ref)
    bsim_ref[...] = jnp.zeros_like(bsim_ref)
    csre_ref[...] = jnp.zeros_like(csre_ref)
    csim_ref[...] = jnp.zeros_like(csim_ref)
    r16 = lax.broadcasted_iota(jnp.int32, (S5_GROUP, S5_GROUP), 0)
    c16 = lax.broadcasted_iota(jnp.int32, (S5_GROUP, S5_GROUP), 1)
    for gi in range(gb):
        lr = lre_ref[0, gi:gi + 1, :]
        li = lim_ref[0, gi:gi + 1, :]
        dt = jnp.exp(ldt_ref[0, gi:gi + 1, :])
        mag = jnp.exp(lr * dt)
        ang = li * dt
        abr, abi = mag * jnp.cos(ang), mag * jnp.sin(ang)
        den = lr * lr + li * li
        fr = ((abr - 1.0) * lr + abi * li) / den
        fi = (abi * lr - (abr - 1.0) * li) / den
        pows = [(jnp.ones_like(abr), jnp.zeros_like(abr))]
        for _ in range(c):
            pows.append(_cmul(pows[-1][0], pows[-1][1], abr, abi))
        cre, cim = cre_ref[0, gi], cim_ref[0, gi]
        btre, btim = btre_ref[0, gi], btim_ref[0, gi]
        q_re, q_im = [], []
        col = slice(gi * p_, (gi + 1) * p_)
        ch = slice(gi * S5_GROUP, (gi + 1) * S5_GROUP)
        for t in range(c):
            er, ei = _cmul(pows[t][0], pows[t][1], fr, fi)
            qr, qi = _cmul(cre, cim, er, ei)
            q_re.append(qr)
            q_im.append(qi)
            er, ei = _cmul(pows[c - 1 - t][0], pows[c - 1 - t][1], fr, fi)
            zr, zi = _cmul(btre, btim, er, ei)
            rows = slice(t * LANES + gi * S5_GROUP, t * LANES + (gi + 1) * S5_GROUP)
            bsre_ref[0, rows, col] = zr.astype(BF16)
            bsim_ref[0, rows, col] = zi.astype(BF16)
            kr, ki = pows[t + 1]
            csre_ref[0, rows, col] = (cre * kr - cim * ki).astype(BF16)
            csim_ref[0, rows, col] = (-cre * ki - cim * kr).astype(BF16)
        q_re = jnp.concatenate(q_re, axis=0)
        q_im = jnp.concatenate(q_im, axis=0)
        m = _dg3(btre, q_re, _NT) - _dg3(btim, q_im, _NT)
        for t in range(c):
            blk = m[:, t * S5_GROUP:(t + 1) * S5_GROUP]
            if t == 0:
                blk = blk + jnp.where(r16 == c16, d_ref[0, gi], 0.0)
            tiles_ref[0, t, ch, ch] = blk.astype(BF16)
        for t in range(c + 1):
            lamre_ref[0, t, :, col] = pows[t][0]
            lamim_ref[0, t, :, col] = pows[t][1]


def s5_prep(prm, chunk):
    g, p_ = S5_GROUPS, S5_STATE
    gb = LANES // S5_GROUP
    nblk = g // gb
    kc = chunk * LANES
    grp = lambda x: x.reshape((nblk, gb) + x.shape[1:])
    args = [grp(prm['lam_re']), grp(prm['lam_im']), grp(prm['log_dt'].reshape(g, 1)),
            grp(prm['c_re']), grp(prm['c_im']),
            grp(jnp.swapaxes(prm['b_re'], 1, 2)), grp(jnp.swapaxes(prm['b_im'], 1, 2)),
            grp(prm['d'].reshape(g, S5_GROUP, 1))]
    blk = lambda x: pl.BlockSpec((1,) + x.shape[1:], lambda i: (i,) + (0,) * (x.ndim - 1))
    out_shape = [jax.ShapeDtypeStruct((nblk, chunk, LANES, LANES), BF16)] + \
        [jax.ShapeDtypeStruct((nblk, kc, gb * p_), BF16)] * 4 + \
        [jax.ShapeDtypeStruct((nblk, chunk + 1, 1, gb * p_), F32)] * 2
    return pl.pallas_call(
        functools.partial(_s5_prep_kernel, chunk=chunk),
        grid=(nblk,),
        in_specs=[blk(a) for a in args],
        out_specs=[blk(o) for o in out_shape],
        out_shape=out_shape,
        compiler_params=pltpu.CompilerParams(dimension_semantics=("parallel",)),
        name="s5_prep",
    )(*args)


def _s5_main_kernel(u_ref, tiles_ref, bsre_ref, bsim_ref, csre_ref, csim_ref, lamre_ref, lamim_ref, x0re_ref,
                    x0im_ref, y_ref, xre_ref, xim_ref, uc_scr, tp_scr, zre_scr, zim_scr, sre_scr, sim_scr,
                    xr_scr, xi_scr, *, chunk):
    c = chunk
    nb, tt, _ = u_ref.shape
    nct = tt // c
    rows = nct * nb
    j = pl.program_id(1)

    @pl.when(j == 0)
    def _():
        xr_scr[...] = x0re_ref[...]
        xi_scr[...] = x0im_ref[...]
        for d in range(c // 2):
            tp_scr[d, :LANES, :LANES] = tiles_ref[0, 2 * d]
            tp_scr[d, :LANES, LANES:] = tiles_ref[0, 2 * d + 1]
            tp_scr[d, LANES:, LANES:] = tiles_ref[0, 2 * d]
            tp_scr[d, LANES:, :LANES] = tiles_ref[0, 2 * d - 1] if d > 0 else jnp.zeros((LANES, LANES), BF16)

    lr, li = lamre_ref[0, 0], lamim_ref[0, 0]
    pair = lambda k: slice(2 * k * LANES, (2 * k + 2) * LANES)
    xr, xi = xr_scr[...], xi_scr[...]
    pieces = S5_PIECES if nct % S5_PIECES == 0 else 1
    nh = nct // pieces
    for piece in range(pieces):
        n0 = piece * nh
        hs = slice(n0 * nb, (n0 + nh) * nb)
        for t in range(c):
            if nct == 1:
                uc_scr[t] = u_ref[:, t, :]
            else:
                for b in range(nb):
                    uc_scr[t, pl.ds(n0 * nb + b, nh, stride=nb), :] = u_ref[b, pl.ds(n0 * c + t, nh, stride=c), :]
        u = jnp.concatenate([uc_scr[t, hs, :].astype(BF16) for t in range(c)], axis=1)
        zre_scr[hs, :] = jnp.dot(u, bsre_ref[0], preferred_element_type=F32)
        zim_scr[hs, :] = jnp.dot(u, bsim_ref[0], preferred_element_type=F32)

        for k in range(c // 2):
            y = jnp.dot(u[:, pair(0)], tp_scr[k], preferred_element_type=F32)
            for kin in range(1, k + 1):
                y = y + jnp.dot(u[:, pair(kin)], tp_scr[k - kin], preferred_element_type=F32)
            uc_scr[2 * k, hs, :] = y[:, :LANES]
            uc_scr[2 * k + 1, hs, :] = y[:, LANES:]

        for i in range(n0, n0 + nh):
            rs = slice(i * nb, (i + 1) * nb)
            sre_scr[rs, :] = xr
            sim_scr[rs, :] = xi
            xr, xi = lr * xr - li * xi + zre_scr[rs, :], li * xr + lr * xi + zim_scr[rs, :]
        sre = sre_scr[hs, :].astype(BF16)
        sim = sim_scr[hs, :].astype(BF16)
        for k in range(c // 2):
            y = (lax.dot_general(sre, csre_ref[0, pair(k), :], _NT, preferred_element_type=F32)
                 + lax.dot_general(sim, csim_ref[0, pair(k), :], _NT, preferred_element_type=F32))
            uc_scr[2 * k, hs, :] = jax.nn.gelu(uc_scr[2 * k, hs, :] + y[:, :LANES])
            uc_scr[2 * k + 1, hs, :] = jax.nn.gelu(uc_scr[2 * k + 1, hs, :] + y[:, LANES:])
        for t in range(c):
            if nct == 1:
                y_ref[:, t, :] = uc_scr[t]
            else:
                for b in range(nb):
                    y_ref[b, pl.ds(n0 * c + t, nh, stride=c), :] = uc_scr[t, pl.ds(n0 * nb + b, nh, stride=nb), :]
    xr_scr[...] = xr
    xi_scr[...] = xi

    @pl.when(j == pl.num_programs(1) - 1)
    def _():
        xre_ref[...] = xr
        xim_ref[...] = xi


def s5_mix(u, x0re, x0im, tables, chunk):
    b, t, e = u.shape
    tiles, bsre, bsim, csre, csim, lamre, lamim = tables
    nblk = e // LANES
    gp = (LANES // S5_GROUP) * S5_STATE
    nct = min(t // chunk, max(1, S5_ROWS // b))
    assert (t // chunk) % nct == 0
    tt = nct * chunk
    rows = nct * b
    kc = chunk * LANES
    last = tiles.shape[1] // chunk - 1
    assert tiles.shape[1] % chunk == 0
    bs_spec = pl.BlockSpec((1, kc, gp), lambda i, j: (i, last, 0))
    cs_spec = pl.BlockSpec((1, kc, gp), lambda i, j: (i, 0, 0))
    lam_spec = pl.BlockSpec((1, 1, 1, gp), lambda i, j: (i, chunk, 0, 0))
    y, xre, xim = pl.pallas_call(
        functools.partial(_s5_main_kernel, chunk=chunk),
        grid=(nblk, t // tt),
        in_specs=[pl.BlockSpec((b, tt, LANES), lambda i, j: (0, j, i)),
                  pl.BlockSpec((1, chunk, LANES, LANES), lambda i, j: (i, 0, 0, 0)),
                  bs_spec, bs_spec, cs_spec, cs_spec, lam_spec, lam_spec,
                  pl.BlockSpec((b, gp), lambda i, j: (0, i)), pl.BlockSpec((b, gp), lambda i, j: (0, i))],
        out_specs=[pl.BlockSpec((b, tt, LANES), lambda i, j: (0, j, i)),
                   pl.BlockSpec((b, gp), lambda i, j: (0, i)), pl.BlockSpec((b, gp), lambda i, j: (0, i))],
        out_shape=[jax.ShapeDtypeStruct((b, t, e), F32),
                   jax.ShapeDtypeStruct((b, S5_GROUPS * S5_STATE), F32),
                   jax.ShapeDtypeStruct((b, S5_GROUPS * S5_STATE), F32)],
        scratch_shapes=[pltpu.VMEM((chunk, rows, LANES), F32), pltpu.VMEM((chunk // 2, 2 * LANES, 2 * LANES), BF16)]
        + [pltpu.VMEM((rows, gp), F32)] * 4 + [pltpu.VMEM((b, gp), F32)] * 2,
        compiler_params=pltpu.CompilerParams(dimension_semantics=("parallel", "arbitrary"),
                                             vmem_limit_bytes=VMEM_LIMIT),
        name="s5_main",
    )(u, tiles, bsre, bsim, csre, csim, lamre, lamim, x0re.reshape(b, -1), x0im.reshape(b, -1))
    return y, xre.reshape(b, S5_GROUPS, S5_STATE), xim.reshape(b, S5_GROUPS, S5_STATE)


def _trunk(x, st_shift, st_wkv, st_gla, st_re, st_im, w):
    b, t, d = x.shape
    m = b * t
    x2 = x.reshape(m, d)

    p_r, p_g, p_xa = norm_proj(x2, w['norm_mix_pre'][0], [w['w_in_rwkv'], w['w_in_gla'], w['w_in_xa']],
                               [BF16, BF16, BF16])
    o_r, wkv = rwkv_mix(p_r.reshape(b, t, -1), st_shift[0], st_wkv, w['rwkv'])
    o_g, gla = gla_mix(p_g.reshape(b, t, -1), p_xa.reshape(b, t, -1), st_gla[0], w['gla_wa2'], w['gla_ba'],
                       w['gla_norm'])
    shift = p_r.reshape(b, t, -1)[:, -1].astype(F32)
    x2 = proj_post([o_r.reshape(m, -1), o_g.reshape(m, -1)], [w['w_out_rwkv'], w['w_out_gla']],
                   w['norm_mix_post'][0], x2)
    x2 = ffn(x2, w['norm_ffn_pre'][0], w['w_ff_up'][0], w['w_ff_down'][0], w['norm_ffn_post'][0])

    (u,) = norm_proj(x2, w['norm_mix_pre'][1], [w['s5_w_in']], [F32])
    chunk = min(S5_CHUNK, t)
    y, s_re, s_im = s5_mix(u.reshape(b, t, d), st_re[0], st_im[0], w['s5_tables'], chunk)
    x2 = proj_post([y.reshape(m, d)], [w['s5_w_out']], w['norm_mix_post'][1], x2, glu=True)
    x2 = ffn(x2, w['norm_ffn_pre'][1], w['w_ff_up'][1], w['w_ff_down'][1], w['norm_ffn_post'][1])
    return x2.reshape(b, t, d), shift[None], wkv[None], gla[None], s_re[None], s_im[None]


def kernel(x_prompt, x_sample, state_rwkv_shift, state_rwkv_wkv, state_gla, state_s5_re, state_s5_im,
           norm_mix_pre, norm_mix_post, norm_ffn_pre, norm_ffn_post,
           w_mix_in, w_mix_out, rwkv_mu, rwkv_w0, rwkv_w2, rwkv_a0, rwkv_a2, rwkv_g2,
           rwkv_k_k, rwkv_k_a, rwkv_r_k, rwkv_gn_g, rwkv_gn_b, gla_wa2, gla_ba, gla_norm,
           s5_w_in, s5_lam_re, s5_lam_im, s5_log_dt, s5_b_re, s5_b_im, s5_c_re, s5_c_im, s5_d, s5_w_out,
           w_ff_up, w_ff_down):
    assert norm_mix_pre.shape[0] == 2, "two layers: one RWKV-7/GLA layer, one S5 layer"
    bf = lambda a: a.astype(BF16)
    w_in = w_mix_in[0]
    gla0 = RWKV_PROJ
    xa0 = gla0 + 2 * GLA_K + GLA_V
    w_in_gla = jnp.concatenate([w_in[:, gla0:xa0], w_in[:, xa0 + GLA_LORA:]], axis=1)
    w_in_xa = jnp.pad(w_in[:, xa0:xa0 + GLA_LORA], ((0, 0), (0, LANES - GLA_LORA)))
    s5_prm = dict(lam_re=s5_lam_re[0], lam_im=s5_lam_im[0], log_dt=s5_log_dt[0], b_re=s5_b_re[0],
                  b_im=s5_b_im[0], c_re=s5_c_re[0], c_im=s5_c_im[0], d=s5_d[0])
    tables = s5_prep(s5_prm, min(S5_CHUNK, max(x_prompt.shape[1], x_sample.shape[1])))
    w = dict(
        norm_mix_pre=norm_mix_pre, norm_mix_post=norm_mix_post, norm_ffn_pre=norm_ffn_pre,
        norm_ffn_post=norm_ffn_post,
        w_in_rwkv=bf(w_in[:, :RWKV_PROJ]), w_in_gla=bf(w_in_gla), w_in_xa=bf(w_in_xa),
        w_out_rwkv=bf(w_mix_out[0, :RWKV_W]), w_out_gla=bf(w_mix_out[0, RWKV_W:]),
        rwkv=dict(mu=rwkv_mu[0], w0=rwkv_w0[0], w2=rwkv_w2[0], a0=rwkv_a0[0], a2=rwkv_a2[0], g2=rwkv_g2[0],
                  k_k=rwkv_k_k[0], k_a=rwkv_k_a[0], r_k=rwkv_r_k[0], gn_g=rwkv_gn_g[0], gn_b=rwkv_gn_b[0]),
        gla_wa2=bf(jnp.pad(gla_wa2[0], ((0, LANES - GLA_LORA), (0, 0)))), gla_ba=gla_ba[0], gla_norm=gla_norm[0],
        s5_w_in=bf(s5_w_in[0]), s5_w_out=bf(s5_w_out[0]), s5_tables=tables,
        w_ff_up=[bf(w_ff_up[l]) for l in range(2)], w_ff_down=[bf(w_ff_down[l]) for l in range(2)],
    )
    bp = x_prompt.shape[0]
    zeros = lambda s: jnp.zeros((s.shape[0], bp) + s.shape[2:], x_prompt.dtype)
    y_p, sh_p, wkv_p, gla_p, re_p, im_p = _trunk(x_prompt, zeros(state_rwkv_shift), zeros(state_rwkv_wkv),
                                                 zeros(state_gla), zeros(state_s5_re), zeros(state_s5_im), w)
    y_s, sh_s, wkv_s, gla_s, re_s, im_s = _trunk(x_sample, state_rwkv_shift, state_rwkv_wkv, state_gla,
                                                 state_s5_re, state_s5_im, w)
    return (y_p, y_s, sh_p, sh_s, wkv_p, wkv_s, gla_p, gla_s, re_p, re_s, im_p, im_s)
```

```python
import functools
import math

import jax
import jax.numpy as jnp
from jax import lax
from jax.experimental import pallas as pl
from jax.experimental.pallas import tpu as pltpu

F32 = jnp.float32
BF16 = jnp.bfloat16

D_MODEL = 1024
NORM_EPS = 1e-6
RWKV_HEADS = 8
RWKV_HEAD = 64
RWKV_W = RWKV_HEADS * RWKV_HEAD
RWKV_PROJ = 1792
RWKV_CHUNK = 64
RWKV_ROWS = 64
RWKV_STEP_ROWS = 256
GN_EPS = 64e-5
GLA_HEADS = 4
GLA_DK = 64
GLA_DV = 128
GLA_K = GLA_HEADS * GLA_DK
GLA_V = GLA_HEADS * GLA_DV
GLA_LORA = 16
GLA_TAU = 16.0
GLA_CHUNK = 32
GLA_MAIN = 2 * GLA_K + 2 * GLA_V
GLA_TILE = 256
GLA_TILE_SEQS = 4
GLA_SMALL_ROWS = 64
S5_GROUPS = 64
S5_GROUP = 16
S5_STATE = 64
S5_CHUNK = 16
S5_ROWS = 512
S5_PIECES = 2
LANES = 128
DENSE_HALVES = 2
VMEM_LIMIT = 48 * 1024 * 1024
FFN_VMEM_LIMIT = 56 * 1024 * 1024

_NN = (((1,), (0,)), ((), ()))
_NT = (((1,), (1,)), ((), ()))
_TN = (((0,), (0,)), ((), ()))


def _dg(a, b, dn=_NN):
    return lax.dot_general(a.astype(BF16), b.astype(BF16), dn, preferred_element_type=F32)


def _split2(x):
    hi = x.astype(BF16)
    lo = (x - hi.astype(F32)).astype(BF16)
    return hi, lo


def _split3(x):
    hi = x.astype(BF16)
    r1 = x - hi.astype(F32)
    mid = r1.astype(BF16)
    lo = (r1 - mid.astype(F32)).astype(BF16)
    return hi, mid, lo


def _dg3(a, b, dn=_NN):
    ah, al = _split2(a)
    bh, bl = _split2(b)
    d = lambda x, y: lax.dot_general(x, y, dn, preferred_element_type=F32)
    return d(ah, bh) + (d(ah, bl) + d(al, bh))


def _dg_sel(sel, x, dn=_NN, terms=3):
    d = lambda y: lax.dot_general(sel, y, dn, preferred_element_type=F32)
    if terms == 2:
        h, l = _split2(x)
        return d(h) + d(l)
    h, m, l = _split3(x)
    return d(h) + (d(m) + d(l))


def _dg_selr(x, sel, dn=_NN):
    d = lambda y: lax.dot_general(y, sel, dn, preferred_element_type=F32)
    h, m, l = _split3(x)
    return d(h) + (d(m) + d(l))


def _rms(x, g):
    return x * lax.rsqrt(jnp.mean(x * x, axis=-1, keepdims=True) + NORM_EPS) * g


def _row_tile(m, want):
    t = min(m, want)
    assert m % t == 0, (m, t)
    return t


def _full(shape):
    nd = len(shape)
    return pl.BlockSpec(shape, lambda *_: (0,) * nd)


def _token_blocks(b, t, nb, tt):
    if tt == t:
        return (b // nb, nb * t), (1, nb * t), (lambda i, j: (i, 0, 0))
    return (b, t), (nb, tt), (lambda i, j: (i, j, 0))


def _halves(tm):
    hm = tm // DENSE_HALVES
    return [slice(i * hm, (i + 1) * hm) for i in range(DENSE_HALVES)]


def _norm_proj_kernel(x_ref, g_ref, *refs):
    n = len(refs) // 2
    for rs in _halves(x_ref.shape[0]):
        h = _rms(x_ref[rs, :], g_ref[...]).astype(BF16)
        for w_ref, o_ref in zip(refs[:n], refs[n:]):
            o_ref[rs, :] = jnp.dot(h, w_ref[...], preferred_element_type=F32).astype(o_ref.dtype)


def norm_proj(x, g, ws, out_dtypes, tm=1024):
    m, d = x.shape
    tm = _row_tile(m, tm)
    once = dict(pipeline_mode=pl.Buffered(1))
    return pl.pallas_call(
        _norm_proj_kernel,
        grid=(m // tm,),
        in_specs=[pl.BlockSpec((tm, d), lambda i: (i, 0)), _full((1, d))]
        + [pl.BlockSpec(w.shape, lambda i: (0, 0), **once) for w in ws],
        out_specs=[pl.BlockSpec((tm, w.shape[1]), lambda i: (i, 0)) for w in ws],
        out_shape=[jax.ShapeDtypeStruct((m, w.shape[1]), dt) for w, dt in zip(ws, out_dtypes)],
        compiler_params=pltpu.CompilerParams(dimension_semantics=("parallel",), vmem_limit_bytes=VMEM_LIMIT),
        name="norm_proj",
    )(x, g.reshape(1, d), *ws)


def _proj_post_kernel(*refs, n_in, glu):
    a_refs = refs[:n_in]
    w_refs = refs[n_in:2 * n_in]
    g_ref, x_ref, o_ref = refs[2 * n_in:]
    for rs in _halves(x_ref.shape[0]):
        m = None
        for a_ref, w_ref in zip(a_refs, w_refs):
            t = jnp.dot(a_ref[rs, :].astype(BF16), w_ref[...], preferred_element_type=F32)
            m = t if m is None else m + t
        if glu:
            m = m[:, :D_MODEL] * jax.nn.sigmoid(m[:, D_MODEL:])
        o_ref[rs, :] = x_ref[rs, :] + _rms(m, g_ref[...])


def proj_post(a_list, w_list, g, x, glu=False, tm=1024):
    m, d = x.shape
    tm = _row_tile(m, tm)
    n_in = len(a_list)
    once = dict(pipeline_mode=pl.Buffered(1))
    return pl.pallas_call(
        functools.partial(_proj_post_kernel, n_in=n_in, glu=glu),
        grid=(m // tm,),
        in_specs=[pl.BlockSpec((tm, a.shape[1]), lambda i: (i, 0)) for a in a_list]
        + [pl.BlockSpec(w.shape, lambda i: (0, 0), **once) for w in w_list]
        + [_full((1, d)), pl.BlockSpec((tm, d), lambda i: (i, 0))],
        out_specs=pl.BlockSpec((tm, d), lambda i: (i, 0)),
        out_shape=jax.ShapeDtypeStruct((m, d), F32),
        compiler_params=pltpu.CompilerParams(dimension_semantics=("parallel",), vmem_limit_bytes=VMEM_LIMIT),
        name="proj_post",
    )(*a_list, *w_list, g.reshape(1, d), x)


def _ffn_kernel(x_ref, gpre_ref, wup_ref, wdn_ref, gpost_ref, o_ref, *, tf):
    f = wup_ref.shape[1]
    for rs in _halves(x_ref.shape[0]):
        x = x_ref[rs, :]
        h = _rms(x, gpre_ref[...]).astype(BF16)
        acc = None
        for j in range(f // tf):
            u = jnp.dot(h, wup_ref[:, j * tf:(j + 1) * tf], preferred_element_type=F32)
            u = jnp.square(jnp.maximum(u, 0.0)).astype(BF16)
            t = jnp.dot(u, wdn_ref[j * tf:(j + 1) * tf, :], preferred_element_type=F32)
            acc = t if acc is None else acc + t
        o_ref[rs, :] = x + _rms(acc, gpost_ref[...])


def ffn(x, g_pre, w_up, w_down, g_post, tm=1024, tf=512):
    m, d = x.shape
    f = w_up.shape[1]
    tm = _row_tile(m, tm)
    once = dict(pipeline_mode=pl.Buffered(1))
    return pl.pallas_call(
        functools.partial(_ffn_kernel, tf=tf),
        grid=(m // tm,),
        in_specs=[
            pl.BlockSpec((tm, d), lambda i: (i, 0)),
            _full((1, d)),
            pl.BlockSpec((d, f), lambda i: (0, 0), **once),
            pl.BlockSpec((f, d), lambda i: (0, 0), **once),
            _full((1, d)),
        ],
        out_specs=pl.BlockSpec((tm, d), lambda i: (i, 0)),
        out_shape=jax.ShapeDtypeStruct((m, d), F32),
        compiler_params=pltpu.CompilerParams(dimension_semantics=("parallel",), vmem_limit_bytes=FFN_VMEM_LIMIT),
        name="ffn",
    )(x, g_pre.reshape(1, d), w_up, w_down, g_post.reshape(1, d))


def _head_sum(x, ones_bd):
    return jnp.dot(x.astype(BF16), ones_bd, preferred_element_type=F32)


def _rwkv_kernel(p_ref, sh0_ref, s0_ref, mu_ref, w0_ref, w2_ref, a0_ref, a2_ref, g2_ref, kk_ref, ka_ref,
                 rk_ref, gng_ref, gnb_ref, ones_ref, o_ref, sout_ref, state_scr, carry_scr, y_scr, *, chunk):
    c = chunk
    nb = s0_ref.shape[0]
    rows = nb * c
    shift = int(math.log2(c))
    n = pl.program_id(1)

    @pl.when(n == 0)
    def _():
        carry_scr[...] = sh0_ref[...]
        for i in range(nb):
            for h in range(RWKV_HEADS):
                state_scr[i, h // 2, :, (h % 2) * RWKV_HEAD:(h % 2 + 1) * RWKV_HEAD] = s0_ref[i, h]

    p3 = p_ref[...]
    p = p3.astype(F32).reshape(rows, RWKV_PROJ)
    t_idx = jnp.bitwise_and(lax.broadcasted_iota(jnp.int32, (rows, 1), 0), c - 1)
    carry = jnp.broadcast_to(carry_scr[...], (nb, c, RWKV_PROJ)).reshape(rows, RWKV_PROJ)
    prev = jnp.where(t_idx == 0, carry, pltpu.roll(p, 1, 0))
    if p3.shape[0] == nb:
        carry_scr[...] = p3[:, c - 1:c, :].astype(F32)
    z = p + (prev - p) * mu_ref[...]

    w_ = RWKV_W
    r = z[:, 0:w_]
    k = z[:, w_:2 * w_]
    v = z[:, 2 * w_:3 * w_]
    lw, la = w2_ref.shape[0], a2_ref.shape[0]
    xw = z[:, 3 * w_:3 * w_ + lw]
    xa = z[:, 3 * w_ + lw:3 * w_ + lw + la]
    xg = z[:, 3 * w_ + lw + la:]

    wlog = -jax.nn.softplus(-(w0_ref[...] + _dg(jnp.tanh(xw), w2_ref[...]))) - 0.5
    logd = -jnp.exp(wlog)
    a = jax.nn.sigmoid(a0_ref[...] + _dg(xa, a2_ref[...]))
    g = _dg(jax.nn.sigmoid(xg), g2_ref[...])

    ones_bd = ones_ref[...]
    kk = k * kk_ref[...]
    kk = kk * lax.rsqrt(jnp.maximum(_head_sum(kk * kk, ones_bd), 1e-24))
    k2 = k * (1.0 + (a - 1.0) * ka_ref[...])
    bb = kk * a

    ra = lax.broadcasted_iota(jnp.int32, (rows, rows), 0)
    ca = lax.broadcasted_iota(jnp.int32, (rows, rows), 1)
    same_all = lax.shift_right_logical(ra, shift) == lax.shift_right_logical(ca, shift)
    lower_eq_all = jnp.logical_and(same_all, ra >= ca)
    cum = _dg_sel(lower_eq_all.astype(BF16), logd, terms=2)
    tot = _dg_sel(same_all.astype(BF16), logd, terms=2)
    g_in = jnp.exp(cum)
    g_ex = jnp.exp(cum - logd)
    g_inv = jnp.exp(-cum)
    g_hat = jnp.exp(tot - cum)
    g_all = jnp.exp(tot)
    rt = r * g_in
    kt = kk * g_ex
    ktil = k2 * g_inv
    btil = bb * g_inv
    khat = k2 * g_hat
    bhat = bb * g_hat

    gr = RWKV_ROWS
    assert rows % gr == 0 and gr == RWKV_HEAD and gr % c == 0
    spg = gr // c
    hp = 2 * RWKV_HEAD
    ri = lax.broadcasted_iota(jnp.int32, (gr, hp), 0)
    si = jnp.bitwise_and(lax.broadcasted_iota(jnp.int32, (gr, hp), 1), RWKV_HEAD - 1)
    same = lax.shift_right_logical(ri, shift) == lax.shift_right_logical(si, shift)
    lower = jnp.logical_and(same, ri > si)
    lower_eq = jnp.logical_and(same, ri >= si)
    eye = (ri == si).astype(F32)
    bi = lax.broadcasted_iota(jnp.int32, (hp, hp), 0)
    bj = lax.broadcasted_iota(jnp.int32, (hp, hp), 1)
    head_bits = int(math.log2(RWKV_HEAD))
    on_diag = lax.shift_right_logical(bi, head_bits) == lax.shift_right_logical(bj, head_bits)
    first_head = lax.broadcasted_iota(jnp.int32, (RWKV_HEAD, hp), 1) < RWKV_HEAD

    def bd(x2):
        x2 = x2.astype(BF16)
        return jnp.where(on_diag, jnp.concatenate([x2, x2], axis=0), jnp.zeros((), BF16))

    probs = [(g_, q) for g_ in range(rows // gr) for q in range(RWKV_HEADS // 2)]
    np_ = len(probs)
    psl = lambda q: slice(q * hp, (q + 1) * hp)
    gsl = lambda g_: slice(g_ * gr, (g_ + 1) * gr)
    pick = lambda x, g_, q: x[gsl(g_), psl(q)]
    seq_rows = lambda g_, i: slice(g_ * gr + i * c, g_ * gr + (i + 1) * c)
    cat = lambda xs: jnp.concatenate(xs, axis=0) if len(xs) > 1 else xs[0]
    kr = [jnp.concatenate([pick(kt, *q), pick(rt, *q)], axis=0) for q in probs]
    gb = [_dg(kr[j], bd(pick(btil, *q)), _NT) for j, q in enumerate(probs)]
    gk = [_dg(kr[j], bd(pick(ktil, *q)), _NT) for j, q in enumerate(probs)]
    a_b = [jnp.where(lower, x[:gr], 0.0) for x in gb]
    a_rb = [jnp.where(lower_eq, x[gr:], 0.0) for x in gb]
    a_kk = [jnp.concatenate([jnp.where(lower, x[:gr], 0.0), jnp.where(lower_eq, x[gr:], 0.0)], axis=0)
            for x in gk]
    ks = [[_dg(jnp.concatenate([kt[seq_rows(g_, i), psl(q)], rt[seq_rows(g_, i), psl(q)]], axis=0),
               bd(state_scr[g_ * spg + i, q]), _NT) for i in range(spg)] for g_, q in probs]
    x0 = [cat([x[:c] for x in ks[j]]) for j in range(np_)]
    r0 = [cat([x[c:] for x in ks[j]]) for j in range(np_)]
    av = [_dg(a_kk[j], bd(pick(v, *q))) for j, q in enumerate(probs)]
    tinv = [eye - x for x in a_b]
    pw = [_dg(x, bd(x)) for x in a_b]
    span = 2
    while span < c:
        span *= 2
        if span < c:
            pr = [_dg(jnp.concatenate([t_, p_], axis=0), bd(p_)) for t_, p_ in zip(tinv, pw)]
            tinv = [t_ + x[:gr] for t_, x in zip(tinv, pr)]
            pw = [x[gr:] for x in pr]
        else:
            tinv = [t_ + _dg(t_, bd(p_)) for t_, p_ in zip(tinv, pw)]
    u = [_dg(tinv[j], bd(x0[j] + av[j][:gr])) for j in range(np_)]
    for j, (g_, q) in enumerate(probs):
        y_scr[gsl(g_), psl(q)] = r0[j] + av[j][gr:] - _dg(a_rb[j], bd(u[j]))
    for j, (g_, q) in enumerate(probs):
        for i in range(spg):
            rs = seq_rows(g_, i)
            vu = jnp.concatenate([v[rs, psl(q)], -u[j][i * c:(i + 1) * c]], axis=0)
            kb = jnp.concatenate([khat[rs, psl(q)], bhat[rs, psl(q)]], axis=0)
            full = _dg(vu, kb, _TN)
            upd = jnp.where(first_head, full[:RWKV_HEAD], full[RWKV_HEAD:])
            s_i = g_ * spg + i
            state_scr[s_i, q] = state_scr[s_i, q] * g_all[rs.start:rs.start + 1, psl(q)] + upd

    y = y_scr[...]
    inv_n = 1.0 / RWKV_HEAD
    mean = _head_sum(y, ones_bd) * inv_n
    yc = y - mean
    var = _head_sum(yc * yc, ones_bd) * inv_n
    yn = yc * lax.rsqrt(var + GN_EPS) * gng_ref[...] + gnb_ref[...]
    bonus = _head_sum(r * k2 * rk_ref[...], ones_bd) * v
    o_ref[...] = ((yn + bonus) * g).astype(o_ref.dtype).reshape(o_ref.shape)

    @pl.when(n == pl.num_programs(1) - 1)
    def _():
        for i in range(nb):
            for h in range(RWKV_HEADS):
                sout_ref[i, h] = state_scr[i, h // 2, :, (h % 2) * RWKV_HEAD:(h % 2 + 1) * RWKV_HEAD]


def rwkv_mix(p, shift0, s0, prm, layer=0):
    b, t, _ = p.shape
    c = min(RWKV_CHUNK, t)
    assert t % c == 0
    nb = min(b, RWKV_STEP_ROWS // c)
    assert b % nb == 0
    w_ = RWKV_W
    row = lambda x: x.reshape(1, -1)
    ones_bd = jnp.kron(jnp.eye(RWKV_HEADS, dtype=F32), jnp.ones((RWKV_HEAD, RWKV_HEAD), F32)).astype(BF16)
    consts = [row(prm['mu']), row(prm['w0']), prm['w2'].astype(BF16), row(prm['a0']), prm['a2'].astype(BF16),
              prm['g2'].astype(BF16), row(prm['k_k']), row(prm['k_a']), row(prm['r_k']), row(prm['gn_g']),
              row(prm['gn_b']), ones_bd]
    tok_shape, tok_block, tok_index = _token_blocks(b, t, nb, c)
    o, s_out = pl.pallas_call(
        functools.partial(_rwkv_kernel, chunk=c),
        grid=(b // nb, t // c),
        in_specs=[
            pl.BlockSpec(tok_block + (RWKV_PROJ,), tok_index),
            pl.BlockSpec((nb, 1, RWKV_PROJ), lambda i, j: (i, 0, 0)),
            pl.BlockSpec((None, nb, RWKV_HEADS, RWKV_HEAD, RWKV_HEAD), lambda i, j: (layer, i, 0, 0, 0)),
        ] + [_full(x.shape) for x in consts],
        out_specs=[
            pl.BlockSpec(tok_block + (w_,), tok_index),
            pl.BlockSpec((nb, RWKV_HEADS, RWKV_HEAD, RWKV_HEAD), lambda i, j: (i, 0, 0, 0)),
        ],
        out_shape=[jax.ShapeDtypeStruct(tok_shape + (w_,), BF16),
                   jax.ShapeDtypeStruct((b, RWKV_HEADS, RWKV_HEAD, RWKV_HEAD), F32)],
        scratch_shapes=[pltpu.VMEM((nb, RWKV_HEADS // 2, RWKV_HEAD, 2 * RWKV_HEAD), F32),
                        pltpu.VMEM((nb, 1, RWKV_PROJ), F32),
                        pltpu.VMEM((nb * c, w_), F32)],
        compiler_params=pltpu.CompilerParams(dimension_semantics=("parallel", "arbitrary")),
        name="rwkv_mix",
    )(p.reshape(tok_shape + (RWKV_PROJ,)), shift0.reshape(b, 1, RWKV_PROJ), s0, *consts)
    return o.reshape(b, t, w_), s_out


def _gla_kernel(pg_ref, xa_ref, s0_ref, wa2_ref, ba_ref, gn_ref, o_ref, sout_ref, state_scr, o_scr, *, chunk):
    c = chunk
    nb = s0_ref.shape[0]
    rows = pg_ref.shape[0] * pg_ref.shape[1]
    tt = rows // nb
    nc = tt // c
    n = pl.program_id(1)
    heads = range(GLA_HEADS)
    seqs = range(nb)
    ident = lambda m: (lax.broadcasted_iota(jnp.int32, (m, m), 0)
                       == lax.broadcasted_iota(jnp.int32, (m, m), 1)).astype(BF16)

    @pl.when(n == 0)
    def _():
        eye_k = ident(GLA_DK)
        for i in seqs:
            for h in heads:
                state_scr[i, h] = _dg_selr(s0_ref[i, h], eye_k, _TN)

    pg = pg_ref[...].astype(F32).reshape(rows, GLA_MAIN)
    q = pg[:, 0:GLA_K] * (GLA_DK ** -0.5)
    k = pg[:, GLA_K:2 * GLA_K]
    v = pg[:, 2 * GLA_K:2 * GLA_K + GLA_V]
    gz = pg[:, 2 * GLA_K + GLA_V:]
    xa = xa_ref[...].astype(F32).reshape(rows, LANES)
    gk = jax.nn.log_sigmoid(_dg(xa, wa2_ref[...]) + ba_ref[...]) * (1.0 / GLA_TAU)

    gsz = min(rows, GLA_TILE)
    groups = [slice(g * gsz, (g + 1) * gsz) for g in range(rows // gsz)]
    shift = int(math.log2(c))
    ri = lax.broadcasted_iota(jnp.int32, (gsz, gsz), 0)
    ci = lax.broadcasted_iota(jnp.int32, (gsz, gsz), 1)
    same = lax.shift_right_logical(ri, shift) == lax.shift_right_logical(ci, shift)
    causal = jnp.logical_and(same, ri >= ci)
    cat = lambda xs: jnp.concatenate(xs, axis=0) if len(xs) > 1 else xs[0]
    bc = cat([_dg_sel(causal.astype(BF16), gk[rg, :], terms=2) for rg in groups])
    ends = [bc[(j + 1) * c - 1:(j + 1) * c, :] for j in range(rows // c)]
    bl = jnp.concatenate([jnp.broadcast_to(e, (c, GLA_K)) for e in ends], axis=0)
    qt = q * jnp.exp(bc)
    kt = k * jnp.exp(-bc)
    ks = k * jnp.exp(bl - bc)

    ksl = [slice(h * GLA_DK, (h + 1) * GLA_DK) for h in heads]
    vsl = [slice(h * GLA_DV, (h + 1) * GLA_DV) for h in heads]
    chunks = [(i, slice(i * tt + j * c, i * tt + (j + 1) * c)) for i in seqs for j in range(nc)]
    att = [[jnp.where(causal, _dg(qt[rg, ksl[h]], kt[rg, ksl[h]], _NT), 0.0) for rg in groups] for h in heads]
    o_intra = [cat([_dg(att[h][g], v[rg, vsl[h]]) for g, rg in enumerate(groups)]) for h in heads]
    kv = [[_dg(v[rs, vsl[h]], ks[rs, ksl[h]], _TN) for _, rs in chunks] for h in heads]
    st = [[state_scr[i, h] for i in seqs] for h in heads]
    for ci_, (i, rs) in enumerate(chunks):
        for h in heads:
            o_scr[rs, vsl[h]] = o_intra[h][rs] + _dg(qt[rs, ksl[h]], st[h][i], _NT)
            st[h][i] = st[h][i] * jnp.exp(ends[ci_][:, ksl[h]]) + kv[h][ci_]
    for h in heads:
        for i in seqs:
            state_scr[i, h] = st[h][i]

    for h in heads:
        o_h = o_scr[:, vsl[h]]
        o_h = o_h * lax.rsqrt(jnp.mean(o_h * o_h, axis=-1, keepdims=True) + NORM_EPS) * gn_ref[...]
        o_scr[:, vsl[h]] = o_h * jax.nn.silu(gz[:, vsl[h]])
    o_ref[...] = o_scr[...].astype(o_ref.dtype).reshape(o_ref.shape)

    @pl.when(n == pl.num_programs(1) - 1)
    def _():
        eye_v = ident(GLA_DV)
        for i in seqs:
            for h in heads:
                sout_ref[i, h] = _dg_selr(state_scr[i, h], eye_v, _TN)


def gla_mix(pg, pxa, s0, wa2_pad, ba, g_norm):
    b, t, _ = pg.shape
    tt = min(GLA_TILE, t)
    c = min(GLA_CHUNK, t)
    nb = min(b, max(1, (GLA_TILE_SEQS * GLA_TILE if tt == GLA_TILE else GLA_SMALL_ROWS) // tt))
    assert t % tt == 0 and tt % c == 0 and b % nb == 0
    tok_shape, tok_block, tok_index = _token_blocks(b, t, nb, tt)
    o, s_out = pl.pallas_call(
        functools.partial(_gla_kernel, chunk=c),
        grid=(b // nb, t // tt),
        in_specs=[
            pl.BlockSpec(tok_block + (GLA_MAIN,), tok_index),
            pl.BlockSpec(tok_block + (LANES,), tok_index),
            pl.BlockSpec((nb, GLA_HEADS, GLA_DK, GLA_DV), lambda i, j: (i, 0, 0, 0)),
            _full(wa2_pad.shape), _full((1, GLA_K)), _full((1, GLA_DV)),
        ],
        out_specs=[
            pl.BlockSpec(tok_block + (GLA_V,), tok_index),
            pl.BlockSpec((nb, GLA_HEADS, GLA_DK, GLA_DV), lambda i, j: (i, 0, 0, 0)),
        ],
        out_shape=[jax.ShapeDtypeStruct(tok_shape + (GLA_V,), BF16),
                   jax.ShapeDtypeStruct((b, GLA_HEADS, GLA_DK, GLA_DV), F32)],
        scratch_shapes=[pltpu.VMEM((nb, GLA_HEADS, GLA_DV, GLA_DK), F32), pltpu.VMEM((nb * tt, GLA_V), F32)],
        compiler_params=pltpu.CompilerParams(dimension_semantics=("parallel", "arbitrary")),
        name="gla_mix",
    )(pg.reshape(tok_shape + (GLA_MAIN,)), pxa.reshape(tok_shape + (LANES,)), s0, wa2_pad,
      ba.reshape(1, GLA_K), g_norm.reshape(1, GLA_DV))
    return o.reshape(b, t, GLA_V), s_out


def _cmul(ar, ai, br, bi):
    return ar * br - ai * bi, ar * bi + ai * br


def _s5_prep_kernel(lre_ref, lim_ref, ldt_ref, cre_ref, cim_ref, btre_ref, btim_ref, d_ref,
                    tiles_ref, bsre_ref, bsim_ref, csre_ref, csim_ref, lamre_ref, lamim_ref, *, chunk):
    c = chunk
    p_ = S5_STATE
    gb = LANES // S5_GROUP
    tiles_ref[...] = jnp.zeros_like(tiles_ref)
    bsre_ref[...] = jnp.zeros_like(bsre_ref)
    bsim_ref[...] = jnp.zeros_like(bsim_ref)
    csre_ref[...] = jnp.zeros_like(csre_ref)
    csim_ref[...] = jnp.zeros_like(csim_ref)
    r16 = lax.broadcasted_iota(jnp.int32, (S5_GROUP, S5_GROUP), 0)
    c16 = lax.broadcasted_iota(jnp.int32, (S5_GROUP, S5_GROUP), 1)
    for gi in range(gb):
        lr = lre_ref[0, gi:gi + 1, :]
        li = lim_ref[0, gi:gi + 1, :]
        dt = jnp.exp(ldt_ref[0, gi:gi + 1, :])
        mag = jnp.exp(lr * dt)
        ang = li * dt
        abr, abi = mag * jnp.cos(ang), mag * jnp.sin(ang)
        den = lr * lr + li * li
        fr = ((abr - 1.0) * lr + abi * li) / den
        fi = (abi * lr - (abr - 1.0) * li) / den
        pows = [(jnp.ones_like(abr), jnp.zeros_like(abr))]
        for _ in range(c):
            pows.append(_cmul(pows[-1][0], pows[-1][1], abr, abi))
        cre, cim = cre_ref[0, gi], cim_ref[0, gi]
        btre, btim = btre_ref[0, gi], btim_ref[0, gi]
        q_re, q_im = [], []
        col = slice(gi * p_, (gi + 1) * p_)
        ch = slice(gi * S5_GROUP, (gi + 1) * S5_GROUP)
        for t in range(c):
            er, ei = _cmul(pows[t][0], pows[t][1], fr, fi)
            qr, qi = _cmul(cre, cim, er, ei)
            q_re.append(qr)
            q_im.append(qi)
            er, ei = _cmul(pows[c - 1 - t][0], pows[c - 1 - t][1], fr, fi)
            zr, zi = _cmul(btre, btim, er, ei)
            rows = slice(t * LANES + gi * S5_GROUP, t * LANES + (gi + 1) * S5_GROUP)
            bsre_ref[0, rows, col] = zr.astype(BF16)
            bsim_ref[0, rows, col] = zi.astype(BF16)
            kr, ki = pows[t + 1]
            csre_ref[0, rows, col] = (cre * kr - cim * ki).astype(BF16)
            csim_ref[0, rows, col] = (-cre * ki - cim * kr).astype(BF16)
        q_re = jnp.concatenate(q_re, axis=0)
        q_im = jnp.concatenate(q_im, axis=0)
        m = _dg3(btre, q_re, _NT) - _dg3(btim, q_im, _NT)
        for t in range(c):
            blk = m[:, t * S5_GROUP:(t + 1) * S5_GROUP]
            if t == 0:
                blk = blk + jnp.where(r16 == c16, d_ref[0, gi], 0.0)
            tiles_ref[0, t, ch, ch] = blk.astype(BF16)
        for t in range(c + 1):
            lamre_ref[0, t, :, col] = pows[t][0]
            lamim_ref[0, t, :, col] = pows[t][1]


def s5_prep(prm, chunk):
    g, p_ = S5_GROUPS, S5_STATE
    gb = LANES // S5_GROUP
    nblk = g // gb
    kc = chunk * LANES
    grp = lambda x: x.reshape((nblk, gb) + x.shape[1:])
    args = [grp(prm['lam_re']), grp(prm['lam_im']), grp(prm['log_dt'].reshape(g, 1)),
            grp(prm['c_re']), grp(prm['c_im']),
            grp(jnp.swapaxes(prm['b_re'], 1, 2)), grp(jnp.swapaxes(prm['b_im'], 1, 2)),
            grp(prm['d'].reshape(g, S5_GROUP, 1))]
    blk = lambda x: pl.BlockSpec((1,) + x.shape[1:], lambda i: (i,) + (0,) * (x.ndim - 1))
    out_shape = [jax.ShapeDtypeStruct((nblk, chunk, LANES, LANES), BF16)] + \
        [jax.ShapeDtypeStruct((nblk, kc, gb * p_), BF16)] * 4 + \
        [jax.ShapeDtypeStruct((nblk, chunk + 1, 1, gb * p_), F32)] * 2
    return pl.pallas_call(
        functools.partial(_s5_prep_kernel, chunk=chunk),
        grid=(nblk,),
        in_specs=[blk(a) for a in args],
        out_specs=[blk(o) for o in out_shape],
        out_shape=out_shape,
        compiler_params=pltpu.CompilerParams(dimension_semantics=("parallel",)),
        name="s5_prep",
    )(*args)


def _s5_main_kernel(u_ref, tiles_ref, bsre_ref, bsim_ref, csre_ref, csim_ref, lamre_ref, lamim_ref, x0re_ref,
                    x0im_ref, y_ref, xre_ref, xim_ref, uc_scr, tp_scr, zre_scr, zim_scr, sre_scr, sim_scr,
                    xr_scr, xi_scr, *, chunk):
    c = chunk
    nb, tt, _ = u_ref.shape
    nct = tt // c
    rows = nct * nb
    j = pl.program_id(1)

    @pl.when(j == 0)
    def _():
        xr_scr[...] = x0re_ref[...]
        xi_scr[...] = x0im_ref[...]
        for d in range(c // 2):
            tp_scr[d, :LANES, :LANES] = tiles_ref[0, 2 * d]
            tp_scr[d, :LANES, LANES:] = tiles_ref[0, 2 * d + 1]
            tp_scr[d, LANES:, LANES:] = tiles_ref[0, 2 * d]
            tp_scr[d, LANES:, :LANES] = tiles_ref[0, 2 * d - 1] if d > 0 else jnp.zeros((LANES, LANES), BF16)

    lr, li = lamre_ref[0, 0], lamim_ref[0, 0]
    pair = lambda k: slice(2 * k * LANES, (2 * k + 2) * LANES)
    xr, xi = xr_scr[...], xi_scr[...]
    pieces = S5_PIECES if nct % S5_PIECES == 0 else 1
    nh = nct // pieces
    for piece in range(pieces):
        n0 = piece * nh
        hs = slice(n0 * nb, (n0 + nh) * nb)
        for t in range(c):
            if nct == 1:
                uc_scr[t] = u_ref[:, t, :]
            else:
                for b in range(nb):
                    uc_scr[t, pl.ds(n0 * nb + b, nh, stride=nb), :] = u_ref[b, pl.ds(n0 * c + t, nh, stride=c), :]
        u = jnp.concatenate([uc_scr[t, hs, :].astype(BF16) for t in range(c)], axis=1)
        zre_scr[hs, :] = jnp.dot(u, bsre_ref[0], preferred_element_type=F32)
        zim_scr[hs, :] = jnp.dot(u, bsim_ref[0], preferred_element_type=F32)

        for k in range(c // 2):
            y = jnp.dot(u[:, pair(0)], tp_scr[k], preferred_element_type=F32)
            for kin in range(1, k + 1):
                y = y + jnp.dot(u[:, pair(kin)], tp_scr[k - kin], preferred_element_type=F32)
            uc_scr[2 * k, hs, :] = y[:, :LANES]
            uc_scr[2 * k + 1, hs, :] = y[:, LANES:]

        for i in range(n0, n0 + nh):
            rs = slice(i * nb, (i + 1) * nb)
            sre_scr[rs, :] = xr
            sim_scr[rs, :] = xi
            xr, xi = lr * xr - li * xi + zre_scr[rs, :], li * xr + lr * xi + zim_scr[rs, :]
        sre = sre_scr[hs, :].astype(BF16)
        sim = sim_scr[hs, :].astype(BF16)
        for k in range(c // 2):
            y = (lax.dot_general(sre, csre_ref[0, pair(k), :], _NT, preferred_element_type=F32)
                 + lax.dot_general(sim, csim_ref[0, pair(k), :], _NT, preferred_element_type=F32))
            uc_scr[2 * k, hs, :] = jax.nn.gelu(uc_scr[2 * k, hs, :] + y[:, :LANES])
            uc_scr[2 * k + 1, hs, :] = jax.nn.gelu(uc_scr[2 * k + 1, hs, :] + y[:, LANES:])
        for t in range(c):
            if nct == 1:
                y_ref[:, t, :] = uc_scr[t]
            else:
                for b in range(nb):
                    y_ref[b, pl.ds(n0 * c + t, nh, stride=c), :] = uc_scr[t, pl.ds(n0 * nb + b, nh, stride=nb), :]
    xr_scr[...] = xr
    xi_scr[...] = xi

    @pl.when(j == pl.num_programs(1) - 1)
    def _():
        xre_ref[...] = xr
        xim_ref[...] = xi


def s5_mix(u, x0re, x0im, tables, chunk):
    b, t, e = u.shape
    tiles, bsre, bsim, csre, csim, lamre, lamim = tables
    nblk = e // LANES
    gp = (LANES // S5_GROUP) * S5_STATE
    nct = min(t // chunk, max(1, S5_ROWS // b))
    assert (t // chunk) % nct == 0
    tt = nct * chunk
    rows = nct * b
    kc = chunk * LANES
    last = tiles.shape[1] // chunk - 1
    assert tiles.shape[1] % chunk == 0
    bs_spec = pl.BlockSpec((1, kc, gp), lambda i, j: (i, last, 0))
    cs_spec = pl.BlockSpec((1, kc, gp), lambda i, j: (i, 0, 0))
    lam_spec = pl.BlockSpec((1, 1, 1, gp), lambda i, j: (i, chunk, 0, 0))
    y, xre, xim = pl.pallas_call(
        functools.partial(_s5_main_kernel, chunk=chunk),
        grid=(nblk, t // tt),
        in_specs=[pl.BlockSpec((b, tt, LANES), lambda i, j: (0, j, i)),
                  pl.BlockSpec((1, chunk, LANES, LANES), lambda i, j: (i, 0, 0, 0)),
                  bs_spec, bs_spec, cs_spec, cs_spec, lam_spec, lam_spec,
                  pl.BlockSpec((b, gp), lambda i, j: (0, i)), pl.BlockSpec((b, gp), lambda i, j: (0, i))],
        out_specs=[pl.BlockSpec((b, tt, LANES), lambda i, j: (0, j, i)),
                   pl.BlockSpec((b, gp), lambda i, j: (0, i)), pl.BlockSpec((b, gp), lambda i, j: (0, i))],
        out_shape=[jax.ShapeDtypeStruct((b, t, e), F32),
                   jax.ShapeDtypeStruct((b, S5_GROUPS * S5_STATE), F32),
                   jax.ShapeDtypeStruct((b, S5_GROUPS * S5_STATE), F32)],
        scratch_shapes=[pltpu.VMEM((chunk, rows, LANES), F32), pltpu.VMEM((chunk // 2, 2 * LANES, 2 * LANES), BF16)]
        + [pltpu.VMEM((rows, gp), F32)] * 4 + [pltpu.VMEM((b, gp), F32)] * 2,
        compiler_params=pltpu.CompilerParams(dimension_semantics=("parallel", "arbitrary"),
                                             vmem_limit_bytes=VMEM_LIMIT),
        name="s5_main",
    )(u, tiles, bsre, bsim, csre, csim, lamre, lamim, x0re.reshape(b, -1), x0im.reshape(b, -1))
    return y, xre.reshape(b, S5_GROUPS, S5_STATE), xim.reshape(b, S5_GROUPS, S5_STATE)


def _trunk(x, st_shift, st_wkv, st_gla, st_re, st_im, w):
    b, t, d = x.shape
    m = b * t
    x2 = x.reshape(m, d)

    p_r, p_g, p_xa = norm_proj(x2, w['norm_mix_pre'][0], [w['w_in_rwkv'], w['w_in_gla'], w['w_in_xa']],
                               [BF16, BF16, BF16])
    o_r, wkv = rwkv_mix(p_r.reshape(b, t, -1), st_shift[0], st_wkv, w['rwkv'])
    o_g, gla = gla_mix(p_g.reshape(b, t, -1), p_xa.reshape(b, t, -1), st_gla[0], w['gla_wa2'], w['gla_ba'],
                       w['gla_norm'])
    shift = p_r.reshape(b, t, -1)[:, -1].astype(F32)
    x2 = proj_post([o_r.reshape(m, -1), o_g.reshape(m, -1)], [w['w_out_rwkv'], w['w_out_gla']],
                   w['norm_mix_post'][0], x2)
    x2 = ffn(x2, w['norm_ffn_pre'][0], w['w_ff_up'][0], w['w_ff_down'][0], w['norm_ffn_post'][0])

    (u,) = norm_proj(x2, w['norm_mix_pre'][1], [w['s5_w_in']], [F32])
    chunk = min(S5_CHUNK, t)
    y, s_re, s_im = s5_mix(u.reshape(b, t, d), st_re[0], st_im[0], w['s5_tables'], chunk)
    x2 = proj_post([y.reshape(m, d)], [w['s5_w_out']], w['norm_mix_post'][1], x2, glu=True)
    x2 = ffn(x2, w['norm_ffn_pre'][1], w['w_ff_up'][1], w['w_ff_down'][1], w['norm_ffn_post'][1])
    return x2.reshape(b, t, d), shift[None], wkv[None], gla[None], s_re[None], s_im[None]


def kernel(x_prompt, x_sample, state_rwkv_shift, state_rwkv_wkv, state_gla, state_s5_re, state_s5_im,
           norm_mix_pre, norm_mix_post, norm_ffn_pre, norm_ffn_post,
           w_mix_in, w_mix_out, rwkv_mu, rwkv_w0, rwkv_w2, rwkv_a0, rwkv_a2, rwkv_g2,
           rwkv_k_k, rwkv_k_a, rwkv_r_k, rwkv_gn_g, rwkv_gn_b, gla_wa2, gla_ba, gla_norm,
           s5_w_in, s5_lam_re, s5_lam_im, s5_log_dt, s5_b_re, s5_b_im, s5_c_re, s5_c_im, s5_d, s5_w_out,
           w_ff_up, w_ff_down):
    assert norm_mix_pre.shape[0] == 2, "two layers: one RWKV-7/GLA layer, one S5 layer"
    bf = lambda a: a.astype(BF16)
    w_in = w_mix_in[0]
    gla0 = RWKV_PROJ
    xa0 = gla0 + 2 * GLA_K + GLA_V
    w_in_gla = jnp.concatenate([w_in[:, gla0:xa0], w_in[:, xa0 + GLA_LORA:]], axis=1)
    w_in_xa = jnp.pad(w_in[:, xa0:xa0 + GLA_LORA], ((0, 0), (0, LANES - GLA_LORA)))
    s5_prm = dict(lam_re=s5_lam_re[0], lam_im=s5_lam_im[0], log_dt=s5_log_dt[0], b_re=s5_b_re[0],
                  b_im=s5_b_im[0], c_re=s5_c_re[0], c_im=s5_c_im[0], d=s5_d[0])
    tables = s5_prep(s5_prm, min(S5_CHUNK, max(x_prompt.shape[1], x_sample.shape[1])))
    w = dict(
        norm_mix_pre=norm_mix_pre, norm_mix_post=norm_mix_post, norm_ffn_pre=norm_ffn_pre,
        norm_ffn_post=norm_ffn_post,
        w_in_rwkv=bf(w_in[:, :RWKV_PROJ]), w_in_gla=bf(w_in_gla), w_in_xa=bf(w_in_xa),
        w_out_rwkv=bf(w_mix_out[0, :RWKV_W]), w_out_gla=bf(w_mix_out[0, RWKV_W:]),
        rwkv=dict(mu=rwkv_mu[0], w0=rwkv_w0[0], w2=rwkv_w2[0], a0=rwkv_a0[0], a2=rwkv_a2[0], g2=rwkv_g2[0],
                  k_k=rwkv_k_k[0], k_a=rwkv_k_a[0], r_k=rwkv_r_k[0], gn_g=rwkv_gn_g[0], gn_b=rwkv_gn_b[0]),
        gla_wa2=bf(jnp.pad(gla_wa2[0], ((0, LANES - GLA_LORA), (0, 0)))), gla_ba=gla_ba[0], gla_norm=gla_norm[0],
        s5_w_in=bf(s5_w_in[0]), s5_w_out=bf(s5_w_out[0]), s5_tables=tables,
        w_ff_up=[bf(w_ff_up[l]) for l in range(2)], w_ff_down=[bf(w_ff_down[l]) for l in range(2)],
    )
    bp = x_prompt.shape[0]
    zeros = lambda s: jnp.zeros((s.shape[0], bp) + s.shape[2:], x_prompt.dtype)
    y_p, sh_p, wkv_p, gla_p, re_p, im_p = _trunk(x_prompt, zeros(state_rwkv_shift), zeros(state_rwkv_wkv),
                                                 zeros(state_gla), zeros(state_s5_re), zeros(state_s5_im), w)
    y_s, sh_s, wkv_s, gla_s, re_s, im_s = _trunk(x_sample, state_rwkv_shift, state_rwkv_wkv, state_gla,
                                                 state_s5_re, state_s5_im, w)
    return (y_p, y_s, sh_p, sh_s, wkv_p, wkv_s, gla_p, gla_s, re_p, re_s, im_p, im_s)
```

```python
import functools
import math

import jax
import jax.numpy as jnp
from jax import lax
from jax.experimental import pallas as pl
from jax.experimental.pallas import tpu as pltpu

F32 = jnp.float32
BF16 = jnp.bfloat16

D_MODEL = 1024
NORM_EPS = 1e-6
RWKV_HEADS = 8
RWKV_HEAD = 64
RWKV_W = RWKV_HEADS * RWKV_HEAD
RWKV_PROJ = 1792
RWKV_CHUNK = 64
RWKV_ROWS = 64
RWKV_STEP_ROWS = 256
GN_EPS = 64e-5
GLA_HEADS = 4
GLA_DK = 64
GLA_DV = 128
GLA_K = GLA_HEADS * GLA_DK
GLA_V = GLA_HEADS * GLA_DV
GLA_LORA = 16
GLA_TAU = 16.0
GLA_CHUNK = 32
GLA_MAIN = 2 * GLA_K + 2 * GLA_V
GLA_TILE = 256
GLA_TILE_SEQS = 8
GLA_SMALL_ROWS = 128
S5_GROUPS = 64
S5_GROUP = 16
S5_STATE = 64
S5_CHUNK = 16
S5_ROWS = 512
S5_PIECES = 2
LANES = 128
DENSE_HALVES = 2
VMEM_LIMIT = 48 * 1024 * 1024
FFN_VMEM_LIMIT = 56 * 1024 * 1024

_NN = (((1,), (0,)), ((), ()))
_NT = (((1,), (1,)), ((), ()))
_TN = (((0,), (0,)), ((), ()))


def _dg(a, b, dn=_NN):
    return lax.dot_general(a.astype(BF16), b.astype(BF16), dn, preferred_element_type=F32)


def _split2(x):
    hi = x.astype(BF16)
    lo = (x - hi.astype(F32)).astype(BF16)
    return hi, lo


def _split3(x):
    hi = x.astype(BF16)
    r1 = x - hi.astype(F32)
    mid = r1.astype(BF16)
    lo = (r1 - mid.astype(F32)).astype(BF16)
    return hi, mid, lo


def _dg3(a, b, dn=_NN):
    ah, al = _split2(a)
    bh, bl = _split2(b)
    d = lambda x, y: lax.dot_general(x, y, dn, preferred_element_type=F32)
    return d(ah, bh) + (d(ah, bl) + d(al, bh))


def _dg_sel(sel, x, dn=_NN, terms=3):
    d = lambda y: lax.dot_general(sel, y, dn, preferred_element_type=F32)
    if terms == 2:
        h, l = _split2(x)
        return d(h) + d(l)
    h, m, l = _split3(x)
    return d(h) + (d(m) + d(l))


def _dg_selr(x, sel, dn=_NN):
    d = lambda y: lax.dot_general(y, sel, dn, preferred_element_type=F32)
    h, m, l = _split3(x)
    return d(h) + (d(m) + d(l))


def _rms(x, g):
    return x * lax.rsqrt(jnp.mean(x * x, axis=-1, keepdims=True) + NORM_EPS) * g


def _row_tile(m, want):
    t = min(m, want)
    assert m % t == 0, (m, t)
    return t


def _full(shape):
    nd = len(shape)
    return pl.BlockSpec(shape, lambda *_: (0,) * nd)


def _token_blocks(b, t, nb, tt):
    if tt == t:
        return (b // nb, nb * t), (1, nb * t), (lambda i, j: (i, 0, 0))
    return (b, t), (nb, tt), (lambda i, j: (i, j, 0))


def _halves(tm):
    hm = tm // DENSE_HALVES
    return [slice(i * hm, (i + 1) * hm) for i in range(DENSE_HALVES)]


def _norm_proj_kernel(x_ref, g_ref, *refs):
    n = len(refs) // 2
    for rs in _halves(x_ref.shape[0]):
        h = _rms(x_ref[rs, :], g_ref[...]).astype(BF16)
        for w_ref, o_ref in zip(refs[:n], refs[n:]):
            o_ref[rs, :] = jnp.dot(h, w_ref[...], preferred_element_type=F32).astype(o_ref.dtype)


def norm_proj(x, g, ws, out_dtypes, tm=1024):
    m, d = x.shape
    tm = _row_tile(m, tm)
    once = dict(pipeline_mode=pl.Buffered(1))
    return pl.pallas_call(
        _norm_proj_kernel,
        grid=(m // tm,),
        in_specs=[pl.BlockSpec((tm, d), lambda i: (i, 0)), _full((1, d))]
        + [pl.BlockSpec(w.shape, lambda i: (0, 0), **once) for w in ws],
        out_specs=[pl.BlockSpec((tm, w.shape[1]), lambda i: (i, 0)) for w in ws],
        out_shape=[jax.ShapeDtypeStruct((m, w.shape[1]), dt) for w, dt in zip(ws, out_dtypes)],
        compiler_params=pltpu.CompilerParams(dimension_semantics=("parallel",), vmem_limit_bytes=VMEM_LIMIT),
        name="norm_proj",
    )(x, g.reshape(1, d), *ws)


def _proj_post_kernel(*refs, n_in, glu):
    a_refs = refs[:n_in]
    w_refs = refs[n_in:2 * n_in]
    g_ref, x_ref, o_ref = refs[2 * n_in:]
    for rs in _halves(x_ref.shape[0]):
        m = None
        for a_ref, w_ref in zip(a_refs, w_refs):
            t = jnp.dot(a_ref[rs, :].astype(BF16), w_ref[...], preferred_element_type=F32)
            m = t if m is None else m + t
        if glu:
            m = m[:, :D_MODEL] * jax.nn.sigmoid(m[:, D_MODEL:])
        o_ref[rs, :] = x_ref[rs, :] + _rms(m, g_ref[...])


def proj_post(a_list, w_list, g, x, glu=False, tm=1024):
    m, d = x.shape
    tm = _row_tile(m, tm)
    n_in = len(a_list)
    once = dict(pipeline_mode=pl.Buffered(1))
    return pl.pallas_call(
        functools.partial(_proj_post_kernel, n_in=n_in, glu=glu),
        grid=(m // tm,),
        in_specs=[pl.BlockSpec((tm, a.shape[1]), lambda i: (i, 0)) for a in a_list]
        + [pl.BlockSpec(w.shape, lambda i: (0, 0), **once) for w in w_list]
        + [_full((1, d)), pl.BlockSpec((tm, d), lambda i: (i, 0))],
        out_specs=pl.BlockSpec((tm, d), lambda i: (i, 0)),
        out_shape=jax.ShapeDtypeStruct((m, d), F32),
        compiler_params=pltpu.CompilerParams(dimension_semantics=("parallel",), vmem_limit_bytes=VMEM_LIMIT),
        name="proj_post",
    )(*a_list, *w_list, g.reshape(1, d), x)


def _ffn_kernel(x_ref, gpre_ref, wup_ref, wdn_ref, gpost_ref, o_ref, *, tf):
    f = wup_ref.shape[1]
    for rs in _halves(x_ref.shape[0]):
        x = x_ref[rs, :]
        h = _rms(x, gpre_ref[...]).astype(BF16)
        acc = None
        for j in range(f // tf):
            u = jnp.dot(h, wup_ref[:, j * tf:(j + 1) * tf], preferred_element_type=F32)
            u = jnp.square(jnp.maximum(u, 0.0)).astype(BF16)
            t = jnp.dot(u, wdn_ref[j * tf:(j + 1) * tf, :], preferred_element_type=F32)
            acc = t if acc is None else acc + t
        o_ref[rs, :] = x + _rms(acc, gpost_ref[...])


def ffn(x, g_pre, w_up, w_down, g_post, tm=1024, tf=512):
    m, d = x.shape
    f = w_up.shape[1]
    tm = _row_tile(m, tm)
    once = dict(pipeline_mode=pl.Buffered(1))
    return pl.pallas_call(
        functools.partial(_ffn_kernel, tf=tf),
        grid=(m // tm,),
        in_specs=[
            pl.BlockSpec((tm, d), lambda i: (i, 0)),
            _full((1, d)),
            pl.BlockSpec((d, f), lambda i: (0, 0), **once),
            pl.BlockSpec((f, d), lambda i: (0, 0), **once),
            _full((1, d)),
        ],
        out_specs=pl.BlockSpec((tm, d), lambda i: (i, 0)),
        out_shape=jax.ShapeDtypeStruct((m, d), F32),
        compiler_params=pltpu.CompilerParams(dimension_semantics=("parallel",), vmem_limit_bytes=FFN_VMEM_LIMIT),
        name="ffn",
    )(x, g_pre.reshape(1, d), w_up, w_down, g_post.reshape(1, d))


def _head_sum(x, ones_bd):
    return jnp.dot(x.astype(BF16), ones_bd, preferred_element_type=F32)


def _rwkv_kernel(p_ref, sh0_ref, s0_ref, mu_ref, w0_ref, w2_ref, a0_ref, a2_ref, g2_ref, kk_ref, ka_ref,
                 rk_ref, gng_ref, gnb_ref, ones_ref, o_ref, sout_ref, state_scr, carry_scr, y_scr, *, chunk):
    c = chunk
    nb = s0_ref.shape[0]
    rows = nb * c
    shift = int(math.log2(c))
    n = pl.program_id(1)

    @pl.when(n == 0)
    def _():
        carry_scr[...] = sh0_ref[...]
        for i in range(nb):
            for h in range(RWKV_HEADS):
                state_scr[i, h // 2, :, (h % 2) * RWKV_HEAD:(h % 2 + 1) * RWKV_HEAD] = s0_ref[i, h]

    p3 = p_ref[...]
    p = p3.astype(F32).reshape(rows, RWKV_PROJ)
    t_idx = jnp.bitwise_and(lax.broadcasted_iota(jnp.int32, (rows, 1), 0), c - 1)
    carry = jnp.broadcast_to(carry_scr[...], (nb, c, RWKV_PROJ)).reshape(rows, RWKV_PROJ)
    prev = jnp.where(t_idx == 0, carry, pltpu.roll(p, 1, 0))
    if p3.shape[0] == nb:
        carry_scr[...] = p3[:, c - 1:c, :].astype(F32)
    z = p + (prev - p) * mu_ref[...]

    w_ = RWKV_W
    r = z[:, 0:w_]
    k = z[:, w_:2 * w_]
    v = z[:, 2 * w_:3 * w_]
    lw, la = w2_ref.shape[0], a2_ref.shape[0]
    xw = z[:, 3 * w_:3 * w_ + lw]
    xa = z[:, 3 * w_ + lw:3 * w_ + lw + la]
    xg = z[:, 3 * w_ + lw + la:]

    wlog = -jax.nn.softplus(-(w0_ref[...] + _dg(jnp.tanh(xw), w2_ref[...]))) - 0.5
    logd = -jnp.exp(wlog)
    a = jax.nn.sigmoid(a0_ref[...] + _dg(xa, a2_ref[...]))
    g = _dg(jax.nn.sigmoid(xg), g2_ref[...])

    ones_bd = ones_ref[...]
    kk = k * kk_ref[...]
    kk = kk * lax.rsqrt(jnp.maximum(_head_sum(kk * kk, ones_bd), 1e-24))
    k2 = k * (1.0 + (a - 1.0) * ka_ref[...])
    bb = kk * a

    ra = lax.broadcasted_iota(jnp.int32, (rows, rows), 0)
    ca = lax.broadcasted_iota(jnp.int32, (rows, rows), 1)
    same_all = lax.shift_right_logical(ra, shift) == lax.shift_right_logical(ca, shift)
    lower_eq_all = jnp.logical_and(same_all, ra >= ca)
    cum = _dg_sel(lower_eq_all.astype(BF16), logd, terms=2)
    tot = _dg_sel(same_all.astype(BF16), logd, terms=2)
    g_in = jnp.exp(cum)
    g_ex = jnp.exp(cum - logd)
    g_inv = jnp.exp(-cum)
    g_hat = jnp.exp(tot - cum)
    g_all = jnp.exp(tot)
    rt = r * g_in
    kt = kk * g_ex
    ktil = k2 * g_inv
    btil = bb * g_inv
    khat = k2 * g_hat
    bhat = bb * g_hat

    gr = RWKV_ROWS
    assert rows % gr == 0 and gr == RWKV_HEAD and gr % c == 0
    spg = gr // c
    hp = 2 * RWKV_HEAD
    ri = lax.broadcasted_iota(jnp.int32, (gr, hp), 0)
    si = jnp.bitwise_and(lax.broadcasted_iota(jnp.int32, (gr, hp), 1), RWKV_HEAD - 1)
    same = lax.shift_right_logical(ri, shift) == lax.shift_right_logical(si, shift)
    lower = jnp.logical_and(same, ri > si)
    lower_eq = jnp.logical_and(same, ri >= si)
    eye = (ri == si).astype(F32)
    bi = lax.broadcasted_iota(jnp.int32, (hp, hp), 0)
    bj = lax.broadcasted_iota(jnp.int32, (hp, hp), 1)
    head_bits = int(math.log2(RWKV_HEAD))
    on_diag = lax.shift_right_logical(bi, head_bits) == lax.shift_right_logical(bj, head_bits)
    first_head = lax.broadcasted_iota(jnp.int32, (RWKV_HEAD, hp), 1) < RWKV_HEAD

    def bd(x2):
        x2 = x2.astype(BF16)
        return jnp.where(on_diag, jnp.concatenate([x2, x2], axis=0), jnp.zeros((), BF16))

    probs = [(g_, q) for g_ in range(rows // gr) for q in range(RWKV_HEADS // 2)]
    np_ = len(probs)
    psl = lambda q: slice(q * hp, (q + 1) * hp)
    gsl = lambda g_: slice(g_ * gr, (g_ + 1) * gr)
    pick = lambda x, g_, q: x[gsl(g_), psl(q)]
    seq_rows = lambda g_, i: slice(g_ * gr + i * c, g_ * gr + (i + 1) * c)
    cat = lambda xs: jnp.concatenate(xs, axis=0) if len(xs) > 1 else xs[0]
    kr = [jnp.concatenate([pick(kt, *q), pick(rt, *q)], axis=0) for q in probs]
    gb = [_dg(kr[j], bd(pick(btil, *q)), _NT) for j, q in enumerate(probs)]
    gk = [_dg(kr[j], bd(pick(ktil, *q)), _NT) for j, q in enumerate(probs)]
    a_b = [jnp.where(lower, x[:gr], 0.0) for x in gb]
    a_rb = [jnp.where(lower_eq, x[gr:], 0.0) for x in gb]
    a_kk = [jnp.concatenate([jnp.where(lower, x[:gr], 0.0), jnp.where(lower_eq, x[gr:], 0.0)], axis=0)
            for x in gk]
    ks = [[_dg(jnp.concatenate([kt[seq_rows(g_, i), psl(q)], rt[seq_rows(g_, i), psl(q)]], axis=0),
               bd(state_scr[g_ * spg + i, q]), _NT) for i in range(spg)] for g_, q in probs]
    x0 = [cat([x[:c] for x in ks[j]]) for j in range(np_)]
    r0 = [cat([x[c:] for x in ks[j]]) for j in range(np_)]
    av = [_dg(a_kk[j], bd(pick(v, *q))) for j, q in enumerate(probs)]
    tinv = [eye - x for x in a_b]
    pw = [_dg(x, bd(x)) for x in a_b]
    span = 2
    while span < c:
        span *= 2
        if span < c:
            pr = [_dg(jnp.concatenate([t_, p_], axis=0), bd(p_)) for t_, p_ in zip(tinv, pw)]
            tinv = [t_ + x[:gr] for t_, x in zip(tinv, pr)]
            pw = [x[gr:] for x in pr]
        else:
            tinv = [t_ + _dg(t_, bd(p_)) for t_, p_ in zip(tinv, pw)]
    u = [_dg(tinv[j], bd(x0[j] + av[j][:gr])) for j in range(np_)]
    for j, (g_, q) in enumerate(probs):
        y_scr[gsl(g_), psl(q)] = r0[j] + av[j][gr:] - _dg(a_rb[j], bd(u[j]))
    for j, (g_, q) in enumerate(probs):
        for i in range(spg):
            rs = seq_rows(g_, i)
            vu = jnp.concatenate([v[rs, psl(q)], -u[j][i * c:(i + 1) * c]], axis=0)
            kb = jnp.concatenate([khat[rs, psl(q)], bhat[rs, psl(q)]], axis=0)
            full = _dg(vu, kb, _TN)
            upd = jnp.where(first_head, full[:RWKV_HEAD], full[RWKV_HEAD:])
            s_i = g_ * spg + i
            state_scr[s_i, q] = state_scr[s_i, q] * g_all[rs.start:rs.start + 1, psl(q)] + upd

    y = y_scr[...]
    inv_n = 1.0 / RWKV_HEAD
    mean = _head_sum(y, ones_bd) * inv_n
    yc = y - mean
    var = _head_sum(yc * yc, ones_bd) * inv_n
    yn = yc * lax.rsqrt(var + GN_EPS) * gng_ref[...] + gnb_ref[...]
    bonus = _head_sum(r * k2 * rk_ref[...], ones_bd) * v
    o_ref[...] = ((yn + bonus) * g).astype(o_ref.dtype).reshape(o_ref.shape)

    @pl.when(n == pl.num_programs(1) - 1)
    def _():
        for i in range(nb):
            for h in range(RWKV_HEADS):
                sout_ref[i, h] = state_scr[i, h // 2, :, (h % 2) * RWKV_HEAD:(h % 2 + 1) * RWKV_HEAD]


def rwkv_mix(p, shift0, s0, prm, layer=0):
    b, t, _ = p.shape
    c = min(RWKV_CHUNK, t)
    assert t % c == 0
    nb = min(b, RWKV_STEP_ROWS // c)
    assert b % nb == 0
    w_ = RWKV_W
    row = lambda x: x.reshape(1, -1)
    ones_bd = jnp.kron(jnp.eye(RWKV_HEADS, dtype=F32), jnp.ones((RWKV_HEAD, RWKV_HEAD), F32)).astype(BF16)
    consts = [row(prm['mu']), row(prm['w0']), prm['w2'].astype(BF16), row(prm['a0']), prm['a2'].astype(BF16),
              prm['g2'].astype(BF16), row(prm['k_k']), row(prm['k_a']), row(prm['r_k']), row(prm['gn_g']),
              row(prm['gn_b']), ones_bd]
    tok_shape, tok_block, tok_index = _token_blocks(b, t, nb, c)
    o, s_out = pl.pallas_call(
        functools.partial(_rwkv_kernel, chunk=c),
        grid=(b // nb, t // c),
        in_specs=[
            pl.BlockSpec(tok_block + (RWKV_PROJ,), tok_index),
            pl.BlockSpec((nb, 1, RWKV_PROJ), lambda i, j: (i, 0, 0)),
            pl.BlockSpec((None, nb, RWKV_HEADS, RWKV_HEAD, RWKV_HEAD), lambda i, j: (layer, i, 0, 0, 0)),
        ] + [_full(x.shape) for x in consts],
        out_specs=[
            pl.BlockSpec(tok_block + (w_,), tok_index),
            pl.BlockSpec((nb, RWKV_HEADS, RWKV_HEAD, RWKV_HEAD), lambda i, j: (i, 0, 0, 0)),
        ],
        out_shape=[jax.ShapeDtypeStruct(tok_shape + (w_,), BF16),
                   jax.ShapeDtypeStruct((b, RWKV_HEADS, RWKV_HEAD, RWKV_HEAD), F32)],
        scratch_shapes=[pltpu.VMEM((nb, RWKV_HEADS // 2, RWKV_HEAD, 2 * RWKV_HEAD), F32),
                        pltpu.VMEM((nb, 1, RWKV_PROJ), F32),
                        pltpu.VMEM((nb * c, w_), F32)],
        compiler_params=pltpu.CompilerParams(dimension_semantics=("parallel", "arbitrary")),
        name="rwkv_mix",
    )(p.reshape(tok_shape + (RWKV_PROJ,)), shift0.reshape(b, 1, RWKV_PROJ), s0, *consts)
    return o.reshape(b, t, w_), s_out


def _gla_kernel(pg_ref, xa_ref, s0_ref, wa2_ref, ba_ref, gn_ref, o_ref, sout_ref, state_scr, o_scr, *, chunk):
    c = chunk
    nb = s0_ref.shape[0]
    rows = pg_ref.shape[0] * pg_ref.shape[1]
    tt = rows // nb
    nc = tt // c
    n = pl.program_id(1)
    heads = range(GLA_HEADS)
    seqs = range(nb)
    ident = lambda m: (lax.broadcasted_iota(jnp.int32, (m, m), 0)
                       == lax.broadcasted_iota(jnp.int32, (m, m), 1)).astype(BF16)

    @pl.when(n == 0)
    def _():
        eye_k = ident(GLA_DK)
        for i in seqs:
            for h in heads:
                state_scr[i, h] = _dg_selr(s0_ref[i, h], eye_k, _TN)

    pg = pg_ref[...].astype(F32).reshape(rows, GLA_MAIN)
    q = pg[:, 0:GLA_K] * (GLA_DK ** -0.5)
    k = pg[:, GLA_K:2 * GLA_K]
    v = pg[:, 2 * GLA_K:2 * GLA_K + GLA_V]
    gz = pg[:, 2 * GLA_K + GLA_V:]
    xa = xa_ref[...].astype(F32).reshape(rows, LANES)
    gk = jax.nn.log_sigmoid(_dg(xa, wa2_ref[...]) + ba_ref[...]) * (1.0 / GLA_TAU)

    gsz = min(rows, GLA_TILE)
    groups = [slice(g * gsz, (g + 1) * gsz) for g in range(rows // gsz)]
    shift = int(math.log2(c))
    ri = lax.broadcasted_iota(jnp.int32, (gsz, gsz), 0)
    ci = lax.broadcasted_iota(jnp.int32, (gsz, gsz), 1)
    same = lax.shift_right_logical(ri, shift) == lax.shift_right_logical(ci, shift)
    causal = jnp.logical_and(same, ri >= ci)
    cat = lambda xs: jnp.concatenate(xs, axis=0) if len(xs) > 1 else xs[0]
    bc = cat([_dg_sel(causal.astype(BF16), gk[rg, :], terms=2) for rg in groups])
    ends = [bc[(j + 1) * c - 1:(j + 1) * c, :] for j in range(rows // c)]
    bl = jnp.concatenate([jnp.broadcast_to(e, (c, GLA_K)) for e in ends], axis=0)
    qt = q * jnp.exp(bc)
    kt = k * jnp.exp(-bc)
    ks = k * jnp.exp(bl - bc)

    ksl = [slice(h * GLA_DK, (h + 1) * GLA_DK) for h in heads]
    vsl = [slice(h * GLA_DV, (h + 1) * GLA_DV) for h in heads]
    chunks = [(i, slice(i * tt + j * c, i * tt + (j + 1) * c)) for i in seqs for j in range(nc)]
    att = [[jnp.where(causal, _dg(qt[rg, ksl[h]], kt[rg, ksl[h]], _NT), 0.0) for rg in groups] for h in heads]
    o_intra = [cat([_dg(att[h][g], v[rg, vsl[h]]) for g, rg in enumerate(groups)]) for h in heads]
    kv = [[_dg(v[rs, vsl[h]], ks[rs, ksl[h]], _TN) for _, rs in chunks] for h in heads]
    st = [[state_scr[i, h] for i in seqs] for h in heads]
    for ci_, (i, rs) in enumerate(chunks):
        for h in heads:
            o_scr[rs, vsl[h]] = o_intra[h][rs] + _dg(qt[rs, ksl[h]], st[h][i], _NT)
            st[h][i] = st[h][i] * jnp.exp(ends[ci_][:, ksl[h]]) + kv[h][ci_]
    for h in heads:
        for i in seqs:
            state_scr[i, h] = st[h][i]

    for h in heads:
        o_h = o_scr[:, vsl[h]]
        o_h = o_h * lax.rsqrt(jnp.mean(o_h * o_h, axis=-1, keepdims=True) + NORM_EPS) * gn_ref[...]
        o_scr[:, vsl[h]] = o_h * jax.nn.silu(gz[:, vsl[h]])
    o_ref[...] = o_scr[...].astype(o_ref.dtype).reshape(o_ref.shape)

    @pl.when(n == pl.num_programs(1) - 1)
    def _():
        eye_v = ident(GLA_DV)
        for i in seqs:
            for h in heads:
                sout_ref[i, h] = _dg_selr(state_scr[i, h], eye_v, _TN)


def gla_mix(pg, pxa, s0, wa2_pad, ba, g_norm):
    b, t, _ = pg.shape
    tt = min(GLA_TILE, t)
    c = min(GLA_CHUNK, t)
    nb = min(b, max(1, (GLA_TILE_SEQS * GLA_TILE if tt == GLA_TILE else GLA_SMALL_ROWS) // tt))
    assert t % tt == 0 and tt % c == 0 and b % nb == 0
    tok_shape, tok_block, tok_index = _token_blocks(b, t, nb, tt)
    o, s_out = pl.pallas_call(
        functools.partial(_gla_kernel, chunk=c),
        grid=(b // nb, t // tt),
        in_specs=[
            pl.BlockSpec(tok_block + (GLA_MAIN,), tok_index),
            pl.BlockSpec(tok_block + (LANES,), tok_index),
            pl.BlockSpec((nb, GLA_HEADS, GLA_DK, GLA_DV), lambda i, j: (i, 0, 0, 0)),
            _full(wa2_pad.shape), _full((1, GLA_K)), _full((1, GLA_DV)),
        ],
        out_specs=[
            pl.BlockSpec(tok_block + (GLA_V,), tok_index),
            pl.BlockSpec((nb, GLA_HEADS, GLA_DK, GLA_DV), lambda i, j: (i, 0, 0, 0)),
        ],
        out_shape=[jax.ShapeDtypeStruct(tok_shape + (GLA_V,), BF16),
                   jax.ShapeDtypeStruct((b, GLA_HEADS, GLA_DK, GLA_DV), F32)],
        scratch_shapes=[pltpu.VMEM((nb, GLA_HEADS, GLA_DV, GLA_DK), F32), pltpu.VMEM((nb * tt, GLA_V), F32)],
        compiler_params=pltpu.CompilerParams(dimension_semantics=("parallel", "arbitrary")),
        name="gla_mix",
    )(pg.reshape(tok_shape + (GLA_MAIN,)), pxa.reshape(tok_shape + (LANES,)), s0, wa2_pad,
      ba.reshape(1, GLA_K), g_norm.reshape(1, GLA_DV))
    return o.reshape(b, t, GLA_V), s_out


def _cmul(ar, ai, br, bi):
    return ar * br - ai * bi, ar * bi + ai * br


def _s5_prep_kernel(lre_ref, lim_ref, ldt_ref, cre_ref, cim_ref, btre_ref, btim_ref, d_ref,
                    tiles_ref, bsre_ref, bsim_ref, csre_ref, csim_ref, lamre_ref, lamim_ref, *, chunk):
    c = chunk
    p_ = S5_STATE
    gb = LANES // S5_GROUP
    tiles_ref[...] = jnp.zeros_like(tiles_ref)
    bsre_ref[...] = jnp.zeros_like(bsre_ref)
    bsim_ref[...] = jnp.zeros_like(bsim_ref)
    csre_ref[...] = jnp.zeros_like(csre_ref)
    csim_ref[...] = jnp.zeros_like(csim_ref)
    r16 = lax.broadcasted_iota(jnp.int32, (S5_GROUP, S5_GROUP), 0)
    c16 = lax.broadcasted_iota(jnp.int32, (S5_GROUP, S5_GROUP), 1)
    for gi in range(gb):
        lr = lre_ref[0, gi:gi + 1, :]
        li = lim_ref[0, gi:gi + 1, :]
        dt = jnp.exp(ldt_ref[0, gi:gi + 1, :])
        mag = jnp.exp(lr * dt)
        ang = li * dt
        abr, abi = mag * jnp.cos(ang), mag * jnp.sin(ang)
        den = lr * lr + li * li
        fr = ((abr - 1.0) * lr + abi * li) / den
        fi = (abi * lr - (abr - 1.0) * li) / den
        pows = [(jnp.ones_like(abr), jnp.zeros_like(abr))]
        for _ in range(c):
            pows.append(_cmul(pows[-1][0], pows[-1][1], abr, abi))
        cre, cim = cre_ref[0, gi], cim_ref[0, gi]
        btre, btim = btre_ref[0, gi], btim_ref[0, gi]
        q_re, q_im = [], []
        col = slice(gi * p_, (gi + 1) * p_)
        ch = slice(gi * S5_GROUP, (gi + 1) * S5_GROUP)
        for t in range(c):
            er, ei = _cmul(pows[t][0], pows[t][1], fr, fi)
            qr, qi = _cmul(cre, cim, er, ei)
            q_re.append(qr)
            q_im.append(qi)
            er, ei = _cmul(pows[c - 1 - t][0], pows[c - 1 - t][1], fr, fi)
            zr, zi = _cmul(btre, btim, er, ei)
            rows = slice(t * LANES + gi * S5_GROUP, t * LANES + (gi + 1) * S5_GROUP)
            bsre_ref[0, rows, col] = zr.astype(BF16)
            bsim_ref[0, rows, col] = zi.astype(BF16)
            kr, ki = pows[t + 1]
            csre_ref[0, rows, col] = (cre * kr - cim * ki).astype(BF16)
            csim_ref[0, rows, col] = (-cre * ki - cim * kr).astype(BF16)
        q_re = jnp.concatenate(q_re, axis=0)
        q_im = jnp.concatenate(q_im, axis=0)
        m = _dg3(btre, q_re, _NT) - _dg3(btim, q_im, _NT)
        for t in range(c):
            blk = m[:, t * S5_GROUP:(t + 1) * S5_GROUP]
            if t == 0:
                blk = blk + jnp.where(r16 == c16, d_ref[0, gi], 0.0)
            tiles_ref[0, t, ch, ch] = blk.astype(BF16)
        for t in range(c + 1):
            lamre_ref[0, t, :, col] = pows[t][0]
            lamim_ref[0, t, :, col] = pows[t][1]


def s5_prep(prm, chunk):
    g, p_ = S5_GROUPS, S5_STATE
    gb = LANES // S5_GROUP
    nblk = g // gb
    kc = chunk * LANES
    grp = lambda x: x.reshape((nblk, gb) + x.shape[1:])
    args = [grp(prm['lam_re']), grp(prm['lam_im']), grp(prm['log_dt'].reshape(g, 1)),
            grp(prm['c_re']), grp(prm['c_im']),
            grp(jnp.swapaxes(prm['b_re'], 1, 2)), grp(jnp.swapaxes(prm['b_im'], 1, 2)),
            grp(prm['d'].reshape(g, S5_GROUP, 1))]
    blk = lambda x: pl.BlockSpec((1,) + x.shape[1:], lambda i: (i,) + (0,) * (x.ndim - 1))
    out_shape = [jax.ShapeDtypeStruct((nblk, chunk, LANES, LANES), BF16)] + \
        [jax.ShapeDtypeStruct((nblk, kc, gb * p_), BF16)] * 4 + \
        [jax.ShapeDtypeStruct((nblk, chunk + 1, 1, gb * p_), F32)] * 2
    return pl.pallas_call(
        functools.partial(_s5_prep_kernel, chunk=chunk),
        grid=(nblk,),
        in_specs=[blk(a) for a in args],
        out_specs=[blk(o) for o in out_shape],
        out_shape=out_shape,
        compiler_params=pltpu.CompilerParams(dimension_semantics=("parallel",)),
        name="s5_prep",
    )(*args)


def _s5_main_kernel(u_ref, tiles_ref, bsre_ref, bsim_ref, csre_ref, csim_ref, lamre_ref, lamim_ref, x0re_ref,
                    x0im_ref, y_ref, xre_ref, xim_ref, uc_scr, tp_scr, zre_scr, zim_scr, sre_scr, sim_scr,
                    xr_scr, xi_scr, *, chunk):
    c = chunk
    nb, tt, _ = u_ref.shape
    nct = tt // c
    rows = nct * nb
    j = pl.program_id(1)

    @pl.when(j == 0)
    def _():
        xr_scr[...] = x0re_ref[...]
        xi_scr[...] = x0im_ref[...]
        for d in range(c // 2):
            tp_scr[d, :LANES, :LANES] = tiles_ref[0, 2 * d]
            tp_scr[d, :LANES, LANES:] = tiles_ref[0, 2 * d + 1]
            tp_scr[d, LANES:, LANES:] = tiles_ref[0, 2 * d]
            tp_scr[d, LANES:, :LANES] = tiles_ref[0, 2 * d - 1] if d > 0 else jnp.zeros((LANES, LANES), BF16)

    lr, li = lamre_ref[0, 0], lamim_ref[0, 0]
    pair = lambda k: slice(2 * k * LANES, (2 * k + 2) * LANES)
    xr, xi = xr_scr[...], xi_scr[...]
    pieces = S5_PIECES if nct % S5_PIECES == 0 else 1
    nh = nct // pieces
    for piece in range(pieces):
        n0 = piece * nh
        hs = slice(n0 * nb, (n0 + nh) * nb)
        for t in range(c):
            if nct == 1:
                uc_scr[t] = u_ref[:, t, :]
            else:
                for b in range(nb):
                    uc_scr[t, pl.ds(n0 * nb + b, nh, stride=nb), :] = u_ref[b, pl.ds(n0 * c + t, nh, stride=c), :]
        u = jnp.concatenate([uc_scr[t, hs, :].astype(BF16) for t in range(c)], axis=1)
        zre_scr[hs, :] = jnp.dot(u, bsre_ref[0], preferred_element_type=F32)
        zim_scr[hs, :] = jnp.dot(u, bsim_ref[0], preferred_element_type=F32)

        for k in range(c // 2):
            y = jnp.dot(u[:, pair(0)], tp_scr[k], preferred_element_type=F32)
            for kin in range(1, k + 1):
                y = y + jnp.dot(u[:, pair(kin)], tp_scr[k - kin], preferred_element_type=F32)
            uc_scr[2 * k, hs, :] = y[:, :LANES]
            uc_scr[2 * k + 1, hs, :] = y[:, LANES:]

        for i in range(n0, n0 + nh):
            rs = slice(i * nb, (i + 1) * nb)
            sre_scr[rs, :] = xr
            sim_scr[rs, :] = xi
            xr, xi = lr * xr - li * xi + zre_scr[rs, :], li * xr + lr * xi + zim_scr[rs, :]
        sre = sre_scr[hs, :].astype(BF16)
        sim = sim_scr[hs, :].astype(BF16)
        for k in range(c // 2):
            y = (lax.dot_general(sre, csre_ref[0, pair(k), :], _NT, preferred_element_type=F32)
                 + lax.dot_general(sim, csim_ref[0, pair(k), :], _NT, preferred_element_type=F32))
            uc_scr[2 * k, hs, :] = jax.nn.gelu(uc_scr[2 * k, hs, :] + y[:, :LANES])
            uc_scr[2 * k + 1, hs, :] = jax.nn.gelu(uc_scr[2 * k + 1, hs, :] + y[:, LANES:])
        for t in range(c):
            if nct == 1:
                y_ref[:, t, :] = uc_scr[t]
            else:
                for b in range(nb):
                    y_ref[b, pl.ds(n0 * c + t, nh, stride=c), :] = uc_scr[t, pl.ds(n0 * nb + b, nh, stride=nb), :]
    xr_scr[...] = xr
    xi_scr[...] = xi

    @pl.when(j == pl.num_programs(1) - 1)
    def _():
        xre_ref[...] = xr
        xim_ref[...] = xi


def s5_mix(u, x0re, x0im, tables, chunk):
    b, t, e = u.shape
    tiles, bsre, bsim, csre, csim, lamre, lamim = tables
    nblk = e // LANES
    gp = (LANES // S5_GROUP) * S5_STATE
    nct = min(t // chunk, max(1, S5_ROWS // b))
    assert (t // chunk) % nct == 0
    tt = nct * chunk
    rows = nct * b
    kc = chunk * LANES
    last = tiles.shape[1] // chunk - 1
    assert tiles.shape[1] % chunk == 0
    bs_spec = pl.BlockSpec((1, kc, gp), lambda i, j: (i, last, 0))
    cs_spec = pl.BlockSpec((1, kc, gp), lambda i, j: (i, 0, 0))
    lam_spec = pl.BlockSpec((1, 1, 1, gp), lambda i, j: (i, chunk, 0, 0))
    y, xre, xim = pl.pallas_call(
        functools.partial(_s5_main_kernel, chunk=chunk),
        grid=(nblk, t // tt),
        in_specs=[pl.BlockSpec((b, tt, LANES), lambda i, j: (0, j, i)),
                  pl.BlockSpec((1, chunk, LANES, LANES), lambda i, j: (i, 0, 0, 0)),
                  bs_spec, bs_spec, cs_spec, cs_spec, lam_spec, lam_spec,
                  pl.BlockSpec((b, gp), lambda i, j: (0, i)), pl.BlockSpec((b, gp), lambda i, j: (0, i))],
        out_specs=[pl.BlockSpec((b, tt, LANES), lambda i, j: (0, j, i)),
                   pl.BlockSpec((b, gp), lambda i, j: (0, i)), pl.BlockSpec((b, gp), lambda i, j: (0, i))],
        out_shape=[jax.ShapeDtypeStruct((b, t, e), F32),
                   jax.ShapeDtypeStruct((b, S5_GROUPS * S5_STATE), F32),
                   jax.ShapeDtypeStruct((b, S5_GROUPS * S5_STATE), F32)],
        scratch_shapes=[pltpu.VMEM((chunk, rows, LANES), F32), pltpu.VMEM((chunk // 2, 2 * LANES, 2 * LANES), BF16)]
        + [pltpu.VMEM((rows, gp), F32)] * 4 + [pltpu.VMEM((b, gp), F32)] * 2,
        compiler_params=pltpu.CompilerParams(dimension_semantics=("parallel", "arbitrary"),
                                             vmem_limit_bytes=VMEM_LIMIT),
        name="s5_main",
    )(u, tiles, bsre, bsim, csre, csim, lamre, lamim, x0re.reshape(b, -1), x0im.reshape(b, -1))
    return y, xre.reshape(b, S5_GROUPS, S5_STATE), xim.reshape(b, S5_GROUPS, S5_STATE)


def _trunk(x, st_shift, st_wkv, st_gla, st_re, st_im, w):
    b, t, d = x.shape
    m = b * t
    x2 = x.reshape(m, d)

    p_r, p_g, p_xa = norm_proj(x2, w['norm_mix_pre'][0], [w['w_in_rwkv'], w['w_in_gla'], w['w_in_xa']],
                               [BF16, BF16, BF16])
    o_r, wkv = rwkv_mix(p_r.reshape(b, t, -1), st_shift[0], st_wkv, w['rwkv'])
    o_g, gla = gla_mix(p_g.reshape(b, t, -1), p_xa.reshape(b, t, -1), st_gla[0], w['gla_wa2'], w['gla_ba'],
                       w['gla_norm'])
    shift = p_r.reshape(b, t, -1)[:, -1].astype(F32)
    x2 = proj_post([o_r.reshape(m, -1), o_g.reshape(m, -1)], [w['w_out_rwkv'], w['w_out_gla']],
                   w['norm_mix_post'][0], x2)
    x2 = ffn(x2, w['norm_ffn_pre'][0], w['w_ff_up'][0], w['w_ff_down'][0], w['norm_ffn_post'][0])

    (u,) = norm_proj(x2, w['norm_mix_pre'][1], [w['s5_w_in']], [F32])
    chunk = min(S5_CHUNK, t)
    y, s_re, s_im = s5_mix(u.reshape(b, t, d), st_re[0], st_im[0], w['s5_tables'], chunk)
    x2 = proj_post([y.reshape(m, d)], [w['s5_w_out']], w['norm_mix_post'][1], x2, glu=True)
    x2 = ffn(x2, w['norm_ffn_pre'][1], w['w_ff_up'][1], w['w_ff_down'][1], w['norm_ffn_post'][1])
    return x2.reshape(b, t, d), shift[None], wkv[None], gla[None], s_re[None], s_im[None]


def kernel(x_prompt, x_sample, state_rwkv_shift, state_rwkv_wkv, state_gla, state_s5_re, state_s5_im,
           norm_mix_pre, norm_mix_post, norm_ffn_pre, norm_ffn_post,
           w_mix_in, w_mix_out, rwkv_mu, rwkv_w0, rwkv_w2, rwkv_a0, rwkv_a2, rwkv_g2,
           rwkv_k_k, rwkv_k_a, rwkv_r_k, rwkv_gn_g, rwkv_gn_b, gla_wa2, gla_ba, gla_norm,
           s5_w_in, s5_lam_re, s5_lam_im, s5_log_dt, s5_b_re, s5_b_im, s5_c_re, s5_c_im, s5_d, s5_w_out,
           w_ff_up, w_ff_down):
    assert norm_mix_pre.shape[0] == 2, "two layers: one RWKV-7/GLA layer, one S5 layer"
    bf = lambda a: a.astype(BF16)
    w_in = w_mix_in[0]
    gla0 = RWKV_PROJ
    xa0 = gla0 + 2 * GLA_K + GLA_V
    w_in_gla = jnp.concatenate([w_in[:, gla0:xa0], w_in[:, xa0 + GLA_LORA:]], axis=1)
    w_in_xa = jnp.pad(w_in[:, xa0:xa0 + GLA_LORA], ((0, 0), (0, LANES - GLA_LORA)))
    s5_prm = dict(lam_re=s5_lam_re[0], lam_im=s5_lam_im[0], log_dt=s5_log_dt[0], b_re=s5_b_re[0],
                  b_im=s5_b_im[0], c_re=s5_c_re[0], c_im=s5_c_im[0], d=s5_d[0])
    tables = s5_prep(s5_prm, min(S5_CHUNK, max(x_prompt.shape[1], x_sample.shape[1])))
    w = dict(
        norm_mix_pre=norm_mix_pre, norm_mix_post=norm_mix_post, norm_ffn_pre=norm_ffn_pre,
        norm_ffn_post=norm_ffn_post,
        w_in_rwkv=bf(w_in[:, :RWKV_PROJ]), w_in_gla=bf(w_in_gla), w_in_xa=bf(w_in_xa),
        w_out_rwkv=bf(w_mix_out[0, :RWKV_W]), w_out_gla=bf(w_mix_out[0, RWKV_W:]),
        rwkv=dict(mu=rwkv_mu[0], w0=rwkv_w0[0], w2=rwkv_w2[0], a0=rwkv_a0[0], a2=rwkv_a2[0], g2=rwkv_g2[0],
                  k_k=rwkv_k_k[0], k_a=rwkv_k_a[0], r_k=rwkv_r_k[0], gn_g=rwkv_gn_g[0], gn_b=rwkv_gn_b[0]),
        gla_wa2=bf(jnp.pad(gla_wa2[0], ((0, LANES - GLA_LORA), (0, 0)))), gla_ba=gla_ba[0], gla_norm=gla_norm[0],
        s5_w_in=bf(s5_w_in[0]), s5_w_out=bf(s5_w_out[0]), s5_tables=tables,
        w_ff_up=[bf(w_ff_up[l]) for l in range(2)], w_ff_down=[bf(w_ff_down[l]) for l in range(2)],
    )
    bp = x_prompt.shape[0]
    zeros = lambda s: jnp.zeros((s.shape[0], bp) + s.shape[2:], x_prompt.dtype)
    y_p, sh_p, wkv_p, gla_p, re_p, im_p = _trunk(x_prompt, zeros(state_rwkv_shift), zeros(state_rwkv_wkv),
                                                 zeros(state_gla), zeros(state_s5_re), zeros(state_s5_im), w)
    y_s, sh_s, wkv_s, gla_s, re_s, im_s = _trunk(x_sample, state_rwkv_shift, state_rwkv_wkv, state_gla,
                                                 state_s5_re, state_s5_im, w)
    return (y_p, y_s, sh_p, sh_s, wkv_p, wkv_s, gla_p, gla_s, re_p, re_s, im_p, im_s)
```
